```python
import math
import jax, jax.numpy as jnp
from jax import lax
import numpy as np

D_MODEL = 1024
BATCH = 8
SEQ = 2048
DEPTH = 4
DEC_BATCH = 128
DEC_SEQ = 8
PAST_LEN = 2048
PAGE_SIZE = 128

EPS = 1e-6
NEG_INF = -1e30
CHUNK = 128
A_GROUPS = 4
A_GROUP_DIM = 128
A_WIDTH = A_GROUPS * A_GROUP_DIM
NSA_HEADS = 8
NSA_KV_HEADS = 2
NSA_REP = NSA_HEADS // NSA_KV_HEADS
HEAD_DIM = 64
NSA_WIDTH = NSA_HEADS * HEAD_DIM
L_CMP = 32
CMP_STRIDE = 16
L_SEL = 64
N_SEL = 8
WINDOW = 256
FORCE_BONUS = 1e3
WIN_Q_BLOCK = 128
SEL_Q_BLOCK = 64
MEM_LEN = 256
MEM_HEADS = 4
MEM_HEAD_DIM = 128
MEM_WIDTH = MEM_HEADS * MEM_HEAD_DIM
NUM_BUCKETS = 32
MAX_DISTANCE = 128
D_FF = 4 * D_MODEL
N_KV_SLOTS = 6
OFF_Q = 2 * A_WIDTH
OFF_KV = OFF_Q + NSA_WIDTH
OFF_NG = OFF_KV + N_KV_SLOTS * NSA_KV_HEADS * HEAD_DIM
OFF_MQ = OFF_NG + 3 * NSA_HEADS
OFF_BG = OFF_MQ + MEM_WIDTH
D_PROJ = OFF_BG + 3 * D_MODEL

kernel_name = "hybrid_gmlp_nsa_memory_decode_step"


def rms_norm(x, g):
    xf = x.astype(jnp.float32)
    y = xf * lax.rsqrt(jnp.mean(xf * xf, axis=-1, keepdims=True) + EPS)
    return (y * g.astype(jnp.float32)).astype(x.dtype)


def masked_softmax(logits, mask):
    logits = jnp.where(mask, logits, NEG_INF)
    p = jax.nn.softmax(logits, axis=-1)
    return jnp.where(jnp.any(mask, axis=-1, keepdims=True), p, 0.0)


def t5_bucket(dist):
    n = jnp.maximum(jnp.asarray(dist, jnp.int32), 0)
    max_exact = NUM_BUCKETS // 2
    nf = jnp.maximum(n, 1).astype(jnp.float32)
    large = max_exact + (jnp.log(nf / max_exact) / math.log(MAX_DISTANCE / max_exact)
                         * (NUM_BUCKETS - max_exact)).astype(jnp.int32)
    return jnp.where(n < max_exact, n, jnp.minimum(large, NUM_BUCKETS - 1))


def chunk_gmlp(z_a, g_v, w_s, b_s):
    z = jax.nn.gelu(z_a)
    u, v = z[..., :A_WIDTH], z[..., A_WIDTH:]
    v = rms_norm(v, g_v)
    B, T, _ = v.shape
    n_chunks = -(-T // CHUNK)
    pad = n_chunks * CHUNK - T
    vc = jnp.pad(v, ((0, 0), (0, pad), (0, 0))).reshape(B, n_chunks, CHUNK, A_GROUPS, A_GROUP_DIM)
    causal = jnp.tril(jnp.ones((CHUNK, CHUNK), dtype=bool))
    w = jnp.where(causal, w_s, 0)
    s = jnp.einsum('gij,bcjgd->bcigd', w, vc) + b_s.T[None, None, :, :, None]
    s = s.reshape(B, n_chunks * CHUNK, A_WIDTH)[:, :T]
    return u * s, v


def compressed_selected_attention(q, pos0, kv_all, pe_cmp, w_phi, rel_table):
    B, Tq, H, hd = q.shape
    T = kv_all.shape[1]
    G, R = NSA_KV_HEADS, NSA_REP
    scale = HEAD_DIM ** -0.5
    dtype = q.dtype
    n_cmp = (T - L_CMP) // CMP_STRIDE + 1
    idx = np.arange(n_cmp)[:, None] * CMP_STRIDE + np.arange(L_CMP)[None, :]
    blocks = kv_all[:, idx, :2] + pe_cmp[None, None, :, :, None, :]
    cmp = jnp.einsum('bjlcgd,lcde->bjcge', blocks, w_phi)
    kc, vc = cmp[:, :, 0], cmp[:, :, 1]
    block_end = jnp.asarray(np.arange(n_cmp) * CMP_STRIDE + L_CMP - 1, jnp.int32)
    n_slc = -(-T // L_SEL)
    n_top = min(N_SEL, n_slc)
    pad_s = n_slc * L_SEL - T
    ks = jnp.pad(kv_all[:, :, 2:4], ((0, 0), (0, pad_s), (0, 0), (0, 0), (0, 0)))
    ks = ks.reshape(B, n_slc, L_SEL, 2, G, hd).transpose(0, 4, 1, 2, 3, 5)
    jj = np.arange(n_cmp)[:, None]
    ss = np.arange(n_slc)[None, :]
    overlap = jnp.asarray(((jj * CMP_STRIDE <= (ss + 1) * L_SEL - 1)
                           & (jj * CMP_STRIDE + L_CMP - 1 >= ss * L_SEL)).astype(np.float32))
    s_idx = jnp.arange(n_slc, dtype=jnp.int32)
    b_ix = jnp.arange(B)[:, None, None, None]
    g_ix = jnp.arange(G)[None, :, None, None]
    table_gr = rel_table.reshape(NUM_BUCKETS, G, R)

    qb = math.gcd(Tq, SEL_Q_BLOCK)
    nb = Tq // qb
    q_blocks = q.reshape(B, nb, qb, G, R, hd).transpose(1, 0, 2, 3, 4, 5)
    pos_blocks = (pos0 + jnp.arange(Tq, dtype=jnp.int32)).reshape(nb, qb)

    def one_block(args):
        qblk, pos = args
        dist_c = pos[:, None] - block_end[None, :]
        bias_c = rel_table[t5_bucket(dist_c)].reshape(qb, n_cmp, G, R).transpose(2, 3, 0, 1)
        s_c = jnp.einsum('bqgrd,bjgd->bgrqj', qblk, kc).astype(jnp.float32) * scale + bias_c.astype(jnp.float32)
        p_c = masked_softmax(s_c, dist_c >= 0)
        o_c = jnp.einsum('bgrqj,bjgd->bqgrd', p_c.astype(dtype), vc)
        imp = jnp.einsum('bgrqj,js->bgqs', p_c, overlap)
        cur = pos // L_SEL
        valid = s_idx[None, :] * L_SEL <= pos[:, None]
        forced = (s_idx[None, :] == 0) | (s_idx[None, :] == cur[:, None]) | (s_idx[None, :] == cur[:, None] - 1)
        imp = jnp.where(valid, imp + jnp.where(forced, FORCE_BONUS, 0.0), -1.0)
        _, sel = lax.top_k(imp, n_top)
        gathered = ks[b_ix, g_ix, sel].reshape(B, G, qb, n_top * L_SEL, 2, hd)
        kg, vg = gathered[..., 0, :], gathered[..., 1, :]
        kpos = (sel[..., None] * L_SEL + jnp.arange(L_SEL, dtype=jnp.int32)).reshape(B, G, qb, n_top * L_SEL)
        dist_s = pos[None, None, :, None] - kpos
        bias_s = jnp.moveaxis(table_gr[t5_bucket(dist_s), g_ix], -1, 2)
        s_s = jnp.einsum('bqgrd,bgqkd->bgrqk', qblk, kg).astype(jnp.float32) * scale + bias_s.astype(jnp.float32)
        p_s = masked_softmax(s_s, (dist_s >= 0)[:, :, None])
        o_s = jnp.einsum('bgrqk,bgqkd->bqgrd', p_s.astype(dtype), vg)
        return o_c, o_s

    o_cmp, o_slc = lax.map(one_block, (q_blocks, pos_blocks))
    o_cmp = o_cmp.transpose(1, 0, 2, 3, 4, 5).reshape(B, Tq, H, hd)
    o_slc = o_slc.transpose(1, 0, 2, 3, 4, 5).reshape(B, Tq, H, hd)
    return o_cmp, o_slc


def window_attention(q, pos0, win_all, win_pos0, rel_table):
    B, Tq, H, hd = q.shape
    Tw = win_all.shape[1]
    G, R = NSA_KV_HEADS, NSA_REP
    scale = HEAD_DIM ** -0.5
    qb = math.gcd(Tq, WIN_Q_BLOCK)
    nb = Tq // qb
    kpad = jnp.pad(win_all, ((0, 0), (WINDOW, 0), (0, 0), (0, 0), (0, 0)))
    kpos_pad = win_pos0 - WINDOW + np.arange(Tw + WINDOW)
    starts = (pos0 - win_pos0) + np.arange(nb) * qb
    idx = starts[:, None] + np.arange(qb + WINDOW)[None, :]
    kvb = kpad[:, idx]
    kb, vb = kvb[:, :, :, 0], kvb[:, :, :, 1]
    kp = kpos_pad[idx]
    qpos = pos0 + np.arange(Tq).reshape(nb, qb)
    dist = qpos[:, :, None] - kp[:, None, :]
    mask = (dist >= 0) & (dist < WINDOW) & (kp[:, None, :] >= 0)
    bias = rel_table[t5_bucket(dist)].reshape(nb, qb, qb + WINDOW, G, R).transpose(0, 3, 4, 1, 2)
    qr = q.reshape(B, nb, qb, G, R, hd)
    s = jnp.einsum('bnqgrd,bnkgd->bngrqk', qr, kb).astype(jnp.float32) * scale + bias[None].astype(jnp.float32)
    p = masked_softmax(s, jnp.asarray(mask)[None, :, None, None])
    o = jnp.einsum('bngrqk,bnkgd->bnqgrd', p.astype(q.dtype), vb)
    return o.reshape(B, Tq, H, hd)


def memory_attention(q, mem_kv):
    s = jnp.einsum('bqhd,bkhd->bhqk', q, mem_kv[:, :, 0]).astype(jnp.float32) * (MEM_HEAD_DIM ** -0.5)
    p = jax.nn.softmax(s, axis=-1)
    return jnp.einsum('bhqk,bkhd->bqhd', p.astype(q.dtype), mem_kv[:, :, 1])


def trunk_layer(x, pos0, past_kv, past_win, mem_kv, lw, rel_table):
    ln1, w_in, g_v, w_s, b_s, w_a, pe_cmp, w_phi, w_b, w_c, w_out, ln2, w_up, w_down = lw
    B, T, _ = x.shape
    h = rms_norm(x, ln1)
    z = h @ w_in
    y_a, v_rows = chunk_gmlp(z[..., :OFF_Q], g_v, w_s, b_s)
    q = z[..., OFF_Q:OFF_KV].reshape(B, T, NSA_HEADS, HEAD_DIM)
    kv_new = z[..., OFF_KV:OFF_NG].reshape(B, T, N_KV_SLOTS, NSA_KV_HEADS, HEAD_DIM)
    paged_new, win_new = kv_new[:, :, :4], kv_new[:, :, 4:]
    if past_kv is None:
        kv_all, win_all, win_pos0 = paged_new, win_new, 0
        win_keep = min(WINDOW, T)
    else:
        kv_all = jnp.concatenate([past_kv, paged_new], axis=1)
        win_all = jnp.concatenate([past_win, win_new], axis=1)
        win_pos0 = pos0 - past_win.shape[1]
        win_keep = past_win.shape[1]
    o_cmp, o_slc = compressed_selected_attention(q, pos0, kv_all, pe_cmp, w_phi, rel_table)
    o_win = window_attention(q, pos0, win_all, win_pos0, rel_table)
    ng = jax.nn.sigmoid(z[..., OFF_NG:OFF_MQ]).reshape(B, T, NSA_HEADS, 3, 1)
    o_nsa = (ng[..., 0, :] * o_cmp + ng[..., 1, :] * o_slc + ng[..., 2, :] * o_win).reshape(B, T, NSA_WIDTH)
    qm = z[..., OFF_MQ:OFF_BG].reshape(B, T, MEM_HEADS, MEM_HEAD_DIM)
    o_mem = memory_attention(qm, mem_kv).reshape(B, T, MEM_WIDTH)
    bg = jax.nn.sigmoid(z[..., OFF_BG:]).reshape(B, T, 3, D_MODEL)
    merged = bg[:, :, 0] * (y_a @ w_a) + bg[:, :, 1] * (o_nsa @ w_b) + bg[:, :, 2] * (o_mem @ w_c)
    x = x + merged @ w_out
    h2 = rms_norm(x, ln2)
    x = x + jnp.square(jax.nn.relu(h2 @ w_up)) @ w_down
    return x, paged_new, win_all[:, -win_keep:], v_rows


def setup_inputs(seed: int = 0) -> dict:
    key = jax.random.key(seed)
    ks = jax.random.split(key, 26)
    n_pages = PAST_LEN // PAGE_SIZE
    n_used = DEC_BATCH * n_pages
    n_pool = n_used + n_used // 4
    win_buf = min(WINDOW, PAST_LEN)

    def nrm(k, shape, scale):
        return scale * jax.random.normal(k, shape, jnp.float32)

    def gain(k, shape):
        return 1.0 + 0.02 * jax.random.normal(k, shape, jnp.float32)

    page_table = jax.random.permutation(ks[5], n_pool)[:n_used].reshape(DEC_BATCH, n_pages).astype(jnp.int32)
    return {
        "x_prompt": nrm(ks[0], (BATCH, SEQ, D_MODEL), 1.0),
        "x_sample": nrm(ks[1], (DEC_BATCH, DEC_SEQ, D_MODEL), 1.0),
        "cache_kv": nrm(ks[2], (DEPTH, n_pool, PAGE_SIZE, 4, NSA_KV_HEADS, HEAD_DIM), 1.0),
        "cache_win_kv": nrm(ks[3], (DEPTH, DEC_BATCH, win_buf, 2, NSA_KV_HEADS, HEAD_DIM), 1.0),
        "cache_mem_kv": nrm(ks[4], (DEPTH, DEC_BATCH, MEM_LEN, 2, MEM_HEADS, MEM_HEAD_DIM), 1.0),
        "page_table": page_table,
        "mem_prompt": nrm(ks[6], (BATCH, MEM_LEN, D_MODEL), 1.0),
        "ln1": gain(ks[7], (DEPTH, D_MODEL)),
        "w_in": nrm(ks[8], (DEPTH, D_MODEL, D_PROJ), D_MODEL ** -0.5),
        "g_v": gain(ks[9], (DEPTH, A_WIDTH)),
        "w_s": nrm(ks[10], (DEPTH, A_GROUPS, CHUNK, CHUNK), CHUNK ** -0.5),
        "b_s": gain(ks[11], (DEPTH, A_GROUPS, CHUNK)),
        "w_a": nrm(ks[12], (DEPTH, A_WIDTH, D_MODEL), A_WIDTH ** -0.5),
        "pe_cmp": nrm(ks[13], (DEPTH, L_CMP, 2, HEAD_DIM), 0.02),
        "w_phi": nrm(ks[14], (DEPTH, L_CMP, 2, HEAD_DIM, HEAD_DIM), (L_CMP * HEAD_DIM) ** -0.5),
        "w_b": nrm(ks[15], (DEPTH, NSA_WIDTH, D_MODEL), NSA_WIDTH ** -0.5),
        "ln_mem": gain(ks[16], (DEPTH, D_MODEL)),
        "w_mem_kv": nrm(ks[17], (DEPTH, D_MODEL, 2 * MEM_WIDTH), D_MODEL ** -0.5),
        "w_c": nrm(ks[18], (DEPTH, MEM_WIDTH, D_MODEL), MEM_WIDTH ** -0.5),
        "w_out": nrm(ks[19], (DEPTH, D_MODEL, D_MODEL), D_MODEL ** -0.5),
        "ln2": gain(ks[20], (DEPTH, D_MODEL)),
        "w_up": nrm(ks[21], (DEPTH, D_MODEL, D_FF), D_MODEL ** -0.5),
        "w_down": nrm(ks[22], (DEPTH, D_FF, D_MODEL), D_FF ** -0.5),
        "rel_bias": nrm(ks[23], (NUM_BUCKETS, NSA_HEADS), 0.3),
        "ln_f": gain(ks[24], (D_MODEL,)),
    }


def reference(x_prompt, x_sample, cache_kv, cache_win_kv, cache_mem_kv, page_table, mem_prompt,
              ln1, w_in, g_v, w_s, b_s, w_a, pe_cmp, w_phi, w_b, ln_mem, w_mem_kv, w_c, w_out,
              ln2, w_up, w_down, rel_bias, ln_f):
    past_len = page_table.shape[1] * cache_kv.shape[2]
    n_req = x_sample.shape[0]
    n_mem = mem_prompt.shape[1]
    xp, xs = x_prompt, x_sample
    kv_p, win_p, mem_p, kv_s, win_s, v_s = [], [], [], [], [], []
    for l in range(DEPTH):
        lw = (ln1[l], w_in[l], g_v[l], w_s[l], b_s[l], w_a[l], pe_cmp[l], w_phi[l], w_b[l],
              w_c[l], w_out[l], ln2[l], w_up[l], w_down[l])
        mem_kv = (rms_norm(mem_prompt, ln_mem[l]) @ w_mem_kv[l]).reshape(
            mem_prompt.shape[0], n_mem, 2, MEM_HEADS, MEM_HEAD_DIM)
        xp, new_kv, new_win, _ = trunk_layer(xp, 0, None, None, mem_kv, lw, rel_bias)
        kv_p.append(new_kv)
        win_p.append(new_win)
        mem_p.append(mem_kv)
        past = cache_kv[l][page_table].reshape(n_req, past_len, 4, NSA_KV_HEADS, HEAD_DIM)
        xs, new_kv, new_win, new_v = trunk_layer(xs, past_len, past, cache_win_kv[l], cache_mem_kv[l], lw, rel_bias)
        kv_s.append(new_kv)
        win_s.append(new_win)
        v_s.append(new_v)
    y_prompt = rms_norm(xp, ln_f)
    y_sample = rms_norm(xs, ln_f)
    return (y_prompt, y_sample, jnp.stack(kv_p), jnp.stack(win_p), jnp.stack(mem_p),
            jnp.stack(kv_s), jnp.stack(win_s), jnp.stack(v_s))
```

```python
import functools
import math

import numpy as np
import jax
import jax.numpy as jnp
from jax import lax
from jax.experimental import pallas as pl
from jax.experimental.pallas import tpu as pltpu

F32 = jnp.float32
BF16 = jnp.bfloat16

EPS = 1e-6
NEG_INF = -1e30
LOWEST = -3e38

D_MODEL = 1024
DEPTH = 4
PAGE = 128
CHUNK = 128
A_GROUPS = 4
A_WIDTH = 512
NSA_HEADS = 8
NSA_KV_HEADS = 2
NSA_REP = 4
HEAD_DIM = 64
L_CMP = 32
CMP_STRIDE = 16
L_SEL = 64
N_SEL = 8
WINDOW = 256
FORCE_BONUS = 1e3
MEM_HEADS = 4
MEM_HEAD_DIM = 128
MEM_WIDTH = 512
NUM_BUCKETS = 32
MAX_DISTANCE = 128
D_FF = 4096
OFF_Q = 1024
OFF_KV = 1536
OFF_NG = 2304
OFF_MQ = 2328
OFF_BG = 2840

TN = 512
N_TILES = 13
Z_COLS = TN * N_TILES
TILE_KINDS = ("gelu", "gelu_norm", "none", "none", "none", "half_sig",
              "sig", "sig", "sig", "sig", "sig", "sig", "none")
CB_Q = 1
CB_PAGED = 4
CB_WN = 5
CB_BG = 3
CB_MQ = 12
COL_PAGED = CB_PAGED * TN

BD_LEN = 256
TQ = 128
VMEM_LIMIT = 56 * 1024 * 1024


def _cparams(sem):
    return pltpu.CompilerParams(dimension_semantics=sem, vmem_limit_bytes=VMEM_LIMIT)


def _rms(x, g):
    return x * lax.rsqrt(jnp.mean(x * x, axis=-1, keepdims=True) + EPS) * g


def _dot(a, b):
    return jnp.dot(a, b, preferred_element_type=F32)


def _dot_hi(a, b):
    return jnp.dot(a, b, preferred_element_type=F32, precision=lax.Precision.HIGHEST)


def _dot_nt(a, b):
    return lax.dot_general(a, b, (((1,), (1,)), ((), ())), preferred_element_type=F32)


def _bd_kernel(relt_ref, o_ref):
    n = lax.broadcasted_iota(jnp.int32, (NSA_HEADS, BD_LEN), 1)
    max_exact = NUM_BUCKETS // 2
    nf = jnp.maximum(n, 1).astype(F32)
    large = max_exact + (jnp.log(nf / max_exact) / math.log(MAX_DISTANCE / max_exact)
                         * (NUM_BUCKETS - max_exact)).astype(jnp.int32)
    bucket = jnp.where(n < max_exact, n, jnp.minimum(large, NUM_BUCKETS - 1))
    out = jnp.zeros((NSA_HEADS, BD_LEN), F32)
    for b in range(NUM_BUCKETS):
        out = jnp.where(bucket == b, relt_ref[:, b:b + 1], out)
    o_ref[...] = out


def _bias_by_distance(rel_bias):
    return pl.pallas_call(
        _bd_kernel,
        out_shape=jax.ShapeDtypeStruct((NSA_HEADS, BD_LEN), F32),
        name="bias_by_distance",
    )(rel_bias.T)


def _bias_lookup(bd, dist, valid):
    idx = np.clip(dist, 0, BD_LEN - 1).astype(np.int32)
    vals = jnp.take(bd, jnp.asarray(idx.reshape(-1)), axis=1).reshape((bd.shape[0],) + dist.shape)
    return jnp.where(jnp.asarray(valid)[None], vals, NEG_INF)


def _norm_matmul_kernel(x_ref, g_ref, w_ref, gv_ref, o_ref, hn_ref, *, kinds):
    n = pl.program_id(1)

    @pl.when(n == 0)
    def _():
        hn_ref[...] = _rms(x_ref[...], g_ref[...]).astype(BF16)

    def value():
        return _dot(hn_ref[...], w_ref[...])

    for kind in sorted(set(kinds)):
        tiles = [i for i, k in enumerate(kinds) if k == kind]
        cond = functools.reduce(jnp.logical_or, [n == i for i in tiles])

        @pl.when(cond)
        def _(kind=kind):
            acc = value()
            if kind == "gelu":
                acc = jax.nn.gelu(acc)
            elif kind == "gelu_norm":
                acc = _rms(jax.nn.gelu(acc), gv_ref[...])
            elif kind == "sig":
                acc = jax.nn.sigmoid(acc)
            elif kind == "half_sig":
                half = acc.shape[1] // 2
                acc = jnp.concatenate([acc[:, :half], jax.nn.sigmoid(acc[:, half:])], axis=1)
            o_ref[...] = acc


def _norm_matmul(x, gain, w, gv, kinds, tm, name):
    m, d = x.shape
    n_tiles = len(kinds)
    tn = w.shape[1] // n_tiles
    return pl.pallas_call(
        functools.partial(_norm_matmul_kernel, kinds=kinds),
        grid=(m // tm, n_tiles),
        in_specs=[
            pl.BlockSpec((tm, d), lambda i, j: (i, 0)),
            pl.BlockSpec((1, d), lambda i, j: (0, 0)),
            pl.BlockSpec((d, tn), lambda i, j: (0, j)),
            pl.BlockSpec((1, tn), lambda i, j: (0, 0)),
        ],
        out_specs=pl.BlockSpec((tm, tn), lambda i, j: (i, j)),
        out_shape=jax.ShapeDtypeStruct((m, w.shape[1]), F32),
        scratch_shapes=[pltpu.VMEM((tm, d), BF16)],
        compiler_params=_cparams(("parallel", "arbitrary")),
        name=name,
    )(x, gain.reshape(1, d), w, gv.reshape(1, tn))


def _cmp_copies(pt_ref, src_ref, buf_ref, sem_ref, step, slot, *, nr, n_pages, col0):
    copies = []
    for r in range(nr):
        for p in range(n_pages):
            pid = pt_ref[(step * nr + r) * n_pages + p]
            for c in range(2):
                copies.append(pltpu.make_async_copy(
                    src_ref.at[pid, :, pl.ds(col0 + c * 128, 128)],
                    buf_ref.at[slot, r, c, pl.ds(p * PAGE, PAGE), :],
                    sem_ref.at[slot]))
    return copies


def _compress_kernel(pt_ref, src_ref, wc_ref, per_ref, wphi_ref, o_ref, buf_ref, sem_ref,
                     *, nr, n_pages, col0):
    step = pl.program_id(0)
    n_steps = pl.num_programs(0)
    slot = lax.rem(step, 2)
    kw = dict(nr=nr, n_pages=n_pages, col0=col0)

    @pl.when(step == 0)
    def _():
        for cp in _cmp_copies(pt_ref, src_ref, buf_ref, sem_ref, step, slot, **kw):
            cp.start()

    @pl.when(step + 1 < n_steps)
    def _():
        for cp in _cmp_copies(pt_ref, src_ref, buf_ref, sem_ref, step + 1, 1 - slot, **kw):
            cp.start()

    for cp in _cmp_copies(pt_ref, src_ref, buf_ref, sem_ref, step, slot, **kw):
        cp.wait()

    rows = n_pages * PAGE
    m = rows // CMP_STRIDE
    outs = []
    for c in range(2):
        per_req = []
        for r in range(nr):
            slabs = [buf_ref[slot, r, c, pl.ds(l, m, stride=CMP_STRIDE), :] for l in range(CMP_STRIDE)]
            per_req.append(jnp.concatenate(slabs, axis=1))
        lhs = jnp.concatenate(per_req, axis=0).astype(BF16)
        p = _dot(lhs, wc_ref[c])
        bias = _dot_hi(per_ref[c], wphi_ref[c])[0:1]
        hi = pltpu.roll(p[:, 128:], nr * m - 1, 0)
        outs.append(p[:, :128] + hi + bias)
    res = jnp.concatenate(outs, axis=1)
    o_ref[...] = res.reshape(nr, m, 256)


def _compress(page_table, src, wc, pe_rep, wphi2, *, col0, nr):
    nb, n_pages = page_table.shape
    rows = n_pages * PAGE
    m = rows // CMP_STRIDE
    grid_spec = pltpu.PrefetchScalarGridSpec(
        num_scalar_prefetch=1,
        grid=(nb // nr,),
        in_specs=[
            pl.BlockSpec(memory_space=pl.ANY),
            pl.BlockSpec((2, CMP_STRIDE * 128, 256), lambda i, pt: (0, 0, 0)),
            pl.BlockSpec((2, 8, L_CMP * HEAD_DIM), lambda i, pt: (0, 0, 0)),
            pl.BlockSpec((2, L_CMP * HEAD_DIM, 128), lambda i, pt: (0, 0, 0)),
        ],
        out_specs=pl.BlockSpec((nr, m, 256), lambda i, pt: (i, 0, 0)),
        scratch_shapes=[pltpu.VMEM((2, nr, 2, rows, 128), F32), pltpu.SemaphoreType.DMA((2,))],
    )
    return pl.pallas_call(
        functools.partial(_compress_kernel, nr=nr, n_pages=n_pages, col0=col0),
        grid_spec=grid_spec,
        out_shape=jax.ShapeDtypeStruct((nb, m, 256), F32),
        compiler_params=_cparams(("arbitrary",)),
        name="compress",
    )(page_table.reshape(-1), src, wc, pe_rep, wphi2)


def _stack_heads(q):
    return jnp.concatenate([q[:, h * 128:(h + 1) * 128] for h in range(NSA_HEADS)], axis=0)


def _softmax_rows(s):
    mx = jnp.max(s, axis=-1, keepdims=True)
    e = jnp.exp(s - mx)
    p = e / jnp.sum(e, axis=-1, keepdims=True)
    return jnp.where(mx > 0.5 * NEG_INF, p, 0.0)


def _select_blocks(imp, pos, n_slc):
    lane_i = lax.broadcasted_iota(jnp.int32, imp.shape, 1)
    lane = lane_i.astype(F32)
    cur = lax.shift_right_logical(pos, 6)
    valid = (lane_i * L_SEL) <= pos
    forced = (lane_i == 0) | (lane_i == cur) | (lane_i == cur - 1)
    vals = jnp.where(valid, imp + jnp.where(forced, FORCE_BONUS, 0.0), -1.0)
    vals = jnp.where(lane_i < n_slc, vals, LOWEST)
    sel = jnp.zeros(imp.shape, F32)
    for _ in range(min(N_SEL, n_slc)):
        mx = jnp.max(vals, axis=-1, keepdims=True)
        first = jnp.min(jnp.where(vals == mx, lane, 1e9), axis=-1, keepdims=True)
        pick = lane == first
        sel = jnp.where(pick, 1.0, sel)
        vals = jnp.where(pick, LOWEST, vals)
    return sel


def _merge_heads(o_branches, gates, rows):
    lane = lax.broadcasted_iota(jnp.int32, (rows, 128), 1)
    tiles = []
    for mtile in range(NSA_HEADS // 2):
        g = (2 * mtile) // NSA_REP
        acc = jnp.zeros((rows, 128), F32)
        for o, gate in zip(o_branches, gates):
            a = o[(2 * mtile) * rows:(2 * mtile + 1) * rows]
            b = o[(2 * mtile + 1) * rows:(2 * mtile + 2) * rows]
            if g == 1:
                a = pltpu.roll(a, 64, 1)
            else:
                b = pltpu.roll(b, 64, 1)
            acc = acc + gate[:, mtile * 128:(mtile + 1) * 128] * jnp.where(lane < 64, a, b)
        tiles.append(acc)
    return jnp.concatenate(tiles, axis=1)


def _nsa_prompt_kernel(q_ref, pg_ref, wn_ref, kvc_ref, bc_ref, toep_ref, ovl_ref, ef_ref, eg_ref,
                       o_ref, ks_ref, vs_ref, kw_ref, vw_ref, mf_ref, *, n_slc):
    qt = pl.program_id(1)
    t_len = pg_ref.shape[1]
    nj = kvc_ref.shape[1]
    rows = NSA_HEADS * TQ

    @pl.when(qt == 0)
    def _():
        ks_ref[...] = pg_ref[0, :, 256:384].astype(BF16)
        vs_ref[...] = pg_ref[0, :, 384:512].astype(BF16)
        kw_ref[...] = wn_ref[0, :, 0:128].astype(BF16)
        vw_ref[...] = wn_ref[0, :, 128:256].astype(BF16)

    q = _stack_heads(q_ref[0] * (HEAD_DIM ** -0.5)).astype(BF16)

    kc = kvc_ref[0, :, 0:128].astype(BF16)
    vc = kvc_ref[0, :, 128:256].astype(BF16)
    s_c = _dot_nt(q, kc).reshape(NSA_HEADS, TQ, nj) + bc_ref[...]
    p_c = _softmax_rows(s_c)
    o_c = _dot(p_c.reshape(rows, nj).astype(BF16), vc)
    p_g = jnp.sum(p_c.reshape(NSA_KV_HEADS, NSA_REP, TQ, nj), axis=1)
    pos = qt * TQ + lax.broadcasted_iota(jnp.int32, (TQ, 1), 0)
    for g in range(NSA_KV_HEADS):
        imp = _dot_hi(p_g[g], ovl_ref[...])
        sel = _select_blocks(imp, pos, n_slc)
        mf_ref[g] = _dot(sel.astype(BF16), ef_ref[...])

    def flash(k_ref, v_ref, lo, hi, table_of, use_mask):
        def body(kt, carry):
            m_i, l_i, acc = carry
            start = pl.multiple_of(kt * TQ, TQ)
            k = k_ref[pl.ds(start, TQ), :]
            v = v_ref[pl.ds(start, TQ), :]
            s = _dot_nt(q, k).reshape(NSA_HEADS, TQ, TQ) + toep_ref[table_of(qt - kt)]
            if use_mask:
                mk = mf_ref[:, :, pl.ds(start, TQ)]
                s4 = s.reshape(NSA_KV_HEADS, NSA_REP, TQ, TQ)
                s = jnp.where(mk[:, None] > 0.5, s4, NEG_INF).reshape(NSA_HEADS, TQ, TQ)
            m_new = jnp.maximum(m_i, jnp.max(s, axis=-1, keepdims=True))
            alpha = jnp.exp(m_i - m_new)
            p = jnp.exp(s - m_new)
            l_new = alpha * l_i + jnp.sum(p, axis=-1, keepdims=True)
            pv = _dot(p.reshape(rows, TQ).astype(BF16), v).reshape(NSA_HEADS, TQ, 128)
            return m_new, l_new, alpha * acc + pv

        init = (jnp.full((NSA_HEADS, TQ, 1), NEG_INF, F32), jnp.zeros((NSA_HEADS, TQ, 1), F32),
                jnp.zeros((NSA_HEADS, TQ, 128), F32))
        _, l_f, acc_f = lax.fori_loop(lo, hi, body, init)
        return (acc_f / l_f).reshape(rows, 128)

    o_s = flash(ks_ref, vs_ref, 0, qt + 1, lambda d: jnp.minimum(d, 2), True)
    o_w = flash(kw_ref, vw_ref, jnp.maximum(qt - WINDOW // TQ, 0), qt + 1,
                lambda d: jnp.where(d == 2, 3, d), False)

    start = pl.multiple_of(qt * TQ, TQ)
    ng = wn_ref[0, pl.ds(start, TQ), 256:384]
    gates = [_dot_hi(ng, eg_ref[t]) for t in range(3)]
    o_ref[0] = _merge_heads([o_c, o_s, o_w], gates, TQ)
    del t_len


def _nsa_prompt(z3, kvc, bias_c, toep, ovl, efull, eg):
    b, t, _ = z3.shape
    nj = kvc.shape[1]
    n_slc = -(-t // L_SEL)
    return pl.pallas_call(
        functools.partial(_nsa_prompt_kernel, n_slc=n_slc),
        grid=(b, t // TQ),
        in_specs=[
            pl.BlockSpec((1, TQ, 1024), lambda i, j: (i, j, CB_Q)),
            pl.BlockSpec((1, t, TN), lambda i, j: (i, 0, CB_PAGED)),
            pl.BlockSpec((1, t, TN), lambda i, j: (i, 0, CB_WN)),
            pl.BlockSpec((1, nj, 256), lambda i, j: (i, 0, 0)),
            pl.BlockSpec((NSA_HEADS, TQ, nj), lambda i, j: (0, j, 0)),
            pl.BlockSpec((4, NSA_HEADS, TQ, TQ), lambda i, j: (0, 0, 0, 0)),
            pl.BlockSpec((nj, 128), lambda i, j: (0, 0)),
            pl.BlockSpec((128, t), lambda i, j: (0, 0)),
            pl.BlockSpec((3, 128, 512), lambda i, j: (0, 0, 0)),
        ],
        out_specs=pl.BlockSpec((1, TQ, 512), lambda i, j: (i, j, 0)),
        out_shape=jax.ShapeDtypeStruct((b, t, 512), F32),
        scratch_shapes=[pltpu.VMEM((t, 128), BF16)] * 4 + [pltpu.VMEM((NSA_KV_HEADS, TQ, t), F32)],
        compiler_params=_cparams(("parallel", "arbitrary")),
        name="nsa_prompt",
    )(z3, z3, z3, kvc, bias_c, toep, ovl, efull, eg)


def _slc_copies(pt_ref, src_ref, buf_ref, sem_ref, req, slot, *, n_pages):
    copies = []
    for p in range(n_pages):
        pid = pt_ref[req * n_pages + p]
        for c in range(2):
            copies.append(pltpu.make_async_copy(
                src_ref.at[pid, :, pl.ds(256 + c * 128, 128)],
                buf_ref.at[slot, c, pl.ds(p * PAGE, PAGE), :],
                sem_ref.at[slot]))
    return copies


def _nsa_sample_kernel(pt_ref, src_ref, q_ref, pg_ref, wn_ref, cw_ref, kvc_ref, bs_ref, bn_ref, bw_ref,
                       bcs_ref, ovl_ref, ef_ref, eg_ref, o_ref, buf_ref, sem_ref,
                       *, n_pages, n_slc, pos0):
    req = pl.program_id(0)
    n_req = pl.num_programs(0)
    slot = lax.rem(req, 2)
    tq = q_ref.shape[1]
    rows = NSA_HEADS * tq
    nj = kvc_ref.shape[1]

    @pl.when(req == 0)
    def _():
        for cp in _slc_copies(pt_ref, src_ref, buf_ref, sem_ref, req, slot, n_pages=n_pages):
            cp.start()

    @pl.when(req + 1 < n_req)
    def _():
        for cp in _slc_copies(pt_ref, src_ref, buf_ref, sem_ref, req + 1, 1 - slot, n_pages=n_pages):
            cp.start()

    q = _stack_heads(q_ref[0] * (HEAD_DIM ** -0.5)).astype(BF16)

    kc = kvc_ref[0, :, 0:128].astype(BF16)
    vc = kvc_ref[0, :, 128:256].astype(BF16)
    p_c = _softmax_rows(_dot_nt(q, kc) + bcs_ref[...])
    o_c = _dot(p_c.astype(BF16), vc)
    p_g = jnp.sum(p_c.reshape(NSA_KV_HEADS, NSA_REP, tq, nj), axis=1).reshape(NSA_KV_HEADS * tq, nj)
    imp = _dot_hi(p_g, ovl_ref[...])
    pos = pos0 + lax.broadcasted_iota(jnp.int32, (NSA_KV_HEADS, tq, 1), 1).reshape(NSA_KV_HEADS * tq, 1)
    sel = _select_blocks(imp, pos, n_slc)
    mf = _dot(sel.astype(BF16), ef_ref[...])
    past = mf.shape[1]
    mf = jnp.broadcast_to(mf.reshape(NSA_KV_HEADS, 1, tq, past),
                          (NSA_KV_HEADS, NSA_REP, tq, past)).reshape(rows, past)

    def pad_rows(x):
        return jnp.concatenate([x, jnp.zeros((128 - tq, 128), F32)], axis=0).astype(BF16)

    def two_part(k_old, v_old, bias_old, k_new, v_new):
        s_o = _dot_nt(q, k_old) + bias_old
        s_n = _dot_nt(q, pad_rows(k_new)) + bn_ref[...]
        mx = jnp.maximum(jnp.max(s_o, axis=-1, keepdims=True), jnp.max(s_n, axis=-1, keepdims=True))
        e_o = jnp.exp(s_o - mx)
        e_n = jnp.exp(s_n - mx)
        den = jnp.sum(e_o, axis=-1, keepdims=True) + jnp.sum(e_n, axis=-1, keepdims=True)
        return (_dot(e_o.astype(BF16), v_old) + _dot(e_n.astype(BF16), pad_rows(v_new))) / den

    pg = pg_ref[0]
    wn = wn_ref[0]
    o_w = two_part(cw_ref[0, :, 0:128].astype(BF16), cw_ref[0, :, 128:256].astype(BF16), bw_ref[...],
                   wn[:, 0:128], wn[:, 128:256])

    for cp in _slc_copies(pt_ref, src_ref, buf_ref, sem_ref, req, slot, n_pages=n_pages):
        cp.wait()

    bias_s = jnp.where(mf > 0.5, bs_ref[...], NEG_INF)
    o_s = two_part(buf_ref[slot, 0].astype(BF16), buf_ref[slot, 1].astype(BF16), bias_s,
                   pg[:, 256:384], pg[:, 384:512])

    gates = [_dot_hi(wn[:, 256:384], eg_ref[t]) for t in range(3)]
    o_ref[0] = _merge_heads([o_c, o_s, o_w], gates, tq)


def _nsa_sample(page_table, cache_l, zs3, cwin, kvc, bias_s, bias_n, bias_w, bias_cs, ovl, efull, eg):
    nb, n_pages = page_table.shape
    past = n_pages * PAGE
    tq = zs3.shape[1]
    rows = NSA_HEADS * tq
    nj = kvc.shape[1]
    n_slc = -(-(past + tq) // L_SEL)
    wb = cwin.shape[1]
    const2 = lambda i, pt: (0, 0)
    grid_spec = pltpu.PrefetchScalarGridSpec(
        num_scalar_prefetch=1,
        grid=(nb,),
        in_specs=[
            pl.BlockSpec(memory_space=pl.ANY),
            pl.BlockSpec((1, tq, 1024), lambda i, pt: (i, 0, CB_Q)),
            pl.BlockSpec((1, tq, TN), lambda i, pt: (i, 0, CB_PAGED)),
            pl.BlockSpec((1, tq, TN), lambda i, pt: (i, 0, CB_WN)),
            pl.BlockSpec((1, wb, 256), lambda i, pt: (i, 0, 0)),
            pl.BlockSpec((1, nj, 256), lambda i, pt: (i, 0, 0)),
            pl.BlockSpec((rows, past), const2),
            pl.BlockSpec((rows, 128), const2),
            pl.BlockSpec((rows, wb), const2),
            pl.BlockSpec((rows, nj), const2),
            pl.BlockSpec((nj, 128), const2),
            pl.BlockSpec((128, past), const2),
            pl.BlockSpec((3, 128, 512), lambda i, pt: (0, 0, 0)),
        ],
        out_specs=pl.BlockSpec((1, tq, 512), lambda i, pt: (i, 0, 0)),
        scratch_shapes=[pltpu.VMEM((2, 2, past, 128), F32), pltpu.SemaphoreType.DMA((2,))],
    )
    return pl.pallas_call(
        functools.partial(_nsa_sample_kernel, n_pages=n_pages, n_slc=n_slc, pos0=past),
        grid_spec=grid_spec,
        out_shape=jax.ShapeDtypeStruct((nb, tq, 512), F32),
        compiler_params=_cparams(("arbitrary",)),
        name="nsa_sample",
    )(page_table.reshape(-1), cache_l, zs3, zs3, zs3, cwin, kvc, bias_s, bias_n, bias_w, bias_cs,
      ovl, efull, eg)


def _mem_attn_kernel(q_ref, kv_ref, o_ref):
    q = q_ref[0]
    outs = []
    for h in range(MEM_HEADS):
        qh = q[:, h * 128:(h + 1) * 128].astype(BF16)
        kh = kv_ref[0, :, h * 128:(h + 1) * 128].astype(BF16)
        vh = kv_ref[0, :, MEM_WIDTH + h * 128:MEM_WIDTH + (h + 1) * 128].astype(BF16)
        s = _dot_nt(qh, kh) * (MEM_HEAD_DIM ** -0.5)
        mx = jnp.max(s, axis=-1, keepdims=True)
        e = jnp.exp(s - mx)
        p = e / jnp.sum(e, axis=-1, keepdims=True)
        outs.append(_dot(p.astype(BF16), vh))
    o_ref[0] = jnp.concatenate(outs, axis=1)


def _mem_attn(z3, mem_kv, tq):
    b, t, _ = z3.shape
    n_mem = mem_kv.shape[1]
    return pl.pallas_call(
        _mem_attn_kernel,
        grid=(b, t // tq),
        in_specs=[
            pl.BlockSpec((1, tq, TN), lambda i, j: (i, j, CB_MQ)),
            pl.BlockSpec((1, n_mem, 2 * MEM_WIDTH), lambda i, j: (i, 0, 0)),
        ],
        out_specs=pl.BlockSpec((1, tq, MEM_WIDTH), lambda i, j: (i, j, 0)),
        out_shape=jax.ShapeDtypeStruct((b, t, MEM_WIDTH), F32),
        compiler_params=_cparams(("parallel", "parallel")),
        name="mem_attn",
    )(z3, mem_kv)


def _merge_kernel(x_ref, u_ref, v_ref, g0_ref, g1_ref, g2_ref, on_ref, om_ref, wsp_ref, bsp_ref,
                  wa_ref, wb_ref, wc_ref, wo_ref, o_ref):
    tm = x_ref.shape[0]
    u = u_ref[...]
    v = v_ref[...].astype(BF16)
    chunks = []
    for c in range(tm // CHUNK):
        parts = []
        for g in range(A_GROUPS):
            vg = v[c * CHUNK:(c + 1) * CHUNK, g * 128:(g + 1) * 128]
            parts.append(_dot(wsp_ref[g], vg) + bsp_ref[g])
        chunks.append(jnp.concatenate(parts, axis=1))
    ya = (u * jnp.concatenate(chunks, axis=0)).astype(BF16)
    merged = (g0_ref[...] * _dot(ya, wa_ref[...])
              + g1_ref[...] * _dot(on_ref[...].astype(BF16), wb_ref[...])
              + g2_ref[...] * _dot(om_ref[...].astype(BF16), wc_ref[...]))
    o_ref[...] = x_ref[...] + _dot(merged.astype(BF16), wo_ref[...])


def _merge(x, z, o_nsa, o_mem, wsp, bsp, w_a, w_b, w_c, w_out, tm):
    m, d = x.shape
    row = lambda i: (i, 0)
    c2 = lambda i: (0, 0)
    c3 = lambda i: (0, 0, 0)
    return pl.pallas_call(
        _merge_kernel,
        grid=(m // tm,),
        in_specs=[
            pl.BlockSpec((tm, d), row),
            pl.BlockSpec((tm, TN), lambda i: (i, 0)),
            pl.BlockSpec((tm, TN), lambda i: (i, 1)),
            pl.BlockSpec((tm, d), lambda i: (i, CB_BG)),
            pl.BlockSpec((tm, d), lambda i: (i, CB_BG + 1)),
            pl.BlockSpec((tm, d), lambda i: (i, CB_BG + 2)),
            pl.BlockSpec((tm, 512), row),
            pl.BlockSpec((tm, 512), row),
            pl.BlockSpec((A_GROUPS, CHUNK, CHUNK), c3),
            pl.BlockSpec((A_GROUPS, CHUNK, 128), c3),
            pl.BlockSpec((A_WIDTH, d), c2),
            pl.BlockSpec((512, d), c2),
            pl.BlockSpec((MEM_WIDTH, d), c2),
            pl.BlockSpec((d, d), c2),
        ],
        out_specs=pl.BlockSpec((tm, d), row),
        out_shape=jax.ShapeDtypeStruct((m, d), F32),
        compiler_params=_cparams(("parallel",)),
        name="merge",
    )(x, z, z, z, z, z, o_nsa, o_mem, wsp, bsp, w_a, w_b, w_c, w_out)


def _mlp_kernel(x_ref, g_ref, wu_ref, wd_ref, o_ref, hn_ref, acc_ref):
    f = pl.program_id(1)

    @pl.when(f == 0)
    def _():
        hn_ref[...] = _rms(x_ref[...], g_ref[...]).astype(BF16)
        acc_ref[...] = jnp.zeros_like(acc_ref)

    h = jnp.square(jnp.maximum(_dot(hn_ref[...], wu_ref[...]), 0.0))
    acc_ref[...] += _dot(h.astype(BF16), wd_ref[...])

    @pl.when(f == pl.num_programs(1) - 1)
    def _():
        o_ref[...] = x_ref[...] + acc_ref[...]


def _mlp(x, gain, w_up, w_down, tm, tf):
    m, d = x.shape
    ff = w_up.shape[1]
    return pl.pallas_call(
        _mlp_kernel,
        grid=(m // tm, ff // tf),
        in_specs=[
            pl.BlockSpec((tm, d), lambda i, j: (i, 0)),
            pl.BlockSpec((1, d), lambda i, j: (0, 0)),
            pl.BlockSpec((d, tf), lambda i, j: (0, j)),
            pl.BlockSpec((tf, d), lambda i, j: (j, 0)),
        ],
        out_specs=pl.BlockSpec((tm, d), lambda i, j: (i, 0)),
        out_shape=jax.ShapeDtypeStruct((m, d), F32),
        scratch_shapes=[pltpu.VMEM((tm, d), BF16), pltpu.VMEM((tm, d), F32)],
        compiler_params=_cparams(("parallel", "arbitrary")),
        name="mlp",
    )(x, gain.reshape(1, d), w_up, w_down)


def _final_norm_kernel(x_ref, g_ref, o_ref):
    o_ref[...] = _rms(x_ref[...], g_ref[...])


def _final_norm(x, gain, tm):
    m, d = x.shape
    return pl.pallas_call(
        _final_norm_kernel,
        grid=(m // tm,),
        in_specs=[pl.BlockSpec((tm, d), lambda i: (i, 0)), pl.BlockSpec((1, d), lambda i: (0, 0))],
        out_specs=pl.BlockSpec((tm, d), lambda i: (i, 0)),
        out_shape=jax.ShapeDtypeStruct((m, d), F32),
        compiler_params=_cparams(("parallel",)),
        name="final_norm",
    )(x, gain.reshape(1, d))


def _relayout_w_in(w_in):
    nl, d, _ = w_in.shape
    wq = w_in[:, :, OFF_Q:OFF_KV].reshape(nl, d, NSA_HEADS, 1, HEAD_DIM)
    grp = (np.arange(NSA_HEADS) // NSA_REP)[:, None] == np.arange(NSA_KV_HEADS)[None, :]
    wq = jnp.where(jnp.asarray(grp)[None, None, :, :, None], wq, 0.0).reshape(nl, d, NSA_HEADS * 128)
    pad = jnp.zeros((nl, d, TN - 256 - (OFF_MQ - OFF_NG)), w_in.dtype)
    parts = [w_in[:, :, :OFF_Q], wq, w_in[:, :, OFF_KV:OFF_KV + 512],
             w_in[:, :, OFF_KV + 512:OFF_MQ], pad, w_in[:, :, OFF_BG:], w_in[:, :, OFF_MQ:OFF_BG]]
    out = jnp.concatenate(parts, axis=2).astype(BF16)
    assert out.shape[2] == Z_COLS
    return out


def _compress_weights(w_phi, pe_cmp):
    nl = w_phi.shape[0]
    w5 = w_phi.reshape(nl, 2, CMP_STRIDE, 2, HEAD_DIM, HEAD_DIM)
    eye = jnp.eye(NSA_KV_HEADS, dtype=w_phi.dtype)
    wc = jnp.einsum("nhlcde,gk->nclgdhke", w5, eye).reshape(nl, 2, CMP_STRIDE * 128, 256).astype(BF16)
    wphi2 = jnp.transpose(w_phi, (0, 2, 1, 3, 4)).reshape(nl, 2, L_CMP * HEAD_DIM, HEAD_DIM)
    wphi2 = jnp.concatenate([wphi2, wphi2], axis=-1)
    pe2 = jnp.transpose(pe_cmp, (0, 2, 1, 3)).reshape(nl, 2, 1, L_CMP * HEAD_DIM)
    pe_rep = jnp.broadcast_to(pe2, (nl, 2, 8, L_CMP * HEAD_DIM))
    return wc, pe_rep, wphi2


def _overlap_matrix(nj, n_cmp, n_slc):
    jj = np.arange(nj)[:, None]
    ss = np.arange(128)[None, :]
    ov = ((jj * CMP_STRIDE <= (ss + 1) * L_SEL - 1) & (jj * CMP_STRIDE + L_CMP - 1 >= ss * L_SEL)
          & (jj < n_cmp) & (ss < n_slc))
    return jnp.asarray(ov.astype(np.float32))


def _block_expand(n_keys):
    e = np.arange(128)[:, None] == (np.arange(n_keys)[None, :] // L_SEL)
    return jnp.asarray(e.astype(np.float32)).astype(BF16)


def _gate_expand():
    e = np.zeros((3, 128, 512), np.float32)
    for t in range(3):
        for h in range(NSA_HEADS):
            e[t, 3 * h + t, h * HEAD_DIM:(h + 1) * HEAD_DIM] = 1.0
    return jnp.asarray(e)


def _prompt_tables(bd, t):
    i = np.arange(TQ)[:, None]
    j = np.arange(TQ)[None, :]
    d0, d1, d3 = i - j, TQ + i - j, 2 * TQ + i - j
    toep = jnp.stack([
        _bias_lookup(bd, d0, d0 >= 0),
        _bias_lookup(bd, d1, np.ones_like(d1, bool)),
        _bias_lookup(bd, np.full_like(d0, BD_LEN - 1), np.ones_like(d0, bool)),
        _bias_lookup(bd, d3, d3 < WINDOW),
    ])
    nj = t // CMP_STRIDE
    n_cmp = (t - L_CMP) // CMP_STRIDE + 1
    qq = np.arange(t)[:, None]
    jj = np.arange(nj)[None, :]
    dc = qq - (jj * CMP_STRIDE + L_CMP - 1)
    bias_c = _bias_lookup(bd, dc, (dc >= 0) & (jj < n_cmp))
    return toep, bias_c, _overlap_matrix(nj, n_cmp, -(-t // L_SEL)), _block_expand(t)


def _sample_tables(bd, past, tq, wb):
    rows = NSA_HEADS * tq
    i = np.arange(tq)[:, None]
    k = np.arange(past)[None, :]
    ds = past + i - k
    bias_s = _bias_lookup(bd, ds, np.ones_like(ds, bool)).reshape(rows, past)
    jn = np.arange(128)[None, :]
    dn = i - jn
    bias_n = _bias_lookup(bd, dn, (dn >= 0) & (jn < tq)).reshape(rows, 128)
    w = np.arange(wb)[None, :]
    dw = wb + i - w
    bias_w = _bias_lookup(bd, dw, (dw >= 0) & (dw < WINDOW)).reshape(rows, wb)
    nj = past // CMP_STRIDE
    n_cmp = (past + tq - L_CMP) // CMP_STRIDE + 1
    jj = np.arange(nj)[None, :]
    dc = past + i - (jj * CMP_STRIDE + L_CMP - 1)
    bias_cs = _bias_lookup(bd, dc, (dc >= 0) & (jj < n_cmp)).reshape(rows, nj)
    n_slc = -(-(past + tq) // L_SEL)
    return bias_s, bias_n, bias_w, bias_cs, _overlap_matrix(nj, n_cmp, n_slc), _block_expand(past)


def _pick_tile(m, pref):
    t = min(m, pref)
    while m % t:
        t //= 2
    return t


def kernel(x_prompt, x_sample, cache_kv, cache_win_kv, cache_mem_kv, page_table, mem_prompt,
           ln1, w_in, g_v, w_s, b_s, w_a, pe_cmp, w_phi, w_b, ln_mem, w_mem_kv, w_c, w_out,
           ln2, w_up, w_down, rel_bias, ln_f):
    bp, t, d = x_prompt.shape
    bs, ts, _ = x_sample.shape
    n_pool = cache_kv.shape[1]
    n_pages = page_table.shape[1]
    past = n_pages * PAGE
    n_mem = mem_prompt.shape[1]
    wb = cache_win_kv.shape[2]
    assert t % TQ == 0 and t % CHUNK == 0 and past % L_SEL == 0 and ts <= L_SEL and CHUNK % ts == 0
    assert (past + ts - L_CMP) // CMP_STRIDE + 1 <= past // CMP_STRIDE
    assert wb == WINDOW and t >= WINDOW

    w_in_p = _relayout_w_in(w_in)
    wc_all, pe_rep_all, wphi2_all = _compress_weights(w_phi, pe_cmp)
    w_a16, w_b16, w_c16, w_out16 = (w.astype(BF16) for w in (w_a, w_b, w_c, w_out))
    w_up16, w_down16, w_mem16 = w_up.astype(BF16), w_down.astype(BF16), w_mem_kv.astype(BF16)
    causal = np.tril(np.ones((CHUNK, CHUNK), bool))
    wsp_p = jnp.where(jnp.asarray(causal), w_s, 0.0)
    blockdiag = np.kron(np.eye(CHUNK // ts), np.ones((ts, ts))) > 0
    wsp_s = jnp.where(jnp.asarray(blockdiag & causal),
                      jnp.tile(w_s[:, :, :ts, :ts], (1, 1, CHUNK // ts, CHUNK // ts)), 0.0)
    bsp_p = jnp.broadcast_to(b_s[:, :, :, None], b_s.shape + (128,))
    bsp_s = jnp.broadcast_to(jnp.tile(b_s[:, :, :ts], (1, 1, CHUNK // ts))[:, :, :, None], b_s.shape + (128,))
    wsp_p, wsp_s = wsp_p.astype(BF16), wsp_s.astype(BF16)

    bd = _bias_by_distance(rel_bias)
    toep, bias_c, ovl_p, ef_p = _prompt_tables(bd, t)
    bias_s, bias_n, bias_w, bias_cs, ovl_s, ef_s = _sample_tables(bd, past, ts, wb)
    eg = _gate_expand()
    pt_prompt = jnp.arange(bp * (t // PAGE), dtype=jnp.int32).reshape(bp, t // PAGE)

    np_tok, ns_tok = bp * t, bs * ts
    tm_p = _pick_tile(np_tok, 1024)
    tm_s = _pick_tile(ns_tok, 1024)
    xp = x_prompt.reshape(np_tok, d)
    xs = x_sample.reshape(ns_tok, d)
    mem_flat = mem_prompt.reshape(bp * n_mem, d)
    zeros_gv = jnp.zeros((TN,), F32)

    kv_p, win_p, mem_p, kv_s, win_s, v_s = [], [], [], [], [], []
    for l in range(DEPTH):
        mem_kv = _norm_matmul(mem_flat, ln_mem[l], w_mem16[l], zeros_gv, ("none", "none"),
                              _pick_tile(bp * n_mem, 1024), "mem_kv_proj")
        zp = _norm_matmul(xp, ln1[l], w_in_p[l], g_v[l], TILE_KINDS, tm_p, "in_proj")
        zp3 = zp.reshape(bp, t, Z_COLS)
        kvc_p = _compress(pt_prompt, zp.reshape(np_tok // PAGE, PAGE, Z_COLS), wc_all[l], pe_rep_all[l],
                          wphi2_all[l], col0=COL_PAGED, nr=_pick_tile(bp, 4))
        o_nsa = _nsa_prompt(zp3, kvc_p, bias_c, toep, ovl_p, ef_p, eg)
        o_mem = _mem_attn(zp3, mem_kv.reshape(bp, n_mem, 2 * MEM_WIDTH), _pick_tile(t, 512))
        x1 = _merge(xp, zp, o_nsa.reshape(np_tok, 512), o_mem.reshape(np_tok, MEM_WIDTH), wsp_p[l], bsp_p[l],
                    w_a16[l], w_b16[l], w_c16[l], w_out16[l], _pick_tile(np_tok, 256))
        xp = _mlp(x1, ln2[l], w_up16[l], w_down16[l], tm_p, 1024)
        kv_p.append(zp3[:, :, COL_PAGED:COL_PAGED + 512].reshape(bp, t, 4, NSA_KV_HEADS, HEAD_DIM))
        win_p.append(zp3[:, t - WINDOW:, CB_WN * TN:CB_WN * TN + 256].reshape(bp, WINDOW, 2, NSA_KV_HEADS, HEAD_DIM))
        mem_p.append(mem_kv.reshape(bp, n_mem, 2, MEM_HEADS, MEM_HEAD_DIM))

        cache_l = cache_kv[l].reshape(n_pool, PAGE, 4 * NSA_KV_HEADS * HEAD_DIM)
        zs = _norm_matmul(xs, ln1[l], w_in_p[l], g_v[l], TILE_KINDS, tm_s, "in_proj")
        zs3 = zs.reshape(bs, ts, Z_COLS)
        kvc_s = _compress(page_table, cache_l, wc_all[l], pe_rep_all[l], wphi2_all[l], col0=0,
                          nr=_pick_tile(bs, 4))
        cwin = cache_win_kv[l].reshape(bs, wb, 256)
        o_nsa = _nsa_sample(page_table, cache_l, zs3, cwin, kvc_s, bias_s, bias_n, bias_w, bias_cs,
                            ovl_s, ef_s, eg)
        o_mem = _mem_attn(zs3, cache_mem_kv[l].reshape(bs, n_mem, 2 * MEM_WIDTH), ts)
        x1 = _merge(xs, zs, o_nsa.reshape(ns_tok, 512), o_mem.reshape(ns_tok, MEM_WIDTH), wsp_s[l], bsp_s[l],
                    w_a16[l], w_b16[l], w_c16[l], w_out16[l], _pick_tile(ns_tok, 256))
        xs = _mlp(x1, ln2[l], w_up16[l], w_down16[l], tm_s, 1024)
        kv_s.append(zs3[:, :, COL_PAGED:COL_PAGED + 512].reshape(bs, ts, 4, NSA_KV_HEADS, HEAD_DIM))
        win_new = zs3[:, :, CB_WN * TN:CB_WN * TN + 256].reshape(bs, ts, 2, NSA_KV_HEADS, HEAD_DIM)
        win_s.append(jnp.concatenate([cache_win_kv[l][:, ts:], win_new], axis=1))
        v_s.append(zs3[:, :, TN:2 * TN])

    y_prompt = _final_norm(xp, ln_f, tm_p).reshape(bp, t, d)
    y_sample = _final_norm(xs, ln_f, tm_s).reshape(bs, ts, d)
    return (y_prompt, y_sample, jnp.stack(kv_p), jnp.stack(win_p), jnp.stack(mem_p),
            jnp.stack(kv_s), jnp.stack(win_s), jnp.stack(v_s))
```

```python
import functools
import math

import numpy as np
import jax
import jax.numpy as jnp
from jax import lax
from jax.experimental import pallas as pl
from jax.experimental.pallas import tpu as pltpu

F32 = jnp.float32
BF16 = jnp.bfloat16

EPS = 1e-6
LOG2E = math.log2(math.e)
NEG_INF = -1e30
LOWEST = -3e38

D_MODEL = 1024
DEPTH = 4
PAGE = 128
CHUNK = 128
A_GROUPS = 4
A_WIDTH = 512
NSA_HEADS = 8
NSA_KV_HEADS = 2
NSA_REP = 4
HEAD_DIM = 64
L_CMP = 32
CMP_STRIDE = 16
L_SEL = 64
N_SEL = 8
WINDOW = 256
FORCE_BONUS = 1e3
MEM_HEADS = 4
MEM_HEAD_DIM = 128
MEM_WIDTH = 512
NUM_BUCKETS = 32
MAX_DISTANCE = 128
D_FF = 4096
OFF_Q = 1024
OFF_KV = 1536
OFF_NG = 2304
OFF_MQ = 2328
OFF_BG = 2840

TN = 512
N_TILES = 13
Z_COLS = TN * N_TILES
TILE_KINDS = ("gelu", "gelu_norm", "none", "none", "none", "half_sig",
              "sig", "sig", "sig", "sig", "sig", "sig", "none")
CB_Q = 1
CB_PAGED = 4
CB_WN = 5
CB_BG = 3
CB_MQ = 12
COL_PAGED = CB_PAGED * TN

TQ = 128
KEY_UNROLL = 4
ROW_CHUNK = 256
VMEM_LIMIT = 56 * 1024 * 1024


def _cparams(sem):
    return pltpu.CompilerParams(dimension_semantics=sem, vmem_limit_bytes=VMEM_LIMIT)


def _rms(x, g):
    return x * lax.rsqrt(jnp.mean(x * x, axis=-1, keepdims=True) + EPS) * g


def _dot(a, b):
    return jnp.dot(a, b, preferred_element_type=F32)


def _dot_hi(a, b):
    return jnp.dot(a, b, preferred_element_type=F32, precision=lax.Precision.HIGHEST)


def _dot_nt(a, b):
    return lax.dot_general(a, b, (((1,), (1,)), ((), ())), preferred_element_type=F32)


def _split3(x):
    hi = x.astype(BF16)
    r1 = x - hi.astype(F32)
    mid = r1.astype(BF16)
    lo = (r1 - mid.astype(F32)).astype(BF16)
    return hi, mid, lo


def _bucket(dist):
    n = jnp.maximum(dist, 0)
    max_exact = NUM_BUCKETS // 2
    nf = jnp.maximum(n, 1).astype(F32)
    large = max_exact + (jnp.log(nf / max_exact) / math.log(MAX_DISTANCE / max_exact)
                         * (NUM_BUCKETS - max_exact)).astype(jnp.int32)
    return jnp.where(n < max_exact, n, jnp.minimum(large, NUM_BUCKETS - 1))


def _bias_table(rel_ref, o_ref, dist, valid, lead=(), mult=1.0):
    bucket = _bucket(dist)
    for h in range(NSA_HEADS):
        out = jnp.full(dist.shape, NEG_INF, F32)
        for b in range(NUM_BUCKETS):
            out = jnp.where(bucket == b, rel_ref[b, h], out)
        o_ref[lead + (h,)] = jnp.where(valid, out * mult, NEG_INF)


def _tables_kernel(rel_ref, toep_ref, tpl_ref, bs_ref, bn_ref, bw_ref, bcs_ref, *, past, n_cmp_s):
    def iota(shape, axis):
        return lax.broadcasted_iota(jnp.int32, shape, axis)

    sh = (TQ, TQ)
    d = iota(sh, 1) - iota(sh, 0)
    yes = d > -10 * TQ
    _bias_table(rel_ref, toep_ref, d, d >= 0, (0,), LOG2E)
    _bias_table(rel_ref, toep_ref, d + TQ, yes, (1,), LOG2E)
    _bias_table(rel_ref, toep_ref, d + 2 * TQ, yes, (2,), LOG2E)
    _bias_table(rel_ref, toep_ref, d + 2 * TQ, d + 2 * TQ < WINDOW, (3,), LOG2E)
    _bias_table(rel_ref, toep_ref, d, d > 10 * TQ, (4,), LOG2E)
    nj2 = tpl_ref.shape[1]
    sh = (nj2, TQ)
    dc = iota(sh, 1) - CMP_STRIDE * (iota(sh, 0) - nj2 // 2) - (L_CMP - 1)
    _bias_table(rel_ref, tpl_ref, dc, dc >= 0, (), LOG2E)
    ts = bs_ref.shape[1]
    sh = (ts, past)
    _bias_table(rel_ref, bs_ref, past + iota(sh, 0) - iota(sh, 1), iota(sh, 0) >= 0)
    sh = (ts, 128)
    dn = iota(sh, 0) - iota(sh, 1)
    _bias_table(rel_ref, bn_ref, dn, (dn >= 0) & (iota(sh, 1) < ts))
    wb = bw_ref.shape[2]
    sh = (ts, wb)
    dw = wb + iota(sh, 0) - iota(sh, 1)
    _bias_table(rel_ref, bw_ref, dw, dw < WINDOW)
    njs = bcs_ref.shape[2]
    sh = (ts, njs)
    dcs = past + iota(sh, 0) - CMP_STRIDE * iota(sh, 1) - (L_CMP - 1)
    _bias_table(rel_ref, bcs_ref, dcs, (dcs >= 0) & (iota(sh, 1) < n_cmp_s))


def _tables(rel_bias, t, past, ts, wb):
    nj = t // CMP_STRIDE
    njs = past // CMP_STRIDE
    n_cmp_s = (past + ts - L_CMP) // CMP_STRIDE + 1
    h = NSA_HEADS
    shapes = [(5, h, TQ, TQ), (h, 2 * nj, TQ), (h, ts, past), (h, ts, 128), (h, ts, wb), (h, ts, njs)]
    toep_t, tpl, bias_s, bias_n, bias_w, bias_cs = pl.pallas_call(
        functools.partial(_tables_kernel, past=past, n_cmp_s=n_cmp_s),
        in_specs=[pl.BlockSpec(memory_space=pltpu.SMEM)],
        out_shape=[jax.ShapeDtypeStruct(s, F32) for s in shapes],
        name="bias_tables",
    )(rel_bias)
    bias_ct = jnp.concatenate([tpl[:, nj - 8 * qt:2 * nj - 8 * qt, :] for qt in range(t // TQ)], axis=2)
    rows = h * ts
    return (toep_t, bias_ct, bias_s.reshape(rows, past), bias_n.reshape(rows, 128),
            bias_w.reshape(rows, wb), bias_cs.reshape(rows, njs))


def _norm_matmul_kernel(x_ref, g_ref, w_ref, gv_ref, o_ref, hn_ref, *, kinds):
    n = pl.program_id(1)
    tm = x_ref.shape[0]
    rc = min(ROW_CHUNK, tm)

    @pl.when(n == 0)
    def _():
        hn_ref[...] = _rms(x_ref[...], g_ref[...]).astype(BF16)

    for kind in sorted(set(kinds)):
        tiles = [i for i, k in enumerate(kinds) if k == kind]
        cond = functools.reduce(jnp.logical_or, [n == i for i in tiles])

        @pl.when(cond)
        def _(kind=kind):
            for r in range(tm // rc):
                acc = _dot(hn_ref[r * rc:(r + 1) * rc, :], w_ref[...])
                if kind == "gelu":
                    acc = jax.nn.gelu(acc)
                elif kind == "gelu_norm":
                    acc = _rms(jax.nn.gelu(acc), gv_ref[...])
                elif kind == "sig":
                    acc = jax.nn.sigmoid(acc)
                elif kind == "half_sig":
                    half = acc.shape[1] // 2
                    acc = jnp.concatenate([acc[:, :half], jax.nn.sigmoid(acc[:, half:])], axis=1)
                o_ref[r * rc:(r + 1) * rc, :] = acc


def _norm_matmul(x, gain, w, gv, kinds, tm, name):
    m, d = x.shape
    n_tiles = len(kinds)
    tn = w.shape[1] // n_tiles
    return pl.pallas_call(
        functools.partial(_norm_matmul_kernel, kinds=kinds),
        grid=(m // tm, n_tiles),
        in_specs=[
            pl.BlockSpec((tm, d), lambda i, j: (i, 0)),
            pl.BlockSpec((1, d), lambda i, j: (0, 0)),
            pl.BlockSpec((d, tn), lambda i, j: (0, j)),
            pl.BlockSpec((1, tn), lambda i, j: (0, 0)),
        ],
        out_specs=pl.BlockSpec((tm, tn), lambda i, j: (i, j)),
        out_shape=jax.ShapeDtypeStruct((m, w.shape[1]), F32),
        scratch_shapes=[pltpu.VMEM((tm, d), BF16)],
        compiler_params=_cparams(("parallel", "arbitrary")),
        name=name,
    )(x, gain.reshape(1, d), w, gv.reshape(1, tn))


def _cmp_copies(pt_ref, src_ref, buf_ref, sem_ref, step, slot, *, nr, n_pages, col0, layer):
    copies = []
    for r in range(nr):
        for p in range(n_pages):
            pid = pt_ref[(step * nr + r) * n_pages + p]
            if layer is None:
                for c in range(2):
                    copies.append(pltpu.make_async_copy(
                        src_ref.at[pid, :, pl.ds(col0 + c * 128, 128)],
                        buf_ref.at[slot, r, c, pl.ds(p * PAGE, PAGE), :],
                        sem_ref.at[slot]))
            else:
                copies.append(pltpu.make_async_copy(
                    src_ref.at[layer, pid, pl.ds(0, 2)],
                    buf_ref.at[slot, r, :, :, :, pl.ds(p * PAGE, PAGE)],
                    sem_ref.at[slot]))
    return copies


def _compress_kernel(pt_ref, src_ref, wc_ref, per_ref, wphi_ref, o_ref, buf_ref, sem_ref, *slab,
                     nr, n_pages, col0, layer):
    step = pl.program_id(0)
    n_steps = pl.num_programs(0)
    slot = lax.rem(step, 2)
    kw = dict(nr=nr, n_pages=n_pages, col0=col0, layer=layer)

    @pl.when(step == 0)
    def _():
        for cp in _cmp_copies(pt_ref, src_ref, buf_ref, sem_ref, step, slot, **kw):
            cp.start()

    @pl.when(step + 1 < n_steps)
    def _():
        for cp in _cmp_copies(pt_ref, src_ref, buf_ref, sem_ref, step + 1, 1 - slot, **kw):
            cp.start()

    for cp in _cmp_copies(pt_ref, src_ref, buf_ref, sem_ref, step, slot, **kw):
        cp.wait()

    rows = n_pages * PAGE
    m = rows // CMP_STRIDE
    if layer is not None:
        slab_ref, = slab
        for r in range(nr):
            for c in range(2):
                for p in range(n_pages):
                    blk = buf_ref[slot, r, c, :, :, pl.ds(p * PAGE, PAGE)].reshape(128, PAGE)
                    slab_ref[r, c, pl.ds(p * PAGE, PAGE), :] = blk.T

    outs = []
    for c in range(2):
        per_req = []
        for r in range(nr):
            if layer is None:
                slabs = [buf_ref[slot, r, c, pl.ds(l, m, stride=CMP_STRIDE), :] for l in range(CMP_STRIDE)]
            else:
                slabs = [slab_ref[r, c, pl.ds(l, m, stride=CMP_STRIDE), :] for l in range(CMP_STRIDE)]
            per_req.append(jnp.concatenate(slabs, axis=1))
        lhs = jnp.concatenate(per_req, axis=0).astype(BF16)
        p = _dot(lhs, wc_ref[c])
        bias = _dot_hi(per_ref[c], wphi_ref[c])[0:1]
        hi = pltpu.roll(p[:, 128:], nr * m - 1, 0)
        outs.append(p[:, :128] + hi + bias)
    res = jnp.concatenate(outs, axis=1)
    o_ref[...] = res.reshape(nr, m, 256)


def _compress(page_table, src, wc, pe_rep, wphi2, *, col0, nr, layer):
    nb, n_pages = page_table.shape
    rows = n_pages * PAGE
    m = rows // CMP_STRIDE
    if layer is None:
        scratch = [pltpu.VMEM((2, nr, 2, rows, 128), F32), pltpu.SemaphoreType.DMA((2,))]
    else:
        scratch = [pltpu.VMEM((2, nr, 2, NSA_KV_HEADS, HEAD_DIM, rows), F32), pltpu.SemaphoreType.DMA((2,)),
                   pltpu.VMEM((nr, 2, rows, 128), F32)]
    grid_spec = pltpu.PrefetchScalarGridSpec(
        num_scalar_prefetch=1,
        grid=(nb // nr,),
        in_specs=[
            pl.BlockSpec(memory_space=pl.ANY),
            pl.BlockSpec((2, CMP_STRIDE * 128, 256), lambda i, pt: (0, 0, 0)),
            pl.BlockSpec((2, 8, L_CMP * HEAD_DIM), lambda i, pt: (0, 0, 0)),
            pl.BlockSpec((2, L_CMP * HEAD_DIM, 128), lambda i, pt: (0, 0, 0)),
        ],
        out_specs=pl.BlockSpec((nr, m, 256), lambda i, pt: (i, 0, 0)),
        scratch_shapes=scratch,
    )
    return pl.pallas_call(
        functools.partial(_compress_kernel, nr=nr, n_pages=n_pages, col0=col0, layer=layer),
        grid_spec=grid_spec,
        out_shape=jax.ShapeDtypeStruct((nb, m, 256), F32),
        compiler_params=_cparams(("arbitrary",)),
        name="compress",
    )(page_table.reshape(-1), src, wc, pe_rep, wphi2)


def _select_blocks_t(imp_t, pos, n_slc):
    ns, c = imp_t.shape
    s_i = lax.broadcasted_iota(jnp.int32, (ns, c), 0)
    cur = lax.shift_right_logical(pos, 6)
    valid = (s_i * L_SEL) <= pos
    forced = (s_i == 0) | (s_i == cur) | (s_i == cur - 1)
    vals = jnp.where(valid, imp_t + jnp.where(forced, FORCE_BONUS, 0.0), -1.0)
    vals = jnp.where(s_i < n_slc, vals, LOWEST)
    rank = jnp.zeros((ns, c), F32)
    for sp in range(n_slc):
        row = vals[sp:sp + 1, :]
        beats = (row > vals) | ((row == vals) & (s_i > sp))
        rank = rank + jnp.where(beats, 1.0, 0.0)
    return jnp.where((rank < min(N_SEL, n_slc)) & (s_i < n_slc), 1.0, 0.0)


def _importance_t(ovl_t, p_sum):
    return functools.reduce(lambda a, b: a + b, [_dot(ovl_t, part) for part in _split3(p_sum)])


def _expand_gates(ng, eg_ref):
    parts = _split3(ng)
    return [functools.reduce(lambda a, b: a + b, [_dot(part, eg_ref[t]) for part in parts]) for t in range(3)]


def _pad_rows(x, rows):
    if x.shape[0] == rows:
        return x
    return jnp.concatenate([x, jnp.zeros((rows - x.shape[0],) + x.shape[1:], x.dtype)], axis=0)


def _nsa_prompt_kernel(q_ref, pg_ref, wn_ref, kvc_ref, bct_ref, toep_ref, ovl_ref, ef_ref, eg_ref,
                       o_ref, ks_ref, vst_ref, kw_ref, vwt_ref, mft_ref, *, n_slc):
    qt = pl.program_id(1)
    t_len = pg_ref.shape[1]
    nj = kvc_ref.shape[1]

    @pl.when(qt == 0)
    def _():
        ks_ref[...] = pg_ref[0, :, 256:384].astype(BF16)
        kw_ref[...] = wn_ref[0, :, 0:128].astype(BF16)
        for blk in range(t_len // TQ):
            rs = slice(blk * TQ, (blk + 1) * TQ)
            vst_ref[:, rs] = pg_ref[0, rs, 384:512].T.astype(BF16)
            vwt_ref[:, rs] = wn_ref[0, rs, 128:256].T.astype(BF16)

    scale = (HEAD_DIM ** -0.5) * LOG2E
    qts = [(q_ref[0, :, h * 128:(h + 1) * 128] * scale).T.astype(BF16) for h in range(NSA_HEADS)]
    groups = [h // NSA_REP for h in range(NSA_HEADS)]

    kc = kvc_ref[0, :, 0:128].astype(BF16)
    vct = kvc_ref[0, :, 128:256].T.astype(BF16)
    raw_c = [_dot(kc, qts[h]) for h in range(NSA_HEADS)]
    p_c, p_sum = [], [None] * NSA_KV_HEADS
    for h, g in enumerate(groups):
        s = raw_c[h] + bct_ref[h]
        mx = jnp.max(s, axis=0, keepdims=True)
        e = jnp.exp2(s - mx)
        p = jnp.where(mx > 0.5 * NEG_INF, e * (1.0 / jnp.sum(e, axis=0, keepdims=True)), 0.0)
        p_c.append(p.astype(BF16))
        p_sum[g] = p if p_sum[g] is None else p_sum[g] + p
    o_c = [_dot(vct[g * HEAD_DIM:(g + 1) * HEAD_DIM, :], p_c[h]) for h, g in enumerate(groups)]
    imp_t = _importance_t(ovl_ref[...], jnp.concatenate(p_sum, axis=1))
    ns = -(-n_slc // 8) * 8
    pos = qt * TQ + lax.rem(lax.broadcasted_iota(jnp.int32, (1, NSA_KV_HEADS * TQ), 1), TQ)
    sel_t = _select_blocks_t(imp_t[:ns], pos, n_slc)
    hidden = ((_pad_rows(sel_t, 128) - 1.0) * (-NEG_INF)).astype(BF16)
    mft_ref[...] = _dot(ef_ref[...], hidden)

    def attend(k_ref, vt_ref, kts, tbls, masked, carry):
        starts = [pl.multiple_of(kt * TQ, TQ) for kt in kts]
        ks = [k_ref[pl.ds(st, TQ), :] for st in starts]
        vts = [vt_ref[:, pl.ds(st, TQ)] for st in starts]
        raw = [[_dot(k, qts[h]) for k in ks] for h in range(NSA_HEADS)]
        stats, probs = [], []
        for h, g in enumerate(groups):
            m_i, l_i, _ = carry[h]
            ss = []
            for s, st, tbl in zip(raw[h], starts, tbls):
                s = s + toep_ref[tbl, h]
                if masked:
                    s = s + mft_ref[pl.ds(st, TQ), g * TQ:(g + 1) * TQ]
                ss.append(s)
            mx = functools.reduce(jnp.maximum, [jnp.max(s, axis=0, keepdims=True) for s in ss])
            m_new = jnp.maximum(m_i, mx)
            alpha = jnp.exp2(m_i - m_new)
            ps = [jnp.exp2(s - m_new) for s in ss]
            l_new = alpha * l_i + functools.reduce(
                lambda a, b: a + b, [jnp.sum(p, axis=0, keepdims=True) for p in ps])
            stats.append((m_new, l_new, alpha))
            probs.append(jnp.concatenate(ps, axis=0).astype(BF16))
        new = []
        for h, g in enumerate(groups):
            m_new, l_new, alpha = stats[h]
            vt = jnp.concatenate([v[g * HEAD_DIM:(g + 1) * HEAD_DIM, :] for v in vts], axis=1)
            new.append((m_new, l_new, alpha * carry[h][2] + _dot(vt, probs[h])))
        return tuple(new)

    init = tuple((jnp.full((1, TQ), NEG_INF, F32), jnp.zeros((1, TQ), F32),
                  jnp.zeros((HEAD_DIM, TQ), F32)) for _ in range(NSA_HEADS))

    def finish(state):
        return [acc * (1.0 / l_f) for (_, l_f, acc) in state]

    def sel_body(it, carry):
        kts = [it * KEY_UNROLL + u for u in range(KEY_UNROLL)]
        tbls = [jnp.where(qt - kt < 0, 4, jnp.minimum(qt - kt, 2)) for kt in kts]
        return attend(ks_ref, vst_ref, kts, tbls, True, carry)

    o_s = finish(lax.fori_loop(0, (qt + KEY_UNROLL) // KEY_UNROLL, sel_body, init))
    w_tiles = WINDOW // TQ + 1
    w_kts = [qt - (w_tiles - 1) + u for u in range(w_tiles)]
    w_tbls = [jnp.where(kt < 0, 4, tb) for kt, tb in zip(w_kts, (3, 1, 0))]
    o_w = finish(attend(kw_ref, vwt_ref, [jnp.maximum(kt, 0) for kt in w_kts], w_tbls, False, init))

    start = pl.multiple_of(qt * TQ, TQ)
    gates = _expand_gates(wn_ref[0, pl.ds(start, TQ), 256:384], eg_ref)
    tiles = []
    for mt in range(NSA_HEADS // 2):
        acc = jnp.zeros((TQ, 128), F32)
        for o, gate in zip((o_c, o_s, o_w), gates):
            pair = jnp.concatenate([o[2 * mt], o[2 * mt + 1]], axis=0)
            acc = acc + gate[:, mt * 128:(mt + 1) * 128] * pair.T
        tiles.append(acc)
    o_ref[0] = jnp.concatenate(tiles, axis=1)


def _nsa_prompt(z3, kvc, bias_ct, toep_t, ovl_t, ef_t, eg):
    b, t, _ = z3.shape
    nj = kvc.shape[1]
    n_slc = -(-t // L_SEL)
    return pl.pallas_call(
        functools.partial(_nsa_prompt_kernel, n_slc=n_slc),
        grid=(b, t // TQ),
        in_specs=[
            pl.BlockSpec((1, TQ, 1024), lambda i, j: (i, j, CB_Q)),
            pl.BlockSpec((1, t, TN), lambda i, j: (i, 0, CB_PAGED)),
            pl.BlockSpec((1, t, TN), lambda i, j: (i, 0, CB_WN)),
            pl.BlockSpec((1, nj, 256), lambda i, j: (i, 0, 0)),
            pl.BlockSpec((NSA_HEADS, nj, TQ), lambda i, j: (0, 0, j)),
            pl.BlockSpec((5, NSA_HEADS, TQ, TQ), lambda i, j: (0, 0, 0, 0)),
            pl.BlockSpec((128, nj), lambda i, j: (0, 0)),
            pl.BlockSpec((t, 128), lambda i, j: (0, 0)),
            pl.BlockSpec((3, 128, 512), lambda i, j: (0, 0, 0)),
        ],
        out_specs=pl.BlockSpec((1, TQ, 512), lambda i, j: (i, j, 0)),
        out_shape=jax.ShapeDtypeStruct((b, t, 512), F32),
        scratch_shapes=[pltpu.VMEM((t, 128), BF16), pltpu.VMEM((128, t), BF16),
                        pltpu.VMEM((t, 128), BF16), pltpu.VMEM((128, t), BF16),
                        pltpu.VMEM((t, NSA_KV_HEADS * TQ), F32)],
        compiler_params=_cparams(("parallel", "arbitrary")),
        name="nsa_prompt",
    )(z3, z3, z3, kvc, bias_ct, toep_t, ovl_t, ef_t, eg)


def _slc_copies(pt_ref, src_ref, buf_ref, sem_ref, req, slot, *, n_pages, layer):
    copies = []
    for p in range(n_pages):
        pid = pt_ref[req * n_pages + p]
        copies.append(pltpu.make_async_copy(
            src_ref.at[layer, pid, pl.ds(2, 2)],
            buf_ref.at[slot, :, :, :, pl.ds(p * PAGE, PAGE)],
            sem_ref.at[slot]))
    return copies


def _nsa_sample_kernel(pt_ref, src_ref, q_ref, pg_ref, wn_ref, cw_ref, kvc_ref, bs_ref, bn_ref, bw_ref,
                       bcs_ref, ovl_ref, ef_ref, eg_ref, o_ref, buf_ref, sem_ref,
                       *, n_pages, n_slc, pos0, layer):
    req = pl.program_id(0)
    n_req = pl.num_programs(0)
    slot = lax.rem(req, 2)
    tq = q_ref.shape[1]
    rows = NSA_HEADS * tq
    nj = kvc_ref.shape[1]
    past = n_pages * PAGE
    kw = dict(n_pages=n_pages, layer=layer)

    @pl.when(req == 0)
    def _():
        for cp in _slc_copies(pt_ref, src_ref, buf_ref, sem_ref, req, slot, **kw):
            cp.start()

    @pl.when(req + 1 < n_req)
    def _():
        for cp in _slc_copies(pt_ref, src_ref, buf_ref, sem_ref, req + 1, 1 - slot, **kw):
            cp.start()

    qs = q_ref[0] * (HEAD_DIM ** -0.5)
    q = jnp.concatenate([qs[:, h * 128:(h + 1) * 128] for h in range(NSA_HEADS)], axis=0).astype(BF16)

    kc = kvc_ref[0, :, 0:128].astype(BF16)
    vc = kvc_ref[0, :, 128:256].astype(BF16)
    s_c = _dot_nt(q, kc) + bcs_ref[...]
    mx = jnp.max(s_c, axis=-1, keepdims=True)
    e_c = jnp.exp(s_c - mx)
    p_c = jnp.where(mx > 0.5 * NEG_INF, e_c / jnp.sum(e_c, axis=-1, keepdims=True), 0.0)
    o_c = _dot(p_c.astype(BF16), vc)
    p_g = jnp.sum(p_c.reshape(NSA_KV_HEADS, NSA_REP, tq, nj), axis=1).reshape(NSA_KV_HEADS * tq, nj)
    imp_t = functools.reduce(lambda a, b: a + b,
                             [_dot_nt(ovl_ref[...], part) for part in _split3(_pad_rows(p_g, 128))])
    ns = -(-n_slc // 8) * 8
    lane = lax.broadcasted_iota(jnp.int32, (1, 128), 1)
    sel_t = _select_blocks_t(imp_t[:ns], pos0 + lax.rem(lane, tq), n_slc)
    sel = _pad_rows(sel_t, 128).T[:NSA_KV_HEADS * tq]
    mf = _dot(sel.astype(BF16), ef_ref[...])
    mf = jnp.broadcast_to(mf.reshape(NSA_KV_HEADS, 1, tq, past),
                          (NSA_KV_HEADS, NSA_REP, tq, past)).reshape(rows, past)

    def new_rows(x):
        return _pad_rows(x, 128).astype(BF16)

    def two_part(kt_old, vt_old, bias_old, k_new, v_new):
        s_o = _dot(q, kt_old) + bias_old
        s_n = _dot_nt(q, new_rows(k_new)) + bn_ref[...]
        m2 = jnp.maximum(jnp.max(s_o, axis=-1, keepdims=True), jnp.max(s_n, axis=-1, keepdims=True))
        e_o = jnp.exp(s_o - m2)
        e_n = jnp.exp(s_n - m2)
        den = jnp.sum(e_o, axis=-1, keepdims=True) + jnp.sum(e_n, axis=-1, keepdims=True)
        return (_dot_nt(e_o.astype(BF16), vt_old) + _dot(e_n.astype(BF16), new_rows(v_new))) / den

    pg = pg_ref[0]
    wn = wn_ref[0]
    wb = cw_ref.shape[5]
    o_w = two_part(cw_ref[0, 0, 0].reshape(128, wb).astype(BF16), cw_ref[0, 0, 1].reshape(128, wb).astype(BF16),
                   bw_ref[...], wn[:, 0:128], wn[:, 128:256])

    for cp in _slc_copies(pt_ref, src_ref, buf_ref, sem_ref, req, slot, **kw):
        cp.wait()

    bias_s = jnp.where(mf > 0.5, bs_ref[...], NEG_INF)
    o_s = two_part(buf_ref[slot, 0].reshape(128, past).astype(BF16),
                   buf_ref[slot, 1].reshape(128, past).astype(BF16), bias_s, pg[:, 256:384], pg[:, 384:512])

    gates = _expand_gates(wn[:, 256:384], eg_ref)
    lane128 = lax.broadcasted_iota(jnp.int32, (tq, 128), 1)
    tiles = []
    for mt in range(NSA_HEADS // 2):
        g = (2 * mt) // NSA_REP
        acc = jnp.zeros((tq, 128), F32)
        for o, gate in zip((o_c, o_s, o_w), gates):
            a = o[(2 * mt) * tq:(2 * mt + 1) * tq]
            b = o[(2 * mt + 1) * tq:(2 * mt + 2) * tq]
            if g == 1:
                a = pltpu.roll(a, 64, 1)
            else:
                b = pltpu.roll(b, 64, 1)
            acc = acc + gate[:, mt * 128:(mt + 1) * 128] * jnp.where(lane128 < 64, a, b)
        tiles.append(acc)
    o_ref[0] = jnp.concatenate(tiles, axis=1)


def _nsa_sample(page_table, cache_t, cwin_t, layer, zs3, kvc, bias_s, bias_n, bias_w, bias_cs, ovl_t, ef, eg):
    nb, n_pages = page_table.shape
    past = n_pages * PAGE
    tq = zs3.shape[1]
    rows = NSA_HEADS * tq
    nj = kvc.shape[1]
    n_slc = -(-(past + tq) // L_SEL)
    wb = cwin_t.shape[5]
    const2 = lambda i, pt: (0, 0)
    grid_spec = pltpu.PrefetchScalarGridSpec(
        num_scalar_prefetch=1,
        grid=(nb,),
        in_specs=[
            pl.BlockSpec(memory_space=pl.ANY),
            pl.BlockSpec((1, tq, 1024), lambda i, pt: (i, 0, CB_Q)),
            pl.BlockSpec((1, tq, TN), lambda i, pt: (i, 0, CB_PAGED)),
            pl.BlockSpec((1, tq, TN), lambda i, pt: (i, 0, CB_WN)),
            pl.BlockSpec((1, 1, 2, NSA_KV_HEADS, HEAD_DIM, wb), lambda i, pt: (layer, i, 0, 0, 0, 0)),
            pl.BlockSpec((1, nj, 256), lambda i, pt: (i, 0, 0)),
            pl.BlockSpec((rows, past), const2),
            pl.BlockSpec((rows, 128), const2),
            pl.BlockSpec((rows, wb), const2),
            pl.BlockSpec((rows, nj), const2),
            pl.BlockSpec((128, nj), const2),
            pl.BlockSpec((128, past), const2),
            pl.BlockSpec((3, 128, 512), lambda i, pt: (0, 0, 0)),
        ],
        out_specs=pl.BlockSpec((1, tq, 512), lambda i, pt: (i, 0, 0)),
        scratch_shapes=[pltpu.VMEM((2, 2, NSA_KV_HEADS, HEAD_DIM, past), F32), pltpu.SemaphoreType.DMA((2,))],
    )
    return pl.pallas_call(
        functools.partial(_nsa_sample_kernel, n_pages=n_pages, n_slc=n_slc, pos0=past, layer=layer),
        grid_spec=grid_spec,
        out_shape=jax.ShapeDtypeStruct((nb, tq, 512), F32),
        compiler_params=_cparams(("arbitrary",)),
        name="nsa_sample",
    )(page_table.reshape(-1), cache_t, zs3, zs3, zs3, cwin_t, kvc, bias_s, bias_n, bias_w, bias_cs,
      ovl_t, ef, eg)


def _mem_attn_kernel(q_ref, kv_ref, o_ref):
    q = q_ref[0]
    outs = []
    for h in range(MEM_HEADS):
        qh = q[:, h * 128:(h + 1) * 128].astype(BF16)
        kh = kv_ref[0, :, h * 128:(h + 1) * 128].astype(BF16)
        vh = kv_ref[0, :, MEM_WIDTH + h * 128:MEM_WIDTH + (h + 1) * 128].astype(BF16)
        s = _dot_nt(qh, kh) * (MEM_HEAD_DIM ** -0.5)
        mx = jnp.max(s, axis=-1, keepdims=True)
        e = jnp.exp(s - mx)
        p = e / jnp.sum(e, axis=-1, keepdims=True)
        outs.append(_dot(p.astype(BF16), vh))
    o_ref[0] = jnp.concatenate(outs, axis=1)


def _mem_attn(z3, mem_kv, tq):
    b, t, _ = z3.shape
    n_mem = mem_kv.shape[1]
    return pl.pallas_call(
        _mem_attn_kernel,
        grid=(b, t // tq),
        in_specs=[
            pl.BlockSpec((1, tq, TN), lambda i, j: (i, j, CB_MQ)),
            pl.BlockSpec((1, n_mem, 2 * MEM_WIDTH), lambda i, j: (i, 0, 0)),
        ],
        out_specs=pl.BlockSpec((1, tq, MEM_WIDTH), lambda i, j: (i, j, 0)),
        out_shape=jax.ShapeDtypeStruct((b, t, MEM_WIDTH), F32),
        compiler_params=_cparams(("parallel", "parallel")),
        name="mem_attn",
    )(z3, mem_kv)


def _merge_kernel(x_ref, u_ref, v_ref, g0_ref, g1_ref, g2_ref, on_ref, om_ref, wsp_ref, bsp_ref,
                  wa_ref, wb_ref, wc_ref, wo_ref, o_ref):
    tm = x_ref.shape[0]
    u = u_ref[...]
    v = v_ref[...].astype(BF16)
    chunks = []
    for c in range(tm // CHUNK):
        parts = []
        for g in range(A_GROUPS):
            vg = v[c * CHUNK:(c + 1) * CHUNK, g * 128:(g + 1) * 128]
            parts.append(_dot(wsp_ref[g], vg) + bsp_ref[g])
        chunks.append(jnp.concatenate(parts, axis=1))
    ya = (u * jnp.concatenate(chunks, axis=0)).astype(BF16)
    merged = (g0_ref[...] * _dot(ya, wa_ref[...])
              + g1_ref[...] * _dot(on_ref[...].astype(BF16), wb_ref[...])
              + g2_ref[...] * _dot(om_ref[...].astype(BF16), wc_ref[...]))
    o_ref[...] = x_ref[...] + _dot(merged.astype(BF16), wo_ref[...])


def _merge(x, z, o_nsa, o_mem, wsp, bsp, w_a, w_b, w_c, w_out, tm):
    m, d = x.shape
    row = lambda i: (i, 0)
    c2 = lambda i: (0, 0)
    c3 = lambda i: (0, 0, 0)
    return pl.pallas_call(
        _merge_kernel,
        grid=(m // tm,),
        in_specs=[
            pl.BlockSpec((tm, d), row),
            pl.BlockSpec((tm, TN), lambda i: (i, 0)),
            pl.BlockSpec((tm, TN), lambda i: (i, 1)),
            pl.BlockSpec((tm, d), lambda i: (i, CB_BG)),
            pl.BlockSpec((tm, d), lambda i: (i, CB_BG + 1)),
            pl.BlockSpec((tm, d), lambda i: (i, CB_BG + 2)),
            pl.BlockSpec((tm, 512), row),
            pl.BlockSpec((tm, 512), row),
            pl.BlockSpec((A_GROUPS, CHUNK, CHUNK), c3),
            pl.BlockSpec((A_GROUPS, CHUNK, 128), c3),
            pl.BlockSpec((A_WIDTH, d), c2),
            pl.BlockSpec((512, d), c2),
            pl.BlockSpec((MEM_WIDTH, d), c2),
            pl.BlockSpec((d, d), c2),
        ],
        out_specs=pl.BlockSpec((tm, d), row),
        out_shape=jax.ShapeDtypeStruct((m, d), F32),
        compiler_params=_cparams(("parallel",)),
        name="merge",
    )(x, z, z, z, z, z, o_nsa, o_mem, wsp, bsp, w_a, w_b, w_c, w_out)


def _mlp_kernel(x_ref, g_ref, wu_ref, wd_ref, o_ref, hn_ref, acc_ref):
    f = pl.program_id(1)

    @pl.when(f == 0)
    def _():
        hn_ref[...] = _rms(x_ref[...], g_ref[...]).astype(BF16)
        acc_ref[...] = jnp.zeros_like(acc_ref)

    h = jnp.square(jnp.maximum(_dot(hn_ref[...], wu_ref[...]), 0.0))
    acc_ref[...] += _dot(h.astype(BF16), wd_ref[...])

    @pl.when(f == pl.num_programs(1) - 1)
    def _():
        o_ref[...] = x_ref[...] + acc_ref[...]


def _mlp(x, gain, w_up, w_down, tm, tf):
    m, d = x.shape
    ff = w_up.shape[1]
    return pl.pallas_call(
        _mlp_kernel,
        grid=(m // tm, ff // tf),
        in_specs=[
            pl.BlockSpec((tm, d), lambda i, j: (i, 0)),
            pl.BlockSpec((1, d), lambda i, j: (0, 0)),
            pl.BlockSpec((d, tf), lambda i, j: (0, j)),
            pl.BlockSpec((tf, d), lambda i, j: (j, 0)),
        ],
        out_specs=pl.BlockSpec((tm, d), lambda i, j: (i, 0)),
        out_shape=jax.ShapeDtypeStruct((m, d), F32),
        scratch_shapes=[pltpu.VMEM((tm, d), BF16), pltpu.VMEM((tm, d), F32)],
        compiler_params=_cparams(("parallel", "arbitrary")),
        name="mlp",
    )(x, gain.reshape(1, d), w_up, w_down)


def _final_norm_kernel(x_ref, g_ref, o_ref):
    o_ref[...] = _rms(x_ref[...], g_ref[...])


def _final_norm(x, gain, tm):
    m, d = x.shape
    return pl.pallas_call(
        _final_norm_kernel,
        grid=(m // tm,),
        in_specs=[pl.BlockSpec((tm, d), lambda i: (i, 0)), pl.BlockSpec((1, d), lambda i: (0, 0))],
        out_specs=pl.BlockSpec((tm, d), lambda i: (i, 0)),
        out_shape=jax.ShapeDtypeStruct((m, d), F32),
        compiler_params=_cparams(("parallel",)),
        name="final_norm",
    )(x, gain.reshape(1, d))


def _relayout_w_in(w_in):
    nl, d, _ = w_in.shape
    wq = w_in[:, :, OFF_Q:OFF_KV].reshape(nl, d, NSA_HEADS, 1, HEAD_DIM)
    grp = (np.arange(NSA_HEADS) // NSA_REP)[:, None] == np.arange(NSA_KV_HEADS)[None, :]
    wq = jnp.where(jnp.asarray(grp)[None, None, :, :, None], wq, 0.0).reshape(nl, d, NSA_HEADS * 128)
    pad = jnp.zeros((nl, d, TN - 256 - (OFF_MQ - OFF_NG)), w_in.dtype)
    parts = [w_in[:, :, :OFF_Q], wq, w_in[:, :, OFF_KV:OFF_KV + 512],
             w_in[:, :, OFF_KV + 512:OFF_MQ], pad, w_in[:, :, OFF_BG:], w_in[:, :, OFF_MQ:OFF_BG]]
    out = jnp.concatenate(parts, axis=2).astype(BF16)
    assert out.shape[2] == Z_COLS
    return out


def _compress_weights(w_phi, pe_cmp):
    nl = w_phi.shape[0]
    w5 = w_phi.reshape(nl, 2, CMP_STRIDE, 2, HEAD_DIM, HEAD_DIM)
    eye = jnp.eye(NSA_KV_HEADS, dtype=w_phi.dtype)
    wc = jnp.einsum("nhlcde,gk->nclgdhke", w5, eye).reshape(nl, 2, CMP_STRIDE * 128, 256).astype(BF16)
    wphi2 = jnp.transpose(w_phi, (0, 2, 1, 3, 4)).reshape(nl, 2, L_CMP * HEAD_DIM, HEAD_DIM)
    wphi2 = jnp.concatenate([wphi2, wphi2], axis=-1)
    pe2 = jnp.transpose(pe_cmp, (0, 2, 1, 3)).reshape(nl, 2, 1, L_CMP * HEAD_DIM)
    pe_rep = jnp.broadcast_to(pe2, (nl, 2, 8, L_CMP * HEAD_DIM))
    return wc, pe_rep, wphi2


def _overlap_t(nj, n_cmp, n_slc):
    ss = np.arange(128)[:, None]
    jj = np.arange(nj)[None, :]
    ov = ((jj * CMP_STRIDE <= (ss + 1) * L_SEL - 1) & (jj * CMP_STRIDE + L_CMP - 1 >= ss * L_SEL)
          & (jj < n_cmp) & (ss < n_slc))
    return jnp.asarray(ov.astype(np.float32)).astype(BF16)


def _block_expand(n_keys):
    e = np.arange(128)[:, None] == (np.arange(n_keys)[None, :] // L_SEL)
    return jnp.asarray(e.astype(np.float32)).astype(BF16)


def _gate_expand():
    e = np.zeros((3, 128, 512), np.float32)
    for t in range(3):
        for h in range(NSA_HEADS):
            e[t, 3 * h + t, h * HEAD_DIM:(h + 1) * HEAD_DIM] = 1.0
    return jnp.asarray(e).astype(BF16)


def _pick_tile(m, pref):
    t = min(m, pref)
    while m % t:
        t //= 2
    return t


def kernel(x_prompt, x_sample, cache_kv, cache_win_kv, cache_mem_kv, page_table, mem_prompt,
           ln1, w_in, g_v, w_s, b_s, w_a, pe_cmp, w_phi, w_b, ln_mem, w_mem_kv, w_c, w_out,
           ln2, w_up, w_down, rel_bias, ln_f):
    bp, t, d = x_prompt.shape
    bs, ts, _ = x_sample.shape
    n_pages = page_table.shape[1]
    past = n_pages * PAGE
    n_mem = mem_prompt.shape[1]
    wb = cache_win_kv.shape[2]
    assert t % (KEY_UNROLL * TQ) == 0 and t % CHUNK == 0 and past % L_SEL == 0 and ts <= L_SEL and CHUNK % ts == 0
    assert (past + ts - L_CMP) // CMP_STRIDE + 1 <= past // CMP_STRIDE
    assert wb == WINDOW and t >= WINDOW

    w_in_p = _relayout_w_in(w_in)
    wc_all, pe_rep_all, wphi2_all = _compress_weights(w_phi, pe_cmp)
    w_a16, w_b16, w_c16, w_out16 = (w.astype(BF16) for w in (w_a, w_b, w_c, w_out))
    w_up16, w_down16, w_mem16 = w_up.astype(BF16), w_down.astype(BF16), w_mem_kv.astype(BF16)
    causal = np.tril(np.ones((CHUNK, CHUNK), bool))
    wsp_p = jnp.where(jnp.asarray(causal), w_s, 0.0)
    blockdiag = np.kron(np.eye(CHUNK // ts), np.ones((ts, ts))) > 0
    wsp_s = jnp.where(jnp.asarray(blockdiag & causal),
                      jnp.tile(w_s[:, :, :ts, :ts], (1, 1, CHUNK // ts, CHUNK // ts)), 0.0)
    bsp_p = jnp.broadcast_to(b_s[:, :, :, None], b_s.shape + (128,))
    bsp_s = jnp.broadcast_to(jnp.tile(b_s[:, :, :ts], (1, 1, CHUNK // ts))[:, :, :, None], b_s.shape + (128,))
    wsp_p, wsp_s = wsp_p.astype(BF16), wsp_s.astype(BF16)

    cache_t = jnp.transpose(cache_kv, (0, 1, 3, 4, 5, 2))
    cwin_t = jnp.transpose(cache_win_kv, (0, 1, 3, 4, 5, 2))

    toep_t, bias_ct, bias_s, bias_n, bias_w, bias_cs = _tables(rel_bias, t, past, ts, wb)
    nj_p, nj_s = t // CMP_STRIDE, past // CMP_STRIDE
    ovl_p = _overlap_t(nj_p, (t - L_CMP) // CMP_STRIDE + 1, -(-t // L_SEL))
    ovl_s = _overlap_t(nj_s, (past + ts - L_CMP) // CMP_STRIDE + 1, -(-(past + ts) // L_SEL))
    ef_p_t = _block_expand(t).T
    ef_s = _block_expand(past)
    eg = _gate_expand()
    pt_prompt = jnp.arange(bp * (t // PAGE), dtype=jnp.int32).reshape(bp, t // PAGE)

    np_tok, ns_tok = bp * t, bs * ts
    tm_p = _pick_tile(np_tok, 1024)
    tm_s = _pick_tile(ns_tok, 1024)
    xp = x_prompt.reshape(np_tok, d)
    xs = x_sample.reshape(ns_tok, d)
    mem_flat = mem_prompt.reshape(bp * n_mem, d)
    zeros_gv = jnp.zeros((TN,), F32)

    kv_p, win_p, mem_p, kv_s, win_s, v_s = [], [], [], [], [], []
    for l in range(DEPTH):
        mem_kv = _norm_matmul(mem_flat, ln_mem[l], w_mem16[l], zeros_gv, ("none", "none"),
                              _pick_tile(bp * n_mem, 1024), "mem_kv_proj")
        zp = _norm_matmul(xp, ln1[l], w_in_p[l], g_v[l], TILE_KINDS, tm_p, "in_proj")
        zp3 = zp.reshape(bp, t, Z_COLS)
        kvc_p = _compress(pt_prompt, zp.reshape(np_tok // PAGE, PAGE, Z_COLS), wc_all[l], pe_rep_all[l],
                          wphi2_all[l], col0=COL_PAGED, nr=_pick_tile(bp, 4), layer=None)
        o_nsa = _nsa_prompt(zp3, kvc_p, bias_ct, toep_t, ovl_p, ef_p_t, eg)
        o_mem = _mem_attn(zp3, mem_kv.reshape(bp, n_mem, 2 * MEM_WIDTH), _pick_tile(t, 512))
        x1 = _merge(xp, zp, o_nsa.reshape(np_tok, 512), o_mem.reshape(np_tok, MEM_WIDTH), wsp_p[l], bsp_p[l],
                    w_a16[l], w_b16[l], w_c16[l], w_out16[l], _pick_tile(np_tok, 256))
        xp = _mlp(x1, ln2[l], w_up16[l], w_down16[l], tm_p, 1024)
        kv_p.append(zp3[:, :, COL_PAGED:COL_PAGED + 512].reshape(bp, t, 4, NSA_KV_HEADS, HEAD_DIM))
        win_p.append(zp3[:, t - WINDOW:, CB_WN * TN:CB_WN * TN + 256].reshape(bp, WINDOW, 2, NSA_KV_HEADS, HEAD_DIM))
        mem_p.append(mem_kv.reshape(bp, n_mem, 2, MEM_HEADS, MEM_HEAD_DIM))

        zs = _norm_matmul(xs, ln1[l], w_in_p[l], g_v[l], TILE_KINDS, tm_s, "in_proj")
        zs3 = zs.reshape(bs, ts, Z_COLS)
        kvc_s = _compress(page_table, cache_t, wc_all[l], pe_rep_all[l], wphi2_all[l], col0=0,
                          nr=_pick_tile(bs, 4), layer=l)
        o_nsa = _nsa_sample(page_table, cache_t, cwin_t, l, zs3, kvc_s, bias_s, bias_n, bias_w, bias_cs,
                            ovl_s, ef_s, eg)
        o_mem = _mem_attn(zs3, cache_mem_kv[l].reshape(bs, n_mem, 2 * MEM_WIDTH), ts)
        x1 = _merge(xs, zs, o_nsa.reshape(ns_tok, 512), o_mem.reshape(ns_tok, MEM_WIDTH), wsp_s[l], bsp_s[l],
                    w_a16[l], w_b16[l], w_c16[l], w_out16[l], _pick_tile(ns_tok, 256))
        xs = _mlp(x1, ln2[l], w_up16[l], w_down16[l], tm_s, 1024)
        kv_s.append(zs3[:, :, COL_PAGED:COL_PAGED + 512].reshape(bs, ts, 4, NSA_KV_HEADS, HEAD_DIM))
        win_new = zs3[:, :, CB_WN * TN:CB_WN * TN + 256].reshape(bs, ts, 2, NSA_KV_HEADS, HEAD_DIM)
        win_s.append(jnp.concatenate([cache_win_kv[l][:, ts:], win_new], axis=1))
        v_s.append(zs3[:, :, TN:2 * TN])

    y_prompt = _final_norm(xp, ln_f, tm_p).reshape(bp, t, d)
    y_sample = _final_norm(xs, ln_f, tm_s).reshape(bs, ts, d)
    return (y_prompt, y_sample, jnp.stack(kv_p), jnp.stack(win_p), jnp.stack(mem_p),
            jnp.stack(kv_s), jnp.stack(win_s), jnp.stack(v_s))
```

```python
import functools
import math

import numpy as np
import jax
import jax.numpy as jnp
from jax import lax
from jax.experimental import pallas as pl
from jax.experimental.pallas import tpu as pltpu

F32 = jnp.float32
BF16 = jnp.bfloat16

EPS = 1e-6
LOG2E = math.log2(math.e)
NEG_INF = -1e30
LOWEST = -3e38

D_MODEL = 1024
DEPTH = 4
PAGE = 128
CHUNK = 128
A_GROUPS = 4
A_WIDTH = 512
NSA_HEADS = 8
NSA_KV_HEADS = 2
NSA_REP = 4
HEAD_DIM = 64
L_CMP = 32
CMP_STRIDE = 16
L_SEL = 64
N_SEL = 8
WINDOW = 256
FORCE_BONUS = 1e3
MEM_HEADS = 4
MEM_HEAD_DIM = 128
MEM_WIDTH = 512
NUM_BUCKETS = 32
MAX_DISTANCE = 128
D_FF = 4096
OFF_Q = 1024
OFF_KV = 1536
OFF_NG = 2304
OFF_MQ = 2328
OFF_BG = 2840

TN = 512
TILE_KINDS = ("gelu_norm", "none", "half_sig",
              "sig", "sig", "sig", "sig", "sig", "sig", "none", "none", "gelu", "none")
N_F32_TILES = 3
Z32_COLS = TN * N_F32_TILES
Z16_COLS = TN * (len(TILE_KINDS) - N_F32_TILES)
CB32_V = 0
CB32_PAGED = 1
CB32_WN = 2
CB16_BG = 0
CB16_Q = 3
CB16_U = 8
CB16_MQ = 9
COL_PAGED = CB32_PAGED * TN
COL_WN = CB32_WN * TN

TQ = 128
KEY_UNROLL = 4
ROW_CHUNK = 256
VMEM_LIMIT = 56 * 1024 * 1024


def _cparams(sem):
    return pltpu.CompilerParams(dimension_semantics=sem, vmem_limit_bytes=VMEM_LIMIT)


def _rms(x, g):
    return x * lax.rsqrt(jnp.mean(x * x, axis=-1, keepdims=True) + EPS) * g


def _dot(a, b):
    return jnp.dot(a, b, preferred_element_type=F32)


def _dot_hi(a, b):
    return jnp.dot(a, b, preferred_element_type=F32, precision=lax.Precision.HIGHEST)


def _dot_nt(a, b):
    return lax.dot_general(a, b, (((1,), (1,)), ((), ())), preferred_element_type=F32)


def _split3(x):
    hi = x.astype(BF16)
    r1 = x - hi.astype(F32)
    mid = r1.astype(BF16)
    lo = (r1 - mid.astype(F32)).astype(BF16)
    return hi, mid, lo


def _bucket(dist):
    n = jnp.maximum(dist, 0)
    max_exact = NUM_BUCKETS // 2
    nf = jnp.maximum(n, 1).astype(F32)
    large = max_exact + (jnp.log(nf / max_exact) / math.log(MAX_DISTANCE / max_exact)
                         * (NUM_BUCKETS - max_exact)).astype(jnp.int32)
    return jnp.where(n < max_exact, n, jnp.minimum(large, NUM_BUCKETS - 1))


def _bias_table(rel_ref, o_ref, dist, valid, lead=(), mult=1.0):
    bucket = _bucket(dist)
    for h in range(NSA_HEADS):
        out = jnp.full(dist.shape, NEG_INF, F32)
        for b in range(NUM_BUCKETS):
            out = jnp.where(bucket == b, rel_ref[b, h], out)
        o_ref[lead + (h,)] = jnp.where(valid, out * mult, NEG_INF)


def _tables_kernel(rel_ref, toep_ref, tpl_ref, bs_ref, bn_ref, bw_ref, bcs_ref, *, past, n_cmp_s):
    def iota(shape, axis):
        return lax.broadcasted_iota(jnp.int32, shape, axis)

    sh = (TQ, TQ)
    d = iota(sh, 1) - iota(sh, 0)
    yes = d > -10 * TQ
    _bias_table(rel_ref, toep_ref, d, d >= 0, (0,), LOG2E)
    _bias_table(rel_ref, toep_ref, d + TQ, yes, (1,), LOG2E)
    _bias_table(rel_ref, toep_ref, d + 2 * TQ, yes, (2,), LOG2E)
    _bias_table(rel_ref, toep_ref, d + 2 * TQ, d + 2 * TQ < WINDOW, (3,), LOG2E)
    _bias_table(rel_ref, toep_ref, d, d > 10 * TQ, (4,), LOG2E)
    nj2 = tpl_ref.shape[1]
    sh = (nj2, TQ)
    dc = iota(sh, 1) - CMP_STRIDE * (iota(sh, 0) - nj2 // 2) - (L_CMP - 1)
    _bias_table(rel_ref, tpl_ref, dc, dc >= 0, (), LOG2E)
    ts = bs_ref.shape[1]
    sh = (ts, past)
    _bias_table(rel_ref, bs_ref, past + iota(sh, 0) - iota(sh, 1), iota(sh, 0) >= 0, (), LOG2E)
    sh = (ts, 128)
    dn = iota(sh, 0) - iota(sh, 1)
    _bias_table(rel_ref, bn_ref, dn, (dn >= 0) & (iota(sh, 1) < ts), (), LOG2E)
    wb = bw_ref.shape[2]
    sh = (ts, wb)
    dw = wb + iota(sh, 0) - iota(sh, 1)
    _bias_table(rel_ref, bw_ref, dw, dw < WINDOW, (), LOG2E)
    njs = bcs_ref.shape[2]
    sh = (ts, njs)
    dcs = past + iota(sh, 0) - CMP_STRIDE * iota(sh, 1) - (L_CMP - 1)
    _bias_table(rel_ref, bcs_ref, dcs, (dcs >= 0) & (iota(sh, 1) < n_cmp_s), (), LOG2E)


def _tables(rel_bias, t, past, ts, wb):
    nj = t // CMP_STRIDE
    njs = past // CMP_STRIDE
    n_cmp_s = (past + ts - L_CMP) // CMP_STRIDE + 1
    h = NSA_HEADS
    shapes = [(5, h, TQ, TQ), (h, 2 * nj, TQ), (h, ts, past), (h, ts, 128), (h, ts, wb), (h, ts, njs)]
    toep_t, tpl, bias_s, bias_n, bias_w, bias_cs = pl.pallas_call(
        functools.partial(_tables_kernel, past=past, n_cmp_s=n_cmp_s),
        in_specs=[pl.BlockSpec(memory_space=pltpu.SMEM)],
        out_shape=[jax.ShapeDtypeStruct(s, F32) for s in shapes],
        name="bias_tables",
    )(rel_bias)
    bias_ct = jnp.concatenate([tpl[:, nj - 8 * qt:2 * nj - 8 * qt, :] for qt in range(t // TQ)], axis=2)
    rows = h * ts
    return (toep_t, bias_ct, bias_s.reshape(rows, past), bias_n.reshape(rows, 128),
            bias_w.reshape(rows, wb), bias_cs.reshape(rows, njs))


def _norm_matmul_kernel(x_ref, g_ref, w_ref, gv_ref, *rest, kinds, n_f32):
    out_refs, hn_ref = rest[:-1], rest[-1]
    n = pl.program_id(1)
    tm = x_ref.shape[0]
    rc = min(ROW_CHUNK, tm)

    @pl.when(n == 0)
    def _():
        hn_ref[...] = _rms(x_ref[...], g_ref[...]).astype(BF16)

    for kind, wide in sorted(set((k, i < n_f32) for i, k in enumerate(kinds))):
        tiles = [i for i, k in enumerate(kinds) if k == kind and (i < n_f32) == wide]
        cond = functools.reduce(jnp.logical_or, [n == i for i in tiles])
        o_ref = out_refs[0] if wide else out_refs[1]

        @pl.when(cond)
        def _(kind=kind, o_ref=o_ref):
            for r in range(tm // rc):
                acc = _dot(hn_ref[r * rc:(r + 1) * rc, :], w_ref[...])
                if kind == "gelu":
                    acc = jax.nn.gelu(acc)
                elif kind == "gelu_norm":
                    acc = _rms(jax.nn.gelu(acc), gv_ref[...])
                elif kind == "sig":
                    acc = jax.nn.sigmoid(acc)
                elif kind == "half_sig":
                    half = acc.shape[1] // 2
                    acc = jnp.concatenate([acc[:, :half], jax.nn.sigmoid(acc[:, half:])], axis=1)
                o_ref[r * rc:(r + 1) * rc, :] = acc.astype(o_ref.dtype)


def _norm_matmul(x, gain, w, gv, kinds, n_f32, tm, name):
    m, d = x.shape
    n_tiles = len(kinds)
    tn = w.shape[1] // n_tiles
    out_specs = [pl.BlockSpec((tm, tn), lambda i, j: (i, jnp.minimum(j, n_f32 - 1)))]
    out_shape = [jax.ShapeDtypeStruct((m, n_f32 * tn), F32)]
    if n_tiles > n_f32:
        out_specs.append(pl.BlockSpec((tm, tn), lambda i, j: (i, jnp.maximum(j - n_f32, 0))))
        out_shape.append(jax.ShapeDtypeStruct((m, (n_tiles - n_f32) * tn), BF16))
    outs = pl.pallas_call(
        functools.partial(_norm_matmul_kernel, kinds=kinds, n_f32=n_f32),
        grid=(m // tm, n_tiles),
        in_specs=[
            pl.BlockSpec((tm, d), lambda i, j: (i, 0)),
            pl.BlockSpec((1, d), lambda i, j: (0, 0)),
            pl.BlockSpec((d, tn), lambda i, j: (0, j)),
            pl.BlockSpec((1, tn), lambda i, j: (0, 0)),
        ],
        out_specs=out_specs,
        out_shape=out_shape,
        scratch_shapes=[pltpu.VMEM((tm, d), BF16)],
        compiler_params=_cparams(("parallel", "arbitrary")),
        name=name,
    )(x, gain.reshape(1, d), w, gv.reshape(1, tn))
    return outs[0], (outs[1] if len(outs) > 1 else None)


def _cmp_copies(pt_ref, src_ref, buf_ref, sem_ref, step, slot, *, nr, n_pages, col0, layer):
    copies = []
    for r in range(nr):
        for p in range(n_pages):
            pid = pt_ref[(step * nr + r) * n_pages + p]
            if layer is None:
                for c in range(2):
                    copies.append(pltpu.make_async_copy(
                        src_ref.at[pid, :, pl.ds(col0 + c * 128, 128)],
                        buf_ref.at[slot, r, c, pl.ds(p * PAGE, PAGE), :],
                        sem_ref.at[slot]))
            else:
                copies.append(pltpu.make_async_copy(
                    src_ref.at[layer, pid, pl.ds(0, 2)],
                    buf_ref.at[slot, r, :, :, :, pl.ds(p * PAGE, PAGE)],
                    sem_ref.at[slot]))
    return copies


def _compress_kernel(pt_ref, src_ref, wc_ref, per_ref, wphi_ref, o_ref, buf_ref, sem_ref, *slab,
                     nr, n_pages, col0, layer):
    step = pl.program_id(0)
    n_steps = pl.num_programs(0)
    slot = lax.rem(step, 2)
    kw = dict(nr=nr, n_pages=n_pages, col0=col0, layer=layer)

    @pl.when(step == 0)
    def _():
        for cp in _cmp_copies(pt_ref, src_ref, buf_ref, sem_ref, step, slot, **kw):
            cp.start()

    @pl.when(step + 1 < n_steps)
    def _():
        for cp in _cmp_copies(pt_ref, src_ref, buf_ref, sem_ref, step + 1, 1 - slot, **kw):
            cp.start()

    for cp in _cmp_copies(pt_ref, src_ref, buf_ref, sem_ref, step, slot, **kw):
        cp.wait()

    rows = n_pages * PAGE
    m = rows // CMP_STRIDE
    if layer is not None:
        slab_ref, = slab
        for r in range(nr):
            for c in range(2):
                for p in range(n_pages):
                    blk = buf_ref[slot, r, c, :, :, pl.ds(p * PAGE, PAGE)].reshape(128, PAGE)
                    slab_ref[r, c, pl.ds(p * PAGE, PAGE), :] = blk.T

    outs = []
    for c in range(2):
        per_req = []
        for r in range(nr):
            if layer is None:
                slabs = [buf_ref[slot, r, c, pl.ds(l, m, stride=CMP_STRIDE), :] for l in range(CMP_STRIDE)]
            else:
                slabs = [slab_ref[r, c, pl.ds(l, m, stride=CMP_STRIDE), :] for l in range(CMP_STRIDE)]
            per_req.append(jnp.concatenate(slabs, axis=1))
        lhs = jnp.concatenate(per_req, axis=0).astype(BF16)
        p = _dot(lhs, wc_ref[c])
        bias = _dot_hi(per_ref[c], wphi_ref[c])[0:1]
        hi = pltpu.roll(p[:, 128:], nr * m - 1, 0)
        outs.append(p[:, :128] + hi + bias)
    res = jnp.concatenate(outs, axis=1)
    o_ref[...] = res.reshape(nr, m, 256)


def _compress(page_table, src, wc, pe_rep, wphi2, *, col0, nr, layer):
    nb, n_pages = page_table.shape
    rows = n_pages * PAGE
    m = rows // CMP_STRIDE
    if layer is None:
        scratch = [pltpu.VMEM((2, nr, 2, rows, 128), F32), pltpu.SemaphoreType.DMA((2,))]
    else:
        scratch = [pltpu.VMEM((2, nr, 2, NSA_KV_HEADS, HEAD_DIM, rows), F32), pltpu.SemaphoreType.DMA((2,)),
                   pltpu.VMEM((nr, 2, rows, 128), F32)]
    grid_spec = pltpu.PrefetchScalarGridSpec(
        num_scalar_prefetch=1,
        grid=(nb // nr,),
        in_specs=[
            pl.BlockSpec(memory_space=pl.ANY),
            pl.BlockSpec((2, CMP_STRIDE * 128, 256), lambda i, pt: (0, 0, 0)),
            pl.BlockSpec((2, 8, L_CMP * HEAD_DIM), lambda i, pt: (0, 0, 0)),
            pl.BlockSpec((2, L_CMP * HEAD_DIM, 128), lambda i, pt: (0, 0, 0)),
        ],
        out_specs=pl.BlockSpec((nr, m, 256), lambda i, pt: (i, 0, 0)),
        scratch_shapes=scratch,
    )
    return pl.pallas_call(
        functools.partial(_compress_kernel, nr=nr, n_pages=n_pages, col0=col0, layer=layer),
        grid_spec=grid_spec,
        out_shape=jax.ShapeDtypeStruct((nb, m, 256), F32),
        compiler_params=_cparams(("arbitrary",)),
        name="compress",
    )(page_table.reshape(-1), src, wc, pe_rep, wphi2)


def _select_blocks_t(imp_t, pos, n_slc):
    ns, c = imp_t.shape
    s_i = lax.broadcasted_iota(jnp.int32, (ns, c), 0)
    cur = lax.shift_right_logical(pos, 6)
    valid = (s_i * L_SEL) <= pos
    forced = (s_i == 0) | (s_i == cur) | (s_i == cur - 1)
    vals = jnp.where(valid, imp_t + jnp.where(forced, FORCE_BONUS, 0.0), -1.0)
    vals = jnp.where(s_i < n_slc, vals, LOWEST)
    rank = jnp.zeros((ns, c), F32)
    for sp in range(n_slc):
        row = vals[sp:sp + 1, :]
        beats = (row > vals) | ((row == vals) & (s_i > sp))
        rank = rank + jnp.where(beats, 1.0, 0.0)
    return jnp.where((rank < min(N_SEL, n_slc)) & (s_i < n_slc), 1.0, 0.0)


def _importance_t(ovl_t, p_sum):
    return functools.reduce(lambda a, b: a + b, [_dot(ovl_t, part) for part in _split3(p_sum)])


def _expand_gates(ng, eg_ref):
    parts = _split3(ng)
    return [functools.reduce(lambda a, b: a + b, [_dot(part, eg_ref[t]) for part in parts]) for t in range(3)]


def _pad_rows(x, rows):
    if x.shape[0] == rows:
        return x
    return jnp.concatenate([x, jnp.zeros((rows - x.shape[0],) + x.shape[1:], x.dtype)], axis=0)


def _nsa_prompt_kernel(q_ref, pg_ref, wn_ref, kvc_ref, bct_ref, toep_ref, ovl_ref, ef_ref, eg_ref,
                       o_ref, ks_ref, vst_ref, kw_ref, vwt_ref, mft_ref, *, n_slc):
    qt = pl.program_id(1)
    t_len = pg_ref.shape[1]
    nj = kvc_ref.shape[1]

    @pl.when(qt == 0)
    def _():
        ks_ref[...] = pg_ref[0, :, 256:384].astype(BF16)
        kw_ref[...] = wn_ref[0, :, 0:128].astype(BF16)
        for blk in range(t_len // TQ):
            rs = slice(blk * TQ, (blk + 1) * TQ)
            vst_ref[:, rs] = pg_ref[0, rs, 384:512].T.astype(BF16)
            vwt_ref[:, rs] = wn_ref[0, rs, 128:256].T.astype(BF16)

    qts = [q_ref[0, :, h * 128:(h + 1) * 128].astype(F32).T.astype(BF16) for h in range(NSA_HEADS)]
    groups = [h // NSA_REP for h in range(NSA_HEADS)]

    def score_dots(k_ref, kts):
        starts = [pl.multiple_of(kt * TQ, TQ) for kt in kts]
        ks = [k_ref[pl.ds(st, TQ), :] for st in starts]
        return starts, [[_dot(k, qts[h]) for k in ks] for h in range(NSA_HEADS)]

    def softmax_update(raw, starts, tbls, masked, carry):
        stats, probs = [], []
        for h, g in enumerate(groups):
            m_i, l_i, _ = carry[h]
            ss = []
            for s, st, tbl in zip(raw[h], starts, tbls):
                s = s + toep_ref[tbl, h]
                if masked:
                    s = s + mft_ref[pl.ds(st, TQ), g * TQ:(g + 1) * TQ]
                ss.append(s)
            mx = functools.reduce(jnp.maximum, [jnp.max(s, axis=0, keepdims=True) for s in ss])
            m_new = jnp.maximum(m_i, mx)
            alpha = jnp.exp2(m_i - m_new)
            ps = [jnp.exp2(s - m_new) for s in ss]
            l_new = alpha * l_i + functools.reduce(
                lambda a, b: a + b, [jnp.sum(p, axis=0, keepdims=True) for p in ps])
            stats.append((m_new, l_new, alpha))
            probs.append(jnp.concatenate(ps, axis=0).astype(BF16))
        return stats, probs

    def value_dots(vt_ref, starts, stats, probs, carry):
        vts = [vt_ref[:, pl.ds(st, TQ)] for st in starts]
        new = []
        for h, g in enumerate(groups):
            m_new, l_new, alpha = stats[h]
            vt = jnp.concatenate([v[g * HEAD_DIM:(g + 1) * HEAD_DIM, :] for v in vts], axis=1)
            new.append((m_new, l_new, alpha * carry[h][2] + _dot(vt, probs[h])))
        return tuple(new)

    init = tuple((jnp.full((1, TQ), NEG_INF, F32), jnp.zeros((1, TQ), F32),
                  jnp.zeros((HEAD_DIM, TQ), F32)) for _ in range(NSA_HEADS))

    def finish(state):
        return [acc * (1.0 / l_f) for (_, l_f, acc) in state]

    w_tiles = WINDOW // TQ + 1
    w_kts = [qt - (w_tiles - 1) + u for u in range(w_tiles)]
    w_tbls = [jnp.where(kt < 0, 4, tb) for kt, tb in zip(w_kts, (3, 1, 0))]
    w_starts, raw_w = score_dots(kw_ref, [jnp.maximum(kt, 0) for kt in w_kts])
    kc = kvc_ref[0, :, 0:128].astype(BF16)
    vct = kvc_ref[0, :, 128:256].T.astype(BF16)
    raw_c = [_dot(kc, qts[h]) for h in range(NSA_HEADS)]

    stats_w, probs_w = softmax_update(raw_w, w_starts, w_tbls, False, init)
    p_c, p_sum = [], [None] * NSA_KV_HEADS
    for h, g in enumerate(groups):
        s = raw_c[h] + bct_ref[h]
        mx = jnp.max(s, axis=0, keepdims=True)
        e = jnp.exp2(s - mx)
        p = jnp.where(mx > 0.5 * NEG_INF, e * (1.0 / jnp.sum(e, axis=0, keepdims=True)), 0.0)
        p_c.append(p.astype(BF16))
        p_sum[g] = p if p_sum[g] is None else p_sum[g] + p

    o_w = finish(value_dots(vwt_ref, w_starts, stats_w, probs_w, init))
    o_c = [_dot(vct[g * HEAD_DIM:(g + 1) * HEAD_DIM, :], p_c[h]) for h, g in enumerate(groups)]
    imp_t = _importance_t(ovl_ref[...], jnp.concatenate(p_sum, axis=1))
    ns = -(-n_slc // 8) * 8
    pos = qt * TQ + lax.rem(lax.broadcasted_iota(jnp.int32, (1, NSA_KV_HEADS * TQ), 1), TQ)
    sel_t = _select_blocks_t(imp_t[:ns], pos, n_slc)
    hidden = ((_pad_rows(sel_t, 128) - 1.0) * (-NEG_INF)).astype(BF16)
    mft_ref[...] = _dot(ef_ref[...], hidden)

    def sel_body(it, carry):
        kts = [it * KEY_UNROLL + u for u in range(KEY_UNROLL)]
        tbls = [jnp.where(qt - kt < 0, 4, jnp.minimum(qt - kt, 2)) for kt in kts]
        starts, raw = score_dots(ks_ref, kts)
        stats, probs = softmax_update(raw, starts, tbls, True, carry)
        return value_dots(vst_ref, starts, stats, probs, carry)

    o_s = finish(lax.fori_loop(0, (qt + KEY_UNROLL) // KEY_UNROLL, sel_body, init))

    start = pl.multiple_of(qt * TQ, TQ)
    gates = _expand_gates(wn_ref[0, pl.ds(start, TQ), 256:384], eg_ref)
    tiles = []
    for mt in range(NSA_HEADS // 2):
        acc = jnp.zeros((TQ, 128), F32)
        for o, gate in zip((o_c, o_s, o_w), gates):
            pair = jnp.concatenate([o[2 * mt], o[2 * mt + 1]], axis=0)
            acc = acc + gate[:, mt * 128:(mt + 1) * 128] * pair.T
        tiles.append(acc)
    o_ref[0] = jnp.concatenate(tiles, axis=1)


def _nsa_prompt(z32, z16, kvc, bias_ct, toep_t, ovl_t, ef_t, eg):
    b, t, _ = z32.shape
    nj = kvc.shape[1]
    n_slc = -(-t // L_SEL)
    return pl.pallas_call(
        functools.partial(_nsa_prompt_kernel, n_slc=n_slc),
        grid=(b, t // TQ),
        in_specs=[
            pl.BlockSpec((1, TQ, 1024), lambda i, j: (i, j, CB16_Q)),
            pl.BlockSpec((1, t, TN), lambda i, j: (i, 0, CB32_PAGED)),
            pl.BlockSpec((1, t, TN), lambda i, j: (i, 0, CB32_WN)),
            pl.BlockSpec((1, nj, 256), lambda i, j: (i, 0, 0)),
            pl.BlockSpec((NSA_HEADS, nj, TQ), lambda i, j: (0, 0, j)),
            pl.BlockSpec((5, NSA_HEADS, TQ, TQ), lambda i, j: (0, 0, 0, 0)),
            pl.BlockSpec((128, nj), lambda i, j: (0, 0)),
            pl.BlockSpec((t, 128), lambda i, j: (0, 0)),
            pl.BlockSpec((3, 128, 512), lambda i, j: (0, 0, 0)),
        ],
        out_specs=pl.BlockSpec((1, TQ, 512), lambda i, j: (i, j, 0)),
        out_shape=jax.ShapeDtypeStruct((b, t, 512), F32),
        scratch_shapes=[pltpu.VMEM((t, 128), BF16), pltpu.VMEM((128, t), BF16),
                        pltpu.VMEM((t, 128), BF16), pltpu.VMEM((128, t), BF16),
                        pltpu.VMEM((t, NSA_KV_HEADS * TQ), F32)],
        compiler_params=_cparams(("parallel", "arbitrary")),
        name="nsa_prompt",
    )(z16, z32, z32, kvc, bias_ct, toep_t, ovl_t, ef_t, eg)


def _slc_copies(pt_ref, src_ref, buf_ref, sem_ref, req, slot, *, n_pages, layer):
    copies = []
    for p in range(n_pages):
        pid = pt_ref[req * n_pages + p]
        copies.append(pltpu.make_async_copy(
            src_ref.at[layer, pid, pl.ds(2, 2)],
            buf_ref.at[slot, :, :, :, pl.ds(p * PAGE, PAGE)],
            sem_ref.at[slot]))
    return copies


def _nsa_sample_kernel(pt_ref, src_ref, q_ref, pg_ref, wn_ref, cw_ref, kvc_ref, bs_ref, bn_ref, bw_ref,
                       bcs_ref, ovl_ref, ef_ref, eg_ref, o_ref, buf_ref, sem_ref,
                       *, n_pages, n_slc, pos0, layer):
    req = pl.program_id(0)
    n_req = pl.num_programs(0)
    slot = lax.rem(req, 2)
    tq = q_ref.shape[1]
    rows = NSA_HEADS * tq
    nj = kvc_ref.shape[1]
    past = n_pages * PAGE
    kw = dict(n_pages=n_pages, layer=layer)

    @pl.when(req == 0)
    def _():
        for cp in _slc_copies(pt_ref, src_ref, buf_ref, sem_ref, req, slot, **kw):
            cp.start()

    @pl.when(req + 1 < n_req)
    def _():
        for cp in _slc_copies(pt_ref, src_ref, buf_ref, sem_ref, req + 1, 1 - slot, **kw):
            cp.start()

    qf = q_ref[0].astype(F32)
    q = jnp.concatenate([qf[:, h * 128:(h + 1) * 128] for h in range(NSA_HEADS)], axis=0).astype(BF16)

    kc = kvc_ref[0, :, 0:128].astype(BF16)
    vc = kvc_ref[0, :, 128:256].astype(BF16)
    s_c = _dot_nt(q, kc) + bcs_ref[...]
    mx = jnp.max(s_c, axis=-1, keepdims=True)
    e_c = jnp.exp2(s_c - mx)
    p_c = jnp.where(mx > 0.5 * NEG_INF, e_c / jnp.sum(e_c, axis=-1, keepdims=True), 0.0)
    o_c = _dot(p_c.astype(BF16), vc)
    p_g = jnp.sum(p_c.reshape(NSA_KV_HEADS, NSA_REP, tq, nj), axis=1).reshape(NSA_KV_HEADS * tq, nj)
    imp_t = functools.reduce(lambda a, b: a + b,
                             [_dot_nt(ovl_ref[...], part) for part in _split3(_pad_rows(p_g, 128))])
    ns = -(-n_slc // 8) * 8
    lane = lax.broadcasted_iota(jnp.int32, (1, 128), 1)
    sel_t = _select_blocks_t(imp_t[:ns], pos0 + lax.rem(lane, tq), n_slc)
    sel = _pad_rows(sel_t, 128).T[:NSA_KV_HEADS * tq]
    mf = _dot(sel.astype(BF16), ef_ref[...])
    mf = jnp.broadcast_to(mf.reshape(NSA_KV_HEADS, 1, tq, past),
                          (NSA_KV_HEADS, NSA_REP, tq, past)).reshape(rows, past)

    def new_rows(x):
        return _pad_rows(x, 128).astype(BF16)

    def two_part(kt_old, vt_old, bias_old, k_new, v_new):
        s_o = _dot(q, kt_old) + bias_old
        s_n = _dot_nt(q, new_rows(k_new)) + bn_ref[...]
        m2 = jnp.maximum(jnp.max(s_o, axis=-1, keepdims=True), jnp.max(s_n, axis=-1, keepdims=True))
        e_o = jnp.exp2(s_o - m2)
        e_n = jnp.exp2(s_n - m2)
        den = jnp.sum(e_o, axis=-1, keepdims=True) + jnp.sum(e_n, axis=-1, keepdims=True)
        return (_dot_nt(e_o.astype(BF16), vt_old) + _dot(e_n.astype(BF16), new_rows(v_new))) / den

    pg = pg_ref[0]
    wn = wn_ref[0]
    wb = cw_ref.shape[5]
    o_w = two_part(cw_ref[0, 0, 0].reshape(128, wb).astype(BF16), cw_ref[0, 0, 1].reshape(128, wb).astype(BF16),
                   bw_ref[...], wn[:, 0:128], wn[:, 128:256])

    for cp in _slc_copies(pt_ref, src_ref, buf_ref, sem_ref, req, slot, **kw):
        cp.wait()

    bias_s = jnp.where(mf > 0.5, bs_ref[...], NEG_INF)
    o_s = two_part(buf_ref[slot, 0].reshape(128, past).astype(BF16),
                   buf_ref[slot, 1].reshape(128, past).astype(BF16), bias_s, pg[:, 256:384], pg[:, 384:512])

    gates = _expand_gates(wn[:, 256:384], eg_ref)
    lane128 = lax.broadcasted_iota(jnp.int32, (tq, 128), 1)
    tiles = []
    for mt in range(NSA_HEADS // 2):
        g = (2 * mt) // NSA_REP
        acc = jnp.zeros((tq, 128), F32)
        for o, gate in zip((o_c, o_s, o_w), gates):
            a = o[(2 * mt) * tq:(2 * mt + 1) * tq]
            b = o[(2 * mt + 1) * tq:(2 * mt + 2) * tq]
            if g == 1:
                a = pltpu.roll(a, 64, 1)
            else:
                b = pltpu.roll(b, 64, 1)
            acc = acc + gate[:, mt * 128:(mt + 1) * 128] * jnp.where(lane128 < 64, a, b)
        tiles.append(acc)
    o_ref[0] = jnp.concatenate(tiles, axis=1)


def _nsa_sample(page_table, cache_t, cwin_t, layer, z32, z16, kvc, bias_s, bias_n, bias_w, bias_cs, ovl_t, ef, eg):
    nb, n_pages = page_table.shape
    past = n_pages * PAGE
    tq = z32.shape[1]
    rows = NSA_HEADS * tq
    nj = kvc.shape[1]
    n_slc = -(-(past + tq) // L_SEL)
    wb = cwin_t.shape[5]
    const2 = lambda i, pt: (0, 0)
    grid_spec = pltpu.PrefetchScalarGridSpec(
        num_scalar_prefetch=1,
        grid=(nb,),
        in_specs=[
            pl.BlockSpec(memory_space=pl.ANY),
            pl.BlockSpec((1, tq, 1024), lambda i, pt: (i, 0, CB16_Q)),
            pl.BlockSpec((1, tq, TN), lambda i, pt: (i, 0, CB32_PAGED)),
            pl.BlockSpec((1, tq, TN), lambda i, pt: (i, 0, CB32_WN)),
            pl.BlockSpec((1, 1, 2, NSA_KV_HEADS, HEAD_DIM, wb), lambda i, pt: (layer, i, 0, 0, 0, 0)),
            pl.BlockSpec((1, nj, 256), lambda i, pt: (i, 0, 0)),
            pl.BlockSpec((rows, past), const2),
            pl.BlockSpec((rows, 128), const2),
            pl.BlockSpec((rows, wb), const2),
            pl.BlockSpec((rows, nj), const2),
            pl.BlockSpec((128, nj), const2),
            pl.BlockSpec((128, past), const2),
            pl.BlockSpec((3, 128, 512), lambda i, pt: (0, 0, 0)),
        ],
        out_specs=pl.BlockSpec((1, tq, 512), lambda i, pt: (i, 0, 0)),
        scratch_shapes=[pltpu.VMEM((2, 2, NSA_KV_HEADS, HEAD_DIM, past), F32), pltpu.SemaphoreType.DMA((2,))],
    )
    return pl.pallas_call(
        functools.partial(_nsa_sample_kernel, n_pages=n_pages, n_slc=n_slc, pos0=past, layer=layer),
        grid_spec=grid_spec,
        out_shape=jax.ShapeDtypeStruct((nb, tq, 512), F32),
        compiler_params=_cparams(("arbitrary",)),
        name="nsa_sample",
    )(page_table.reshape(-1), cache_t, z16, z32, z32, cwin_t, kvc, bias_s, bias_n, bias_w, bias_cs,
      ovl_t, ef, eg)


def _mem_attn_kernel(q_ref, kv_ref, o_ref):
    q = q_ref[0]
    heads = range(MEM_HEADS)
    scores = [_dot_nt(q[:, h * 128:(h + 1) * 128], kv_ref[0, :, h * 128:(h + 1) * 128].astype(BF16))
              for h in heads]
    probs = []
    for s in scores:
        s = s * (MEM_HEAD_DIM ** -0.5)
        e = jnp.exp(s - jnp.max(s, axis=-1, keepdims=True))
        probs.append((e * (1.0 / jnp.sum(e, axis=-1, keepdims=True))).astype(BF16))
    o_ref[0] = jnp.concatenate(
        [_dot(probs[h], kv_ref[0, :, MEM_WIDTH + h * 128:MEM_WIDTH + (h + 1) * 128].astype(BF16))
         for h in heads], axis=1)


def _mem_attn(z16, mem_kv, tq):
    b, t, _ = z16.shape
    n_mem = mem_kv.shape[1]
    return pl.pallas_call(
        _mem_attn_kernel,
        grid=(b, t // tq),
        in_specs=[
            pl.BlockSpec((1, tq, TN), lambda i, j: (i, j, CB16_MQ)),
            pl.BlockSpec((1, n_mem, 2 * MEM_WIDTH), lambda i, j: (i, 0, 0)),
        ],
        out_specs=pl.BlockSpec((1, tq, MEM_WIDTH), lambda i, j: (i, j, 0)),
        out_shape=jax.ShapeDtypeStruct((b, t, MEM_WIDTH), F32),
        compiler_params=_cparams(("parallel", "parallel")),
        name="mem_attn",
    )(z16, mem_kv)


def _mem_attn_cache_kernel(q_ref, kv_ref, o_ref):
    nr, tq, _ = q_ref.shape
    n_mem = kv_ref.shape[2] // (2 * MEM_HEADS)
    pairs = [(r, h) for r in range(nr) for h in range(MEM_HEADS)]
    qs = [q_ref[r].astype(F32) for r in range(nr)]
    scores = []
    for r, h in pairs:
        kh = kv_ref[0, r, pl.ds(h, n_mem, stride=2 * MEM_HEADS), :].astype(BF16)
        scores.append(_dot_nt(qs[r][:, h * 128:(h + 1) * 128].astype(BF16), kh))
    s = jnp.concatenate(scores, axis=0) * (MEM_HEAD_DIM ** -0.5)
    mx = jnp.max(s, axis=-1, keepdims=True)
    e = jnp.exp(s - mx)
    p = e * (1.0 / jnp.sum(e, axis=-1, keepdims=True))
    outs = []
    for i, (r, h) in enumerate(pairs):
        vh = kv_ref[0, r, pl.ds(MEM_HEADS + h, n_mem, stride=2 * MEM_HEADS), :].astype(BF16)
        outs.append(_dot(p[i * tq:(i + 1) * tq].astype(BF16), vh))
    for r in range(nr):
        o_ref[r] = jnp.concatenate(outs[r * MEM_HEADS:(r + 1) * MEM_HEADS], axis=1)


def _mem_attn_cache(z16, cache_mem, layer, nr):
    b, tq, _ = z16.shape
    nl, _, n_mem = cache_mem.shape[:3]
    rows = n_mem * 2 * MEM_HEADS
    return pl.pallas_call(
        _mem_attn_cache_kernel,
        grid=(b // nr,),
        in_specs=[
            pl.BlockSpec((nr, tq, TN), lambda i: (i, 0, CB16_MQ)),
            pl.BlockSpec((1, nr, rows, MEM_HEAD_DIM), lambda i: (layer, i, 0, 0)),
        ],
        out_specs=pl.BlockSpec((nr, tq, MEM_WIDTH), lambda i: (i, 0, 0)),
        out_shape=jax.ShapeDtypeStruct((b, tq, MEM_WIDTH), F32),
        compiler_params=_cparams(("parallel",)),
        name="mem_attn_cache",
    )(z16, cache_mem.reshape(nl, b, rows, MEM_HEAD_DIM))


def _merge_kernel(x_ref, u_ref, v_ref, g0_ref, g1_ref, g2_ref, on_ref, om_ref, wsp_ref, bsp_ref,
                  wa_ref, wb_ref, wc_ref, wo_ref, o_ref):
    tm = x_ref.shape[0]
    u = u_ref[...].astype(F32)
    v = v_ref[...].astype(BF16)
    chunks = []
    for c in range(tm // CHUNK):
        parts = []
        for g in range(A_GROUPS):
            vg = v[c * CHUNK:(c + 1) * CHUNK, g * 128:(g + 1) * 128]
            parts.append(_dot(wsp_ref[g], vg) + bsp_ref[g])
        chunks.append(jnp.concatenate(parts, axis=1))
    ya = (u * jnp.concatenate(chunks, axis=0)).astype(BF16)
    merged = (g0_ref[...].astype(F32) * _dot(ya, wa_ref[...])
              + g1_ref[...].astype(F32) * _dot(on_ref[...].astype(BF16), wb_ref[...])
              + g2_ref[...].astype(F32) * _dot(om_ref[...].astype(BF16), wc_ref[...]))
    o_ref[...] = x_ref[...] + _dot(merged.astype(BF16), wo_ref[...])


def _merge(x, z32, z16, o_nsa, o_mem, wsp, bsp, w_a, w_b, w_c, w_out, tm):
    m, d = x.shape
    row = lambda i: (i, 0)
    c2 = lambda i: (0, 0)
    c3 = lambda i: (0, 0, 0)
    return pl.pallas_call(
        _merge_kernel,
        grid=(m // tm,),
        in_specs=[
            pl.BlockSpec((tm, d), row),
            pl.BlockSpec((tm, TN), lambda i: (i, CB16_U)),
            pl.BlockSpec((tm, TN), lambda i: (i, CB32_V)),
            pl.BlockSpec((tm, d), lambda i: (i, CB16_BG)),
            pl.BlockSpec((tm, d), lambda i: (i, CB16_BG + 1)),
            pl.BlockSpec((tm, d), lambda i: (i, CB16_BG + 2)),
            pl.BlockSpec((tm, 512), row),
            pl.BlockSpec((tm, 512), row),
            pl.BlockSpec((A_GROUPS, CHUNK, CHUNK), c3),
            pl.BlockSpec((A_GROUPS, CHUNK, 128), c3),
            pl.BlockSpec((A_WIDTH, d), c2),
            pl.BlockSpec((512, d), c2),
            pl.BlockSpec((MEM_WIDTH, d), c2),
            pl.BlockSpec((d, d), c2),
        ],
        out_specs=pl.BlockSpec((tm, d), row),
        out_shape=jax.ShapeDtypeStruct((m, d), F32),
        compiler_params=_cparams(("parallel",)),
        name="merge",
    )(x, z16, z32, z16, z16, z16, o_nsa, o_mem, wsp, bsp, w_a, w_b, w_c, w_out)


def _mlp_kernel(x_ref, g_ref, wu_ref, wd_ref, o_ref, hn_ref, acc_ref):
    f = pl.program_id(1)

    @pl.when(f == 0)
    def _():
        hn_ref[...] = _rms(x_ref[...], g_ref[...]).astype(BF16)
        acc_ref[...] = jnp.zeros_like(acc_ref)

    h = jnp.square(jnp.maximum(_dot(hn_ref[...], wu_ref[...]), 0.0))
    acc_ref[...] += _dot(h.astype(BF16), wd_ref[...])

    @pl.when(f == pl.num_programs(1) - 1)
    def _():
        o_ref[...] = x_ref[...] + acc_ref[...]


def _mlp(x, gain, w_up, w_down, tm, tf):
    m, d = x.shape
    ff = w_up.shape[1]
    return pl.pallas_call(
        _mlp_kernel,
        grid=(m // tm, ff // tf),
        in_specs=[
            pl.BlockSpec((tm, d), lambda i, j: (i, 0)),
            pl.BlockSpec((1, d), lambda i, j: (0, 0)),
            pl.BlockSpec((d, tf), lambda i, j: (0, j)),
            pl.BlockSpec((tf, d), lambda i, j: (j, 0)),
        ],
        out_specs=pl.BlockSpec((tm, d), lambda i, j: (i, 0)),
        out_shape=jax.ShapeDtypeStruct((m, d), F32),
        scratch_shapes=[pltpu.VMEM((tm, d), BF16), pltpu.VMEM((tm, d), F32)],
        compiler_params=_cparams(("parallel", "arbitrary")),
        name="mlp",
    )(x, gain.reshape(1, d), w_up, w_down)


def _final_norm_kernel(x_ref, g_ref, o_ref):
    o_ref[...] = _rms(x_ref[...], g_ref[...])


def _final_norm(x, gain, tm):
    m, d = x.shape
    return pl.pallas_call(
        _final_norm_kernel,
        grid=(m // tm,),
        in_specs=[pl.BlockSpec((tm, d), lambda i: (i, 0)), pl.BlockSpec((1, d), lambda i: (0, 0))],
        out_specs=pl.BlockSpec((tm, d), lambda i: (i, 0)),
        out_shape=jax.ShapeDtypeStruct((m, d), F32),
        compiler_params=_cparams(("parallel",)),
        name="final_norm",
    )(x, gain.reshape(1, d))


def _relayout_w_in(w_in):
    nl, d, _ = w_in.shape
    wq = w_in[:, :, OFF_Q:OFF_KV].reshape(nl, d, NSA_HEADS, 1, HEAD_DIM) * ((HEAD_DIM ** -0.5) * LOG2E)
    grp = (np.arange(NSA_HEADS) // NSA_REP)[:, None] == np.arange(NSA_KV_HEADS)[None, :]
    wq = jnp.where(jnp.asarray(grp)[None, None, :, :, None], wq, 0.0).reshape(nl, d, NSA_HEADS * 128)
    pad = jnp.zeros((nl, d, TN - 256 - (OFF_MQ - OFF_NG)), w_in.dtype)
    half_a = A_WIDTH
    parts = [w_in[:, :, half_a:OFF_Q], w_in[:, :, OFF_KV:OFF_KV + 512], w_in[:, :, OFF_KV + 512:OFF_MQ], pad,
             w_in[:, :, OFF_BG:], wq, w_in[:, :, :half_a], w_in[:, :, OFF_MQ:OFF_BG]]
    out = jnp.concatenate(parts, axis=2).astype(BF16)
    assert out.shape[2] == Z32_COLS + Z16_COLS
    return out


def _compress_weights(w_phi, pe_cmp):
    nl = w_phi.shape[0]
    w5 = w_phi.reshape(nl, 2, CMP_STRIDE, 2, HEAD_DIM, HEAD_DIM)
    eye = jnp.eye(NSA_KV_HEADS, dtype=w_phi.dtype)
    wc = jnp.einsum("nhlcde,gk->nclgdhke", w5, eye).reshape(nl, 2, CMP_STRIDE * 128, 256).astype(BF16)
    wphi2 = jnp.transpose(w_phi, (0, 2, 1, 3, 4)).reshape(nl, 2, L_CMP * HEAD_DIM, HEAD_DIM)
    wphi2 = jnp.concatenate([wphi2, wphi2], axis=-1)
    pe2 = jnp.transpose(pe_cmp, (0, 2, 1, 3)).reshape(nl, 2, 1, L_CMP * HEAD_DIM)
    pe_rep = jnp.broadcast_to(pe2, (nl, 2, 8, L_CMP * HEAD_DIM))
    return wc, pe_rep, wphi2


def _overlap_t(nj, n_cmp, n_slc):
    ss = np.arange(128)[:, None]
    jj = np.arange(nj)[None, :]
    ov = ((jj * CMP_STRIDE <= (ss + 1) * L_SEL - 1) & (jj * CMP_STRIDE + L_CMP - 1 >= ss * L_SEL)
          & (jj < n_cmp) & (ss < n_slc))
    return jnp.asarray(ov.astype(np.float32)).astype(BF16)


def _block_expand(n_keys):
    e = np.arange(128)[:, None] == (np.arange(n_keys)[None, :] // L_SEL)
    return jnp.asarray(e.astype(np.float32)).astype(BF16)


def _gate_expand():
    e = np.zeros((3, 128, 512), np.float32)
    for t in range(3):
        for h in range(NSA_HEADS):
            e[t, 3 * h + t, h * HEAD_DIM:(h + 1) * HEAD_DIM] = 1.0
    return jnp.asarray(e).astype(BF16)


def _pick_tile(m, pref):
    t = min(m, pref)
    while m % t:
        t //= 2
    return t


def kernel(x_prompt, x_sample, cache_kv, cache_win_kv, cache_mem_kv, page_table, mem_prompt,
           ln1, w_in, g_v, w_s, b_s, w_a, pe_cmp, w_phi, w_b, ln_mem, w_mem_kv, w_c, w_out,
           ln2, w_up, w_down, rel_bias, ln_f):
    bp, t, d = x_prompt.shape
    bs, ts, _ = x_sample.shape
    n_pages = page_table.shape[1]
    past = n_pages * PAGE
    n_mem = mem_prompt.shape[1]
    wb = cache_win_kv.shape[2]
    assert t % (KEY_UNROLL * TQ) == 0 and t % CHUNK == 0 and past % L_SEL == 0 and ts <= L_SEL and CHUNK % ts == 0
    assert (past + ts - L_CMP) // CMP_STRIDE + 1 <= past // CMP_STRIDE
    assert wb == WINDOW and t >= WINDOW

    w_in_p = _relayout_w_in(w_in)
    wc_all, pe_rep_all, wphi2_all = _compress_weights(w_phi, pe_cmp)
    w_a16, w_b16, w_c16, w_out16 = (w.astype(BF16) for w in (w_a, w_b, w_c, w_out))
    w_up16, w_down16, w_mem16 = w_up.astype(BF16), w_down.astype(BF16), w_mem_kv.astype(BF16)
    causal = np.tril(np.ones((CHUNK, CHUNK), bool))
    wsp_p = jnp.where(jnp.asarray(causal), w_s, 0.0)
    blockdiag = np.kron(np.eye(CHUNK // ts), np.ones((ts, ts))) > 0
    wsp_s = jnp.where(jnp.asarray(blockdiag & causal),
                      jnp.tile(w_s[:, :, :ts, :ts], (1, 1, CHUNK // ts, CHUNK // ts)), 0.0)
    bsp_p = jnp.broadcast_to(b_s[:, :, :, None], b_s.shape + (128,))
    bsp_s = jnp.broadcast_to(jnp.tile(b_s[:, :, :ts], (1, 1, CHUNK // ts))[:, :, :, None], b_s.shape + (128,))
    wsp_p, wsp_s = wsp_p.astype(BF16), wsp_s.astype(BF16)

    cache_t = jnp.transpose(cache_kv, (0, 1, 3, 4, 5, 2))
    cwin_t = jnp.transpose(cache_win_kv, (0, 1, 3, 4, 5, 2))

    toep_t, bias_ct, bias_s, bias_n, bias_w, bias_cs = _tables(rel_bias, t, past, ts, wb)
    nj_p, nj_s = t // CMP_STRIDE, past // CMP_STRIDE
    ovl_p = _overlap_t(nj_p, (t - L_CMP) // CMP_STRIDE + 1, -(-t // L_SEL))
    ovl_s = _overlap_t(nj_s, (past + ts - L_CMP) // CMP_STRIDE + 1, -(-(past + ts) // L_SEL))
    ef_p_t = _block_expand(t).T
    ef_s = _block_expand(past)
    eg = _gate_expand()
    pt_prompt = jnp.arange(bp * (t // PAGE), dtype=jnp.int32).reshape(bp, t // PAGE)

    np_tok, ns_tok = bp * t, bs * ts
    tm_p = _pick_tile(np_tok, 1024)
    tm_in = _pick_tile(np_tok, 2048)
    tm_s = _pick_tile(ns_tok, 1024)
    xp = x_prompt.reshape(np_tok, d)
    xs = x_sample.reshape(ns_tok, d)
    mem_flat = mem_prompt.reshape(bp * n_mem, d)
    zeros_gv = jnp.zeros((TN,), F32)

    kv_p, win_p, mem_p, kv_s, win_s, v_s = [], [], [], [], [], []
    for l in range(DEPTH):
        mem_kv, _ = _norm_matmul(mem_flat, ln_mem[l], w_mem16[l], zeros_gv, ("none", "none"), 2,
                                 _pick_tile(bp * n_mem, 1024), "mem_kv_proj")
        zp32, zp16 = _norm_matmul(xp, ln1[l], w_in_p[l], g_v[l], TILE_KINDS, N_F32_TILES, tm_in, "in_proj")
        zp32_3, zp16_3 = zp32.reshape(bp, t, Z32_COLS), zp16.reshape(bp, t, Z16_COLS)
        kvc_p = _compress(pt_prompt, zp32.reshape(np_tok // PAGE, PAGE, Z32_COLS), wc_all[l], pe_rep_all[l],
                          wphi2_all[l], col0=COL_PAGED, nr=_pick_tile(bp, 4), layer=None)
        o_nsa = _nsa_prompt(zp32_3, zp16_3, kvc_p, bias_ct, toep_t, ovl_p, ef_p_t, eg)
        o_mem = _mem_attn(zp16_3, mem_kv.reshape(bp, n_mem, 2 * MEM_WIDTH), _pick_tile(t, 512))
        x1 = _merge(xp, zp32, zp16, o_nsa.reshape(np_tok, 512), o_mem.reshape(np_tok, MEM_WIDTH), wsp_p[l],
                    bsp_p[l], w_a16[l], w_b16[l], w_c16[l], w_out16[l], _pick_tile(np_tok, 256))
        xp = _mlp(x1, ln2[l], w_up16[l], w_down16[l], tm_p, 1024)
        kv_p.append(zp32_3[:, :, COL_PAGED:COL_PAGED + 512].reshape(bp, t, 4, NSA_KV_HEADS, HEAD_DIM))
        win_p.append(zp32_3[:, t - WINDOW:, COL_WN:COL_WN + 256].reshape(bp, WINDOW, 2, NSA_KV_HEADS, HEAD_DIM))
        mem_p.append(mem_kv.reshape(bp, n_mem, 2, MEM_HEADS, MEM_HEAD_DIM))

        zs32, zs16 = _norm_matmul(xs, ln1[l], w_in_p[l], g_v[l], TILE_KINDS, N_F32_TILES, tm_s, "in_proj")
        zs32_3, zs16_3 = zs32.reshape(bs, ts, Z32_COLS), zs16.reshape(bs, ts, Z16_COLS)
        kvc_s = _compress(page_table, cache_t, wc_all[l], pe_rep_all[l], wphi2_all[l], col0=0,
                          nr=_pick_tile(bs, 4), layer=l)
        o_nsa = _nsa_sample(page_table, cache_t, cwin_t, l, zs32_3, zs16_3, kvc_s, bias_s, bias_n, bias_w,
                            bias_cs, ovl_s, ef_s, eg)
        o_mem = _mem_attn_cache(zs16_3, cache_mem_kv, l, _pick_tile(bs, 8))
        x1 = _merge(xs, zs32, zs16, o_nsa.reshape(ns_tok, 512), o_mem.reshape(ns_tok, MEM_WIDTH), wsp_s[l],
                    bsp_s[l], w_a16[l], w_b16[l], w_c16[l], w_out16[l], _pick_tile(ns_tok, 256))
        xs = _mlp(x1, ln2[l], w_up16[l], w_down16[l], tm_s, 1024)
        kv_s.append(zs32_3[:, :, COL_PAGED:COL_PAGED + 512].reshape(bs, ts, 4, NSA_KV_HEADS, HEAD_DIM))
        win_new = zs32_3[:, :, COL_WN:COL_WN + 256].reshape(bs, ts, 2, NSA_KV_HEADS, HEAD_DIM)
        win_s.append(jnp.concatenate([cache_win_kv[l][:, ts:], win_new], axis=1))
        v_s.append(zs32_3[:, :, CB32_V * TN:(CB32_V + 1) * TN])

    y_prompt = _final_norm(xp, ln_f, tm_p).reshape(bp, t, d)
    y_sample = _final_norm(xs, ln_f, tm_s).reshape(bs, ts, d)
    return (y_prompt, y_sample, jnp.stack(kv_p), jnp.stack(win_p), jnp.stack(mem_p),
            jnp.stack(kv_s), jnp.stack(win_s), jnp.stack(v_s))
```

```python
import functools
import math

import numpy as np
import jax
import jax.numpy as jnp
from jax import lax
from jax.experimental import pallas as pl
from jax.experimental.pallas import tpu as pltpu

F32 = jnp.float32
BF16 = jnp.bfloat16

EPS = 1e-6
LOG2E = math.log2(math.e)
NEG_INF = -1e30
LOWEST = -3e38

D_MODEL = 1024
DEPTH = 4
PAGE = 128
CHUNK = 128
A_GROUPS = 4
A_WIDTH = 512
NSA_HEADS = 8
NSA_KV_HEADS = 2
NSA_REP = 4
HEAD_DIM = 64
L_CMP = 32
CMP_STRIDE = 16
L_SEL = 64
N_SEL = 8
WINDOW = 256
FORCE_BONUS = 1e3
MEM_HEADS = 4
MEM_HEAD_DIM = 128
MEM_WIDTH = 512
NUM_BUCKETS = 32
MAX_DISTANCE = 128
D_FF = 4096
OFF_Q = 1024
OFF_KV = 1536
OFF_NG = 2304
OFF_MQ = 2328
OFF_BG = 2840

TN = 512
TILE_KINDS = ("gelu_norm", "none", "half_sig",
              "sig", "sig", "sig", "sig", "sig", "sig", "none", "none", "gelu", "none")
N_F32_TILES = 3
Z32_COLS = TN * N_F32_TILES
Z16_COLS = TN * (len(TILE_KINDS) - N_F32_TILES)
CB32_V = 0
CB32_PAGED = 1
CB32_WN = 2
CB16_BG = 0
CB16_Q = 3
CB16_U = 8
CB16_MQ = 9
COL_PAGED = CB32_PAGED * TN
COL_WN = CB32_WN * TN

TQ = 128
KEY_UNROLL = 4
ROW_CHUNK = 256
VMEM_LIMIT = 56 * 1024 * 1024


def _cparams(sem):
    return pltpu.CompilerParams(dimension_semantics=sem, vmem_limit_bytes=VMEM_LIMIT)


def _rms(x, g):
    return x * lax.rsqrt(jnp.mean(x * x, axis=-1, keepdims=True) + EPS) * g


def _dot(a, b):
    return jnp.dot(a, b, preferred_element_type=F32)


def _dot_hi(a, b):
    return jnp.dot(a, b, preferred_element_type=F32, precision=lax.Precision.HIGHEST)


def _dot_nt(a, b):
    return lax.dot_general(a, b, (((1,), (1,)), ((), ())), preferred_element_type=F32)


def _split3(x):
    hi = x.astype(BF16)
    r1 = x - hi.astype(F32)
    mid = r1.astype(BF16)
    lo = (r1 - mid.astype(F32)).astype(BF16)
    return hi, mid, lo


def _bucket(dist):
    n = jnp.maximum(dist, 0)
    max_exact = NUM_BUCKETS // 2
    nf = jnp.maximum(n, 1).astype(F32)
    large = max_exact + (jnp.log(nf / max_exact) / math.log(MAX_DISTANCE / max_exact)
                         * (NUM_BUCKETS - max_exact)).astype(jnp.int32)
    return jnp.where(n < max_exact, n, jnp.minimum(large, NUM_BUCKETS - 1))


def _bias_table(rel_ref, o_ref, dist, valid, lead=(), mult=1.0):
    bucket = _bucket(dist)
    for h in range(NSA_HEADS):
        out = jnp.full(dist.shape, NEG_INF, F32)
        for b in range(NUM_BUCKETS):
            out = jnp.where(bucket == b, rel_ref[b, h], out)
        o_ref[lead + (h,)] = jnp.where(valid, out * mult, NEG_INF)


def _tables_kernel(rel_ref, toep_ref, tpl_ref, bs_ref, bn_ref, bw_ref, bcs_ref, *, past, n_cmp_s):
    def iota(shape, axis):
        return lax.broadcasted_iota(jnp.int32, shape, axis)

    sh = (TQ, TQ)
    d = iota(sh, 1) - iota(sh, 0)
    yes = d > -10 * TQ
    _bias_table(rel_ref, toep_ref, d, d >= 0, (0,), LOG2E)
    _bias_table(rel_ref, toep_ref, d + TQ, yes, (1,), LOG2E)
    _bias_table(rel_ref, toep_ref, d + 2 * TQ, yes, (2,), LOG2E)
    _bias_table(rel_ref, toep_ref, d + 2 * TQ, d + 2 * TQ < WINDOW, (3,), LOG2E)
    _bias_table(rel_ref, toep_ref, d, d > 10 * TQ, (4,), LOG2E)
    nj2 = tpl_ref.shape[1]
    sh = (nj2, TQ)
    dc = iota(sh, 1) - CMP_STRIDE * (iota(sh, 0) - nj2 // 2) - (L_CMP - 1)
    _bias_table(rel_ref, tpl_ref, dc, dc >= 0, (), LOG2E)
    ts = bs_ref.shape[1]
    sh = (ts, past)
    _bias_table(rel_ref, bs_ref, past + iota(sh, 0) - iota(sh, 1), iota(sh, 0) >= 0, (), LOG2E)
    sh = (ts, 128)
    dn = iota(sh, 0) - iota(sh, 1)
    _bias_table(rel_ref, bn_ref, dn, (dn >= 0) & (iota(sh, 1) < ts), (), LOG2E)
    wb = bw_ref.shape[2]
    sh = (ts, wb)
    dw = wb + iota(sh, 0) - iota(sh, 1)
    _bias_table(rel_ref, bw_ref, dw, dw < WINDOW, (), LOG2E)
    njs = bcs_ref.shape[2]
    sh = (ts, njs)
    dcs = past + iota(sh, 0) - CMP_STRIDE * iota(sh, 1) - (L_CMP - 1)
    _bias_table(rel_ref, bcs_ref, dcs, (dcs >= 0) & (iota(sh, 1) < n_cmp_s), (), LOG2E)


def _tables(rel_bias, t, past, ts, wb):
    nj = t // CMP_STRIDE
    njs = past // CMP_STRIDE
    n_cmp_s = (past + ts - L_CMP) // CMP_STRIDE + 1
    h = NSA_HEADS
    shapes = [(5, h, TQ, TQ), (h, 2 * nj, TQ), (h, ts, past), (h, ts, 128), (h, ts, wb), (h, ts, njs)]
    toep_t, tpl, bias_s, bias_n, bias_w, bias_cs = pl.pallas_call(
        functools.partial(_tables_kernel, past=past, n_cmp_s=n_cmp_s),
        in_specs=[pl.BlockSpec(memory_space=pltpu.SMEM)],
        out_shape=[jax.ShapeDtypeStruct(s, F32) for s in shapes],
        name="bias_tables",
    )(rel_bias)
    bias_ct = jnp.concatenate([tpl[:, nj - 8 * qt:2 * nj - 8 * qt, :] for qt in range(t // TQ)], axis=2)
    rows = h * ts
    return (toep_t, bias_ct, bias_s.reshape(rows, past), bias_n.reshape(rows, 128),
            bias_w.reshape(rows, wb), bias_cs.reshape(rows, njs))


def _norm_matmul_kernel(x_ref, g_ref, w_ref, gv_ref, *rest, kinds, n_f32):
    out_refs, hn_ref = rest[:-1], rest[-1]
    n = pl.program_id(1)
    tm = x_ref.shape[0]
    rc = min(ROW_CHUNK, tm)

    @pl.when(n == 0)
    def _():
        hn_ref[...] = _rms(x_ref[...], g_ref[...]).astype(BF16)

    for kind, wide in sorted(set((k, i < n_f32) for i, k in enumerate(kinds))):
        tiles = [i for i, k in enumerate(kinds) if k == kind and (i < n_f32) == wide]
        cond = functools.reduce(jnp.logical_or, [n == i for i in tiles])
        o_ref = out_refs[0] if wide else out_refs[1]

        @pl.when(cond)
        def _(kind=kind, o_ref=o_ref):
            for r in range(tm // rc):
                acc = _dot(hn_ref[r * rc:(r + 1) * rc, :], w_ref[...])
                if kind == "gelu":
                    acc = jax.nn.gelu(acc)
                elif kind == "gelu_norm":
                    acc = _rms(jax.nn.gelu(acc), gv_ref[...])
                elif kind == "sig":
                    acc = jax.nn.sigmoid(acc)
                elif kind == "half_sig":
                    half = acc.shape[1] // 2
                    acc = jnp.concatenate([acc[:, :half], jax.nn.sigmoid(acc[:, half:])], axis=1)
                o_ref[r * rc:(r + 1) * rc, :] = acc.astype(o_ref.dtype)


def _norm_matmul(x, gain, w, gv, kinds, n_f32, tm, name):
    m, d = x.shape
    n_tiles = len(kinds)
    tn = w.shape[1] // n_tiles
    out_specs = [pl.BlockSpec((tm, tn), lambda i, j: (i, jnp.minimum(j, n_f32 - 1)))]
    out_shape = [jax.ShapeDtypeStruct((m, n_f32 * tn), F32)]
    if n_tiles > n_f32:
        out_specs.append(pl.BlockSpec((tm, tn), lambda i, j: (i, jnp.maximum(j - n_f32, 0))))
        out_shape.append(jax.ShapeDtypeStruct((m, (n_tiles - n_f32) * tn), BF16))
    outs = pl.pallas_call(
        functools.partial(_norm_matmul_kernel, kinds=kinds, n_f32=n_f32),
        grid=(m // tm, n_tiles),
        in_specs=[
            pl.BlockSpec((tm, d), lambda i, j: (i, 0)),
            pl.BlockSpec((1, d), lambda i, j: (0, 0)),
            pl.BlockSpec((d, tn), lambda i, j: (0, j)),
            pl.BlockSpec((1, tn), lambda i, j: (0, 0)),
        ],
        out_specs=out_specs,
        out_shape=out_shape,
        scratch_shapes=[pltpu.VMEM((tm, d), BF16)],
        compiler_params=_cparams(("parallel", "arbitrary")),
        name=name,
    )(x, gain.reshape(1, d), w, gv.reshape(1, tn))
    return outs[0], (outs[1] if len(outs) > 1 else None)


def _cmp_copies(pt_ref, src_ref, buf_ref, sem_ref, step, slot, *, nr, n_pages, col0, layer):
    copies = []
    for r in range(nr):
        for p in range(n_pages):
            pid = pt_ref[(step * nr + r) * n_pages + p]
            if layer is None:
                for c in range(2):
                    copies.append(pltpu.make_async_copy(
                        src_ref.at[pid, :, pl.ds(col0 + c * 128, 128)],
                        buf_ref.at[slot, r, c, pl.ds(p * PAGE, PAGE), :],
                        sem_ref.at[slot]))
            else:
                copies.append(pltpu.make_async_copy(
                    src_ref.at[layer, pid, pl.ds(0, 2)],
                    buf_ref.at[slot, r, :, :, :, pl.ds(p * PAGE, PAGE)],
                    sem_ref.at[slot]))
    return copies


def _compress_kernel(pt_ref, src_ref, wc_ref, per_ref, wphi_ref, o_ref, buf_ref, sem_ref, *slab,
                     nr, n_pages, col0, layer):
    step = pl.program_id(0)
    n_steps = pl.num_programs(0)
    slot = lax.rem(step, 2)
    kw = dict(nr=nr, n_pages=n_pages, col0=col0, layer=layer)

    @pl.when(step == 0)
    def _():
        for cp in _cmp_copies(pt_ref, src_ref, buf_ref, sem_ref, step, slot, **kw):
            cp.start()

    @pl.when(step + 1 < n_steps)
    def _():
        for cp in _cmp_copies(pt_ref, src_ref, buf_ref, sem_ref, step + 1, 1 - slot, **kw):
            cp.start()

    for cp in _cmp_copies(pt_ref, src_ref, buf_ref, sem_ref, step, slot, **kw):
        cp.wait()

    rows = n_pages * PAGE
    m = rows // CMP_STRIDE
    if layer is not None:
        slab_ref, = slab
        for r in range(nr):
            for c in range(2):
                for p in range(n_pages):
                    blk = buf_ref[slot, r, c, :, :, pl.ds(p * PAGE, PAGE)].reshape(128, PAGE)
                    slab_ref[r, c, pl.ds(p * PAGE, PAGE), :] = blk.T

    outs = []
    for c in range(2):
        per_req = []
        for r in range(nr):
            if layer is None:
                slabs = [buf_ref[slot, r, c, pl.ds(l, m, stride=CMP_STRIDE), :] for l in range(CMP_STRIDE)]
            else:
                slabs = [slab_ref[r, c, pl.ds(l, m, stride=CMP_STRIDE), :] for l in range(CMP_STRIDE)]
            per_req.append(jnp.concatenate(slabs, axis=1))
        lhs = jnp.concatenate(per_req, axis=0).astype(BF16)
        p = _dot(lhs, wc_ref[c])
        bias = _dot_hi(per_ref[c], wphi_ref[c])[0:1]
        hi = pltpu.roll(p[:, 128:], nr * m - 1, 0)
        outs.append(p[:, :128] + hi + bias)
    res = jnp.concatenate(outs, axis=1)
    o_ref[...] = res.reshape(nr, m, 256)


def _compress(page_table, src, wc, pe_rep, wphi2, *, col0, nr, layer):
    nb, n_pages = page_table.shape
    rows = n_pages * PAGE
    m = rows // CMP_STRIDE
    if layer is None:
        scratch = [pltpu.VMEM((2, nr, 2, rows, 128), F32), pltpu.SemaphoreType.DMA((2,))]
    else:
        scratch = [pltpu.VMEM((2, nr, 2, NSA_KV_HEADS, HEAD_DIM, rows), F32), pltpu.SemaphoreType.DMA((2,)),
                   pltpu.VMEM((nr, 2, rows, 128), F32)]
    grid_spec = pltpu.PrefetchScalarGridSpec(
        num_scalar_prefetch=1,
        grid=(nb // nr,),
        in_specs=[
            pl.BlockSpec(memory_space=pl.ANY),
            pl.BlockSpec((2, CMP_STRIDE * 128, 256), lambda i, pt: (0, 0, 0)),
            pl.BlockSpec((2, 8, L_CMP * HEAD_DIM), lambda i, pt: (0, 0, 0)),
            pl.BlockSpec((2, L_CMP * HEAD_DIM, 128), lambda i, pt: (0, 0, 0)),
        ],
        out_specs=pl.BlockSpec((nr, m, 256), lambda i, pt: (i, 0, 0)),
        scratch_shapes=scratch,
    )
    return pl.pallas_call(
        functools.partial(_compress_kernel, nr=nr, n_pages=n_pages, col0=col0, layer=layer),
        grid_spec=grid_spec,
        out_shape=jax.ShapeDtypeStruct((nb, m, 256), F32),
        compiler_params=_cparams(("arbitrary",)),
        name="compress",
    )(page_table.reshape(-1), src, wc, pe_rep, wphi2)


def _select_blocks_t(imp_t, pos, n_slc):
    ns, c = imp_t.shape
    s_i = lax.broadcasted_iota(jnp.int32, (ns, c), 0)
    cur = lax.shift_right_logical(pos, 6)
    valid = (s_i * L_SEL) <= pos
    forced = (s_i == 0) | (s_i == cur) | (s_i == cur - 1)
    vals = jnp.where(valid, imp_t + jnp.where(forced, FORCE_BONUS, 0.0), -1.0)
    vals = jnp.where(s_i < n_slc, vals, LOWEST)
    rank = jnp.zeros((ns, c), F32)
    for sp in range(n_slc):
        row = vals[sp:sp + 1, :]
        beats = (row > vals) | ((row == vals) & (s_i > sp))
        rank = rank + jnp.where(beats, 1.0, 0.0)
    return jnp.where((rank < min(N_SEL, n_slc)) & (s_i < n_slc), 1.0, 0.0)


def _importance_t(ovl_t, p_sum):
    return functools.reduce(lambda a, b: a + b, [_dot(ovl_t, part) for part in _split3(p_sum)])


def _expand_gates(ng, eg_ref):
    parts = _split3(ng)
    return [functools.reduce(lambda a, b: a + b, [_dot(part, eg_ref[t]) for part in parts]) for t in range(3)]


def _pad_rows(x, rows):
    if x.shape[0] == rows:
        return x
    return jnp.concatenate([x, jnp.zeros((rows - x.shape[0],) + x.shape[1:], x.dtype)], axis=0)


def _nsa_prompt_kernel(q_ref, pg_ref, wn_ref, kvc_ref, bct_ref, toep_ref, ovl_ref, ef_ref, eg_ref,
                       o_ref, ks_ref, vst_ref, kw_ref, vwt_ref, mft_ref, *, n_slc):
    qt = pl.program_id(1)
    t_len = pg_ref.shape[1]
    nj = kvc_ref.shape[1]

    @pl.when(qt == 0)
    def _():
        ks_ref[...] = pg_ref[0, :, 256:384].astype(BF16)
        kw_ref[...] = wn_ref[0, :, 0:128].astype(BF16)
        for blk in range(t_len // TQ):
            rs = slice(blk * TQ, (blk + 1) * TQ)
            vst_ref[:, rs] = pg_ref[0, rs, 384:512].T.astype(BF16)
            vwt_ref[:, rs] = wn_ref[0, rs, 128:256].T.astype(BF16)

    qts = [q_ref[0, :, h * 128:(h + 1) * 128].astype(F32).T.astype(BF16) for h in range(NSA_HEADS)]
    groups = [h // NSA_REP for h in range(NSA_HEADS)]

    def score_dots(k_ref, kts):
        starts = [pl.multiple_of(kt * TQ, TQ) for kt in kts]
        ks = [k_ref[pl.ds(st, TQ), :] for st in starts]
        return starts, [[_dot(k, qts[h]) for k in ks] for h in range(NSA_HEADS)]

    all_heads = tuple(range(NSA_HEADS))

    def softmax_update(raw, starts, tbls, masked, carry, heads=all_heads):
        stats, probs = [], []
        for h in heads:
            g = groups[h]
            m_i, l_i, _ = carry[h]
            ss = []
            for s, st, tbl in zip(raw[h], starts, tbls):
                s = s + toep_ref[tbl, h]
                if masked:
                    s = s + mft_ref[pl.ds(st, TQ), g * TQ:(g + 1) * TQ]
                ss.append(s)
            mx = functools.reduce(jnp.maximum, [jnp.max(s, axis=0, keepdims=True) for s in ss])
            m_new = jnp.maximum(m_i, mx)
            alpha = jnp.exp2(m_i - m_new)
            ps = [jnp.exp2(s - m_new) for s in ss]
            l_new = alpha * l_i + functools.reduce(
                lambda a, b: a + b, [jnp.sum(p, axis=0, keepdims=True) for p in ps])
            stats.append((m_new, l_new, alpha))
            probs.append(jnp.concatenate(ps, axis=0).astype(BF16))
        return stats, probs

    def value_dots(vt_ref, starts, stats, probs, carry, heads=all_heads):
        vts = [vt_ref[:, pl.ds(st, TQ)] for st in starts]
        new = []
        for i, h in enumerate(heads):
            g = groups[h]
            m_new, l_new, alpha = stats[i]
            vt = jnp.concatenate([v[g * HEAD_DIM:(g + 1) * HEAD_DIM, :] for v in vts], axis=1)
            new.append((m_new, l_new, alpha * carry[h][2] + _dot(vt, probs[i])))
        return tuple(new)

    init = tuple((jnp.full((1, TQ), NEG_INF, F32), jnp.zeros((1, TQ), F32),
                  jnp.zeros((HEAD_DIM, TQ), F32)) for _ in range(NSA_HEADS))

    def finish(state):
        return [acc * (1.0 / l_f) for (_, l_f, acc) in state]

    w_tiles = WINDOW // TQ + 1
    w_kts = [qt - (w_tiles - 1) + u for u in range(w_tiles)]
    w_tbls = [jnp.where(kt < 0, 4, tb) for kt, tb in zip(w_kts, (3, 1, 0))]
    w_starts, raw_w = score_dots(kw_ref, [jnp.maximum(kt, 0) for kt in w_kts])
    kc = kvc_ref[0, :, 0:128].astype(BF16)
    vct = kvc_ref[0, :, 128:256].T.astype(BF16)
    raw_c = [_dot(kc, qts[h]) for h in range(NSA_HEADS)]

    stats_w, probs_w = softmax_update(raw_w, w_starts, w_tbls, False, init)
    p_c, p_sum = [], [None] * NSA_KV_HEADS
    for h, g in enumerate(groups):
        s = raw_c[h] + bct_ref[h]
        mx = jnp.max(s, axis=0, keepdims=True)
        e = jnp.exp2(s - mx)
        p = jnp.where(mx > 0.5 * NEG_INF, e * (1.0 / jnp.sum(e, axis=0, keepdims=True)), 0.0)
        p_c.append(p.astype(BF16))
        p_sum[g] = p if p_sum[g] is None else p_sum[g] + p

    o_w = finish(value_dots(vwt_ref, w_starts, stats_w, probs_w, init))
    o_c = [_dot(vct[g * HEAD_DIM:(g + 1) * HEAD_DIM, :], p_c[h]) for h, g in enumerate(groups)]
    imp_t = _importance_t(ovl_ref[...], jnp.concatenate(p_sum, axis=1))
    ns = -(-n_slc // 8) * 8
    pos = qt * TQ + lax.rem(lax.broadcasted_iota(jnp.int32, (1, NSA_KV_HEADS * TQ), 1), TQ)
    sel_t = _select_blocks_t(imp_t[:ns], pos, n_slc)
    hidden = ((_pad_rows(sel_t, 128) - 1.0) * (-NEG_INF)).astype(BF16)
    mft_ref[...] = _dot(ef_ref[...], hidden)

    def sel_body(it, carry):
        kts = [it * KEY_UNROLL + u for u in range(KEY_UNROLL)]
        tbls = [jnp.where(qt - kt < 0, 4, jnp.minimum(qt - kt, 2)) for kt in kts]
        starts, raw = score_dots(ks_ref, kts)
        stats, probs = softmax_update(raw, starts, tbls, True, carry)
        return value_dots(vst_ref, starts, stats, probs, carry)

    o_s = finish(lax.fori_loop(0, (qt + KEY_UNROLL) // KEY_UNROLL, sel_body, init))

    start = pl.multiple_of(qt * TQ, TQ)
    gates = _expand_gates(wn_ref[0, pl.ds(start, TQ), 256:384], eg_ref)
    tiles = []
    for mt in range(NSA_HEADS // 2):
        acc = jnp.zeros((TQ, 128), F32)
        for o, gate in zip((o_c, o_s, o_w), gates):
            pair = jnp.concatenate([o[2 * mt], o[2 * mt + 1]], axis=0)
            acc = acc + gate[:, mt * 128:(mt + 1) * 128] * pair.T
        tiles.append(acc)
    o_ref[0] = jnp.concatenate(tiles, axis=1)


def _nsa_prompt(z32, z16, kvc, bias_ct, toep_t, ovl_t, ef_t, eg):
    b, t, _ = z32.shape
    nj = kvc.shape[1]
    n_slc = -(-t // L_SEL)
    return pl.pallas_call(
        functools.partial(_nsa_prompt_kernel, n_slc=n_slc),
        grid=(b, t // TQ),
        in_specs=[
            pl.BlockSpec((1, TQ, 1024), lambda i, j: (i, j, CB16_Q)),
            pl.BlockSpec((1, t, TN), lambda i, j: (i, 0, CB32_PAGED)),
            pl.BlockSpec((1, t, TN), lambda i, j: (i, 0, CB32_WN)),
            pl.BlockSpec((1, nj, 256), lambda i, j: (i, 0, 0)),
            pl.BlockSpec((NSA_HEADS, nj, TQ), lambda i, j: (0, 0, j)),
            pl.BlockSpec((5, NSA_HEADS, TQ, TQ), lambda i, j: (0, 0, 0, 0)),
            pl.BlockSpec((128, nj), lambda i, j: (0, 0)),
            pl.BlockSpec((t, 128), lambda i, j: (0, 0)),
            pl.BlockSpec((3, 128, 512), lambda i, j: (0, 0, 0)),
        ],
        out_specs=pl.BlockSpec((1, TQ, 512), lambda i, j: (i, j, 0)),
        out_shape=jax.ShapeDtypeStruct((b, t, 512), F32),
        scratch_shapes=[pltpu.VMEM((t, 128), BF16), pltpu.VMEM((128, t), BF16),
                        pltpu.VMEM((t, 128), BF16), pltpu.VMEM((128, t), BF16),
                        pltpu.VMEM((t, NSA_KV_HEADS * TQ), F32)],
        compiler_params=_cparams(("parallel", "arbitrary")),
        name="nsa_prompt",
    )(z16, z32, z32, kvc, bias_ct, toep_t, ovl_t, ef_t, eg)


def _slc_copies(pt_ref, src_ref, buf_ref, sem_ref, step, slot, *, nr, n_pages, layer):
    copies = []
    for r in range(nr):
        for p in range(n_pages):
            pid = pt_ref[(step * nr + r) * n_pages + p]
            copies.append(pltpu.make_async_copy(
                src_ref.at[layer, pid, pl.ds(2, 2)],
                buf_ref.at[slot, r, :, :, :, pl.ds(p * PAGE, PAGE)],
                sem_ref.at[slot]))
    return copies


def _nsa_sample_kernel(pt_ref, src_ref, q_ref, pg_ref, wn_ref, cw_ref, kvc_ref, bs_ref, bn_ref, bw_ref,
                       bcs_ref, ovl_ref, ef_ref, eg_ref, o_ref, buf_ref, sem_ref,
                       *, nr, n_pages, n_slc, pos0, layer):
    step = pl.program_id(0)
    n_steps = pl.num_programs(0)
    slot = lax.rem(step, 2)
    tq = q_ref.shape[1]
    rows = NSA_HEADS * tq
    nj = kvc_ref.shape[1]
    past = n_pages * PAGE
    wb = cw_ref.shape[5]
    reqs = range(nr)
    kw = dict(nr=nr, n_pages=n_pages, layer=layer)

    @pl.when(step == 0)
    def _():
        for cp in _slc_copies(pt_ref, src_ref, buf_ref, sem_ref, step, slot, **kw):
            cp.start()

    @pl.when(step + 1 < n_steps)
    def _():
        for cp in _slc_copies(pt_ref, src_ref, buf_ref, sem_ref, step + 1, 1 - slot, **kw):
            cp.start()

    for cp in _slc_copies(pt_ref, src_ref, buf_ref, sem_ref, step, slot, **kw):
        cp.wait()

    def new_rows(x):
        return _pad_rows(x, 128).astype(BF16)

    def row_max(*xs):
        return functools.reduce(jnp.maximum, [jnp.max(x, axis=-1, keepdims=True) for x in xs])

    def row_sum(*xs):
        return functools.reduce(lambda a, b: a + b, [jnp.sum(x, axis=-1, keepdims=True) for x in xs])

    qs, pgs, wns, raw = [], [], [], []
    for r in reqs:
        qf = q_ref[r].astype(F32)
        q = jnp.concatenate([qf[:, h * 128:(h + 1) * 128] for h in range(NSA_HEADS)], axis=0).astype(BF16)
        pg, wn = pg_ref[r], wn_ref[r]
        qs.append(q)
        pgs.append(pg)
        wns.append(wn)
        raw.append(dict(
            c=_dot_nt(q, kvc_ref[r, :, 0:128].astype(BF16)),
            w_old=_dot(q, cw_ref[0, r, 0].reshape(128, wb).astype(BF16)),
            w_new=_dot_nt(q, new_rows(wn[:, 0:128])),
            s_old=_dot(q, buf_ref[slot, r, 0].reshape(128, past).astype(BF16)),
            s_new=_dot_nt(q, new_rows(pg[:, 256:384]))))

    p_cs, win = [], []
    for r in reqs:
        s_c = raw[r]["c"] + bcs_ref[...]
        mx = row_max(s_c)
        e_c = jnp.exp2(s_c - mx)
        p_cs.append(jnp.where(mx > 0.5 * NEG_INF, e_c * (1.0 / row_sum(e_c)), 0.0))
        s_o, s_n = raw[r]["w_old"] + bw_ref[...], raw[r]["w_new"] + bn_ref[...]
        m2 = row_max(s_o, s_n)
        e_o, e_n = jnp.exp2(s_o - m2), jnp.exp2(s_n - m2)
        win.append((e_o.astype(BF16), e_n.astype(BF16), 1.0 / row_sum(e_o, e_n)))

    o_cs, o_ws, imps, gates = [], [], [], []
    for r in reqs:
        o_cs.append(_dot(p_cs[r].astype(BF16), kvc_ref[r, :, 128:256].astype(BF16)))
        e_o, e_n, inv = win[r]
        o_ws.append((_dot_nt(e_o, cw_ref[0, r, 1].reshape(128, wb).astype(BF16))
                     + _dot(e_n, new_rows(wns[r][:, 128:256]))) * inv)
        p_g = jnp.sum(p_cs[r].reshape(NSA_KV_HEADS, NSA_REP, tq, nj), axis=1).reshape(NSA_KV_HEADS * tq, nj)
        imps.append(functools.reduce(lambda a, b: a + b,
                                     [_dot_nt(ovl_ref[...], part) for part in _split3(_pad_rows(p_g, 128))]))
        gates.append(_expand_gates(wns[r][:, 256:384], eg_ref))

    ns = -(-n_slc // 8) * 8
    lane = lax.broadcasted_iota(jnp.int32, (1, 128), 1)
    mfs = []
    for r in reqs:
        sel_t = _select_blocks_t(imps[r][:ns], pos0 + lax.rem(lane, tq), n_slc)
        sel = _pad_rows(sel_t, 128).T[:NSA_KV_HEADS * tq]
        mfs.append(_dot(sel.astype(BF16), ef_ref[...]))

    sel_probs = []
    for r in reqs:
        mf = jnp.broadcast_to(mfs[r].reshape(NSA_KV_HEADS, 1, tq, past),
                              (NSA_KV_HEADS, NSA_REP, tq, past)).reshape(rows, past)
        s_o = raw[r]["s_old"] + jnp.where(mf > 0.5, bs_ref[...], NEG_INF)
        s_n = raw[r]["s_new"] + bn_ref[...]
        m2 = row_max(s_o, s_n)
        e_o, e_n = jnp.exp2(s_o - m2), jnp.exp2(s_n - m2)
        sel_probs.append((e_o.astype(BF16), e_n.astype(BF16), 1.0 / row_sum(e_o, e_n)))
    o_ss = []
    for r in reqs:
        e_o, e_n, inv = sel_probs[r]
        o_ss.append((_dot_nt(e_o, buf_ref[slot, r, 1].reshape(128, past).astype(BF16))
                     + _dot(e_n, new_rows(pgs[r][:, 384:512]))) * inv)

    lane128 = lax.broadcasted_iota(jnp.int32, (tq, 128), 1)
    for r in reqs:
        tiles = []
        for mt in range(NSA_HEADS // 2):
            g = (2 * mt) // NSA_REP
            acc = jnp.zeros((tq, 128), F32)
            for o, gate in zip((o_cs[r], o_ss[r], o_ws[r]), gates[r]):
                a = o[(2 * mt) * tq:(2 * mt + 1) * tq]
                b = o[(2 * mt + 1) * tq:(2 * mt + 2) * tq]
                if g == 1:
                    a = pltpu.roll(a, 64, 1)
                else:
                    b = pltpu.roll(b, 64, 1)
                acc = acc + gate[:, mt * 128:(mt + 1) * 128] * jnp.where(lane128 < 64, a, b)
            tiles.append(acc)
        o_ref[r] = jnp.concatenate(tiles, axis=1)


def _nsa_sample(page_table, cache_t, cwin_t, layer, z32, z16, kvc, bias_s, bias_n, bias_w, bias_cs, ovl_t, ef, eg,
                nr):
    nb, n_pages = page_table.shape
    past = n_pages * PAGE
    tq = z32.shape[1]
    rows = NSA_HEADS * tq
    nj = kvc.shape[1]
    n_slc = -(-(past + tq) // L_SEL)
    wb = cwin_t.shape[5]
    const2 = lambda i, pt: (0, 0)
    grid_spec = pltpu.PrefetchScalarGridSpec(
        num_scalar_prefetch=1,
        grid=(nb // nr,),
        in_specs=[
            pl.BlockSpec(memory_space=pl.ANY),
            pl.BlockSpec((nr, tq, 1024), lambda i, pt: (i, 0, CB16_Q)),
            pl.BlockSpec((nr, tq, TN), lambda i, pt: (i, 0, CB32_PAGED)),
            pl.BlockSpec((nr, tq, TN), lambda i, pt: (i, 0, CB32_WN)),
            pl.BlockSpec((1, nr, 2, NSA_KV_HEADS, HEAD_DIM, wb), lambda i, pt: (layer, i, 0, 0, 0, 0)),
            pl.BlockSpec((nr, nj, 256), lambda i, pt: (i, 0, 0)),
            pl.BlockSpec((rows, past), const2),
            pl.BlockSpec((rows, 128), const2),
            pl.BlockSpec((rows, wb), const2),
            pl.BlockSpec((rows, nj), const2),
            pl.BlockSpec((128, nj), const2),
            pl.BlockSpec((128, past), const2),
            pl.BlockSpec((3, 128, 512), lambda i, pt: (0, 0, 0)),
        ],
        out_specs=pl.BlockSpec((nr, tq, 512), lambda i, pt: (i, 0, 0)),
        scratch_shapes=[pltpu.VMEM((2, nr, 2, NSA_KV_HEADS, HEAD_DIM, past), F32),
                        pltpu.SemaphoreType.DMA((2,))],
    )
    return pl.pallas_call(
        functools.partial(_nsa_sample_kernel, nr=nr, n_pages=n_pages, n_slc=n_slc, pos0=past, layer=layer),
        grid_spec=grid_spec,
        out_shape=jax.ShapeDtypeStruct((nb, tq, 512), F32),
        compiler_params=_cparams(("arbitrary",)),
        name="nsa_sample",
    )(page_table.reshape(-1), cache_t, z16, z32, z32, cwin_t, kvc, bias_s, bias_n, bias_w, bias_cs,
      ovl_t, ef, eg)


def _mem_attn_kernel(q_ref, kv_ref, o_ref):
    q = q_ref[0]
    heads = range(MEM_HEADS)
    scores = [_dot_nt(q[:, h * 128:(h + 1) * 128], kv_ref[0, :, h * 128:(h + 1) * 128].astype(BF16))
              for h in heads]
    probs = []
    for s in scores:
        s = s * (MEM_HEAD_DIM ** -0.5)
        e = jnp.exp(s - jnp.max(s, axis=-1, keepdims=True))
        probs.append((e * (1.0 / jnp.sum(e, axis=-1, keepdims=True))).astype(BF16))
    o_ref[0] = jnp.concatenate(
        [_dot(probs[h], kv_ref[0, :, MEM_WIDTH + h * 128:MEM_WIDTH + (h + 1) * 128].astype(BF16))
         for h in heads], axis=1)


def _mem_attn(z16, mem_kv, tq):
    b, t, _ = z16.shape
    n_mem = mem_kv.shape[1]
    return pl.pallas_call(
        _mem_attn_kernel,
        grid=(b, t // tq),
        in_specs=[
            pl.BlockSpec((1, tq, TN), lambda i, j: (i, j, CB16_MQ)),
            pl.BlockSpec((1, n_mem, 2 * MEM_WIDTH), lambda i, j: (i, 0, 0)),
        ],
        out_specs=pl.BlockSpec((1, tq, MEM_WIDTH), lambda i, j: (i, j, 0)),
        out_shape=jax.ShapeDtypeStruct((b, t, MEM_WIDTH), F32),
        compiler_params=_cparams(("parallel", "parallel")),
        name="mem_attn",
    )(z16, mem_kv)


def _mem_attn_cache_kernel(q_ref, kv_ref, o_ref):
    nr, tq, _ = q_ref.shape
    n_mem = kv_ref.shape[2] // (2 * MEM_HEADS)
    pairs = [(r, h) for r in range(nr) for h in range(MEM_HEADS)]
    qs = [q_ref[r].astype(F32) for r in range(nr)]
    scores = []
    for r, h in pairs:
        kh = kv_ref[0, r, pl.ds(h, n_mem, stride=2 * MEM_HEADS), :].astype(BF16)
        scores.append(_dot_nt(qs[r][:, h * 128:(h + 1) * 128].astype(BF16), kh))
    s = jnp.concatenate(scores, axis=0) * (MEM_HEAD_DIM ** -0.5)
    mx = jnp.max(s, axis=-1, keepdims=True)
    e = jnp.exp(s - mx)
    p = e * (1.0 / jnp.sum(e, axis=-1, keepdims=True))
    outs = []
    for i, (r, h) in enumerate(pairs):
        vh = kv_ref[0, r, pl.ds(MEM_HEADS + h, n_mem, stride=2 * MEM_HEADS), :].astype(BF16)
        outs.append(_dot(p[i * tq:(i + 1) * tq].astype(BF16), vh))
    for r in range(nr):
        o_ref[r] = jnp.concatenate(outs[r * MEM_HEADS:(r + 1) * MEM_HEADS], axis=1)


def _mem_attn_cache(z16, cache_mem, layer, nr):
    b, tq, _ = z16.shape
    nl, _, n_mem = cache_mem.shape[:3]
    rows = n_mem * 2 * MEM_HEADS
    return pl.pallas_call(
        _mem_attn_cache_kernel,
        grid=(b // nr,),
        in_specs=[
            pl.BlockSpec((nr, tq, TN), lambda i: (i, 0, CB16_MQ)),
            pl.BlockSpec((1, nr, rows, MEM_HEAD_DIM), lambda i: (layer, i, 0, 0)),
        ],
        out_specs=pl.BlockSpec((nr, tq, MEM_WIDTH), lambda i: (i, 0, 0)),
        out_shape=jax.ShapeDtypeStruct((b, tq, MEM_WIDTH), F32),
        compiler_params=_cparams(("parallel",)),
        name="mem_attn_cache",
    )(z16, cache_mem.reshape(nl, b, rows, MEM_HEAD_DIM))


def _merge_kernel(x_ref, u_ref, v_ref, g0_ref, g1_ref, g2_ref, on_ref, om_ref, wsp_ref, bsp_ref,
                  wa_ref, wb_ref, wc_ref, wo_ref, o_ref):
    tm = x_ref.shape[0]
    u = u_ref[...].astype(F32)
    v = v_ref[...].astype(BF16)
    chunks = []
    for c in range(tm // CHUNK):
        parts = []
        for g in range(A_GROUPS):
            vg = v[c * CHUNK:(c + 1) * CHUNK, g * 128:(g + 1) * 128]
            parts.append(_dot(wsp_ref[g], vg) + bsp_ref[g])
        chunks.append(jnp.concatenate(parts, axis=1))
    ya = (u * jnp.concatenate(chunks, axis=0)).astype(BF16)
    merged = (g0_ref[...].astype(F32) * _dot(ya, wa_ref[...])
              + g1_ref[...].astype(F32) * _dot(on_ref[...].astype(BF16), wb_ref[...])
              + g2_ref[...].astype(F32) * _dot(om_ref[...].astype(BF16), wc_ref[...]))
    o_ref[...] = x_ref[...] + _dot(merged.astype(BF16), wo_ref[...])


def _merge(x, z32, z16, o_nsa, o_mem, wsp, bsp, w_a, w_b, w_c, w_out, tm):
    m, d = x.shape
    row = lambda i: (i, 0)
    c2 = lambda i: (0, 0)
    c3 = lambda i: (0, 0, 0)
    return pl.pallas_call(
        _merge_kernel,
        grid=(m // tm,),
        in_specs=[
            pl.BlockSpec((tm, d), row),
            pl.BlockSpec((tm, TN), lambda i: (i, CB16_U)),
            pl.BlockSpec((tm, TN), lambda i: (i, CB32_V)),
            pl.BlockSpec((tm, d), lambda i: (i, CB16_BG)),
            pl.BlockSpec((tm, d), lambda i: (i, CB16_BG + 1)),
            pl.BlockSpec((tm, d), lambda i: (i, CB16_BG + 2)),
            pl.BlockSpec((tm, 512), row),
            pl.BlockSpec((tm, 512), row),
            pl.BlockSpec((A_GROUPS, CHUNK, CHUNK), c3),
            pl.BlockSpec((A_GROUPS, CHUNK, 128), c3),
            pl.BlockSpec((A_WIDTH, d), c2),
            pl.BlockSpec((512, d), c2),
            pl.BlockSpec((MEM_WIDTH, d), c2),
            pl.BlockSpec((d, d), c2),
        ],
        out_specs=pl.BlockSpec((tm, d), row),
        out_shape=jax.ShapeDtypeStruct((m, d), F32),
        compiler_params=_cparams(("parallel",)),
        name="merge",
    )(x, z16, z32, z16, z16, z16, o_nsa, o_mem, wsp, bsp, w_a, w_b, w_c, w_out)


def _mlp_kernel(x_ref, g_ref, wu_ref, wd_ref, o_ref, hn_ref, acc_ref):
    f = pl.program_id(1)

    @pl.when(f == 0)
    def _():
        hn_ref[...] = _rms(x_ref[...], g_ref[...]).astype(BF16)
        acc_ref[...] = jnp.zeros_like(acc_ref)

    h = jnp.square(jnp.maximum(_dot(hn_ref[...], wu_ref[...]), 0.0))
    acc_ref[...] += _dot(h.astype(BF16), wd_ref[...])

    @pl.when(f == pl.num_programs(1) - 1)
    def _():
        o_ref[...] = x_ref[...] + acc_ref[...]


def _mlp(x, gain, w_up, w_down, tm, tf):
    m, d = x.shape
    ff = w_up.shape[1]
    return pl.pallas_call(
        _mlp_kernel,
        grid=(m // tm, ff // tf),
        in_specs=[
            pl.BlockSpec((tm, d), lambda i, j: (i, 0)),
            pl.BlockSpec((1, d), lambda i, j: (0, 0)),
            pl.BlockSpec((d, tf), lambda i, j: (0, j)),
            pl.BlockSpec((tf, d), lambda i, j: (j, 0)),
        ],
        out_specs=pl.BlockSpec((tm, d), lambda i, j: (i, 0)),
        out_shape=jax.ShapeDtypeStruct((m, d), F32),
        scratch_shapes=[pltpu.VMEM((tm, d), BF16), pltpu.VMEM((tm, d), F32)],
        compiler_params=_cparams(("parallel", "arbitrary")),
        name="mlp",
    )(x, gain.reshape(1, d), w_up, w_down)


def _final_norm_kernel(x_ref, g_ref, o_ref):
    o_ref[...] = _rms(x_ref[...], g_ref[...])


def _final_norm(x, gain, tm):
    m, d = x.shape
    return pl.pallas_call(
        _final_norm_kernel,
        grid=(m // tm,),
        in_specs=[pl.BlockSpec((tm, d), lambda i: (i, 0)), pl.BlockSpec((1, d), lambda i: (0, 0))],
        out_specs=pl.BlockSpec((tm, d), lambda i: (i, 0)),
        out_shape=jax.ShapeDtypeStruct((m, d), F32),
        compiler_params=_cparams(("parallel",)),
        name="final_norm",
    )(x, gain.reshape(1, d))


def _relayout_w_in(w_in):
    nl, d, _ = w_in.shape
    wq = w_in[:, :, OFF_Q:OFF_KV].reshape(nl, d, NSA_HEADS, 1, HEAD_DIM) * ((HEAD_DIM ** -0.5) * LOG2E)
    grp = (np.arange(NSA_HEADS) // NSA_REP)[:, None] == np.arange(NSA_KV_HEADS)[None, :]
    wq = jnp.where(jnp.asarray(grp)[None, None, :, :, None], wq, 0.0).reshape(nl, d, NSA_HEADS * 128)
    pad = jnp.zeros((nl, d, TN - 256 - (OFF_MQ - OFF_NG)), w_in.dtype)
    half_a = A_WIDTH
    parts = [w_in[:, :, half_a:OFF_Q], w_in[:, :, OFF_KV:OFF_KV + 512], w_in[:, :, OFF_KV + 512:OFF_MQ], pad,
             w_in[:, :, OFF_BG:], wq, w_in[:, :, :half_a], w_in[:, :, OFF_MQ:OFF_BG]]
    out = jnp.concatenate(parts, axis=2).astype(BF16)
    assert out.shape[2] == Z32_COLS + Z16_COLS
    return out


def _compress_weights(w_phi, pe_cmp):
    nl = w_phi.shape[0]
    w5 = w_phi.reshape(nl, 2, CMP_STRIDE, 2, HEAD_DIM, HEAD_DIM)
    eye = jnp.eye(NSA_KV_HEADS, dtype=w_phi.dtype)
    wc = jnp.einsum("nhlcde,gk->nclgdhke", w5, eye).reshape(nl, 2, CMP_STRIDE * 128, 256).astype(BF16)
    wphi2 = jnp.transpose(w_phi, (0, 2, 1, 3, 4)).reshape(nl, 2, L_CMP * HEAD_DIM, HEAD_DIM)
    wphi2 = jnp.concatenate([wphi2, wphi2], axis=-1)
    pe2 = jnp.transpose(pe_cmp, (0, 2, 1, 3)).reshape(nl, 2, 1, L_CMP * HEAD_DIM)
    pe_rep = jnp.broadcast_to(pe2, (nl, 2, 8, L_CMP * HEAD_DIM))
    return wc, pe_rep, wphi2


def _overlap_t(nj, n_cmp, n_slc):
    ss = np.arange(128)[:, None]
    jj = np.arange(nj)[None, :]
    ov = ((jj * CMP_STRIDE <= (ss + 1) * L_SEL - 1) & (jj * CMP_STRIDE + L_CMP - 1 >= ss * L_SEL)
          & (jj < n_cmp) & (ss < n_slc))
    return jnp.asarray(ov.astype(np.float32)).astype(BF16)


def _block_expand(n_keys):
    e = np.arange(128)[:, None] == (np.arange(n_keys)[None, :] // L_SEL)
    return jnp.asarray(e.astype(np.float32)).astype(BF16)


def _gate_expand():
    e = np.zeros((3, 128, 512), np.float32)
    for t in range(3):
        for h in range(NSA_HEADS):
            e[t, 3 * h + t, h * HEAD_DIM:(h + 1) * HEAD_DIM] = 1.0
    return jnp.asarray(e).astype(BF16)


def _pick_tile(m, pref):
    t = min(m, pref)
    while m % t:
        t //= 2
    return t


def kernel(x_prompt, x_sample, cache_kv, cache_win_kv, cache_mem_kv, page_table, mem_prompt,
           ln1, w_in, g_v, w_s, b_s, w_a, pe_cmp, w_phi, w_b, ln_mem, w_mem_kv, w_c, w_out,
           ln2, w_up, w_down, rel_bias, ln_f):
    bp, t, d = x_prompt.shape
    bs, ts, _ = x_sample.shape
    n_pages = page_table.shape[1]
    past = n_pages * PAGE
    n_mem = mem_prompt.shape[1]
    wb = cache_win_kv.shape[2]
    assert t % (KEY_UNROLL * TQ) == 0 and t % CHUNK == 0 and past % L_SEL == 0 and ts <= L_SEL and CHUNK % ts == 0
    assert (past + ts - L_CMP) // CMP_STRIDE + 1 <= past // CMP_STRIDE
    assert wb == WINDOW and t >= WINDOW

    w_in_p = _relayout_w_in(w_in)
    wc_all, pe_rep_all, wphi2_all = _compress_weights(w_phi, pe_cmp)
    w_a16, w_b16, w_c16, w_out16 = (w.astype(BF16) for w in (w_a, w_b, w_c, w_out))
    w_up16, w_down16, w_mem16 = w_up.astype(BF16), w_down.astype(BF16), w_mem_kv.astype(BF16)
    causal = np.tril(np.ones((CHUNK, CHUNK), bool))
    wsp_p = jnp.where(jnp.asarray(causal), w_s, 0.0)
    blockdiag = np.kron(np.eye(CHUNK // ts), np.ones((ts, ts))) > 0
    wsp_s = jnp.where(jnp.asarray(blockdiag & causal),
                      jnp.tile(w_s[:, :, :ts, :ts], (1, 1, CHUNK // ts, CHUNK // ts)), 0.0)
    bsp_p = jnp.broadcast_to(b_s[:, :, :, None], b_s.shape + (128,))
    bsp_s = jnp.broadcast_to(jnp.tile(b_s[:, :, :ts], (1, 1, CHUNK // ts))[:, :, :, None], b_s.shape + (128,))
    wsp_p, wsp_s = wsp_p.astype(BF16), wsp_s.astype(BF16)

    cache_t = jnp.transpose(cache_kv, (0, 1, 3, 4, 5, 2))
    cwin_t = jnp.transpose(cache_win_kv, (0, 1, 3, 4, 5, 2))

    toep_t, bias_ct, bias_s, bias_n, bias_w, bias_cs = _tables(rel_bias, t, past, ts, wb)
    nj_p, nj_s = t // CMP_STRIDE, past // CMP_STRIDE
    ovl_p = _overlap_t(nj_p, (t - L_CMP) // CMP_STRIDE + 1, -(-t // L_SEL))
    ovl_s = _overlap_t(nj_s, (past + ts - L_CMP) // CMP_STRIDE + 1, -(-(past + ts) // L_SEL))
    ef_p_t = _block_expand(t).T
    ef_s = _block_expand(past)
    eg = _gate_expand()
    pt_prompt = jnp.arange(bp * (t // PAGE), dtype=jnp.int32).reshape(bp, t // PAGE)

    np_tok, ns_tok = bp * t, bs * ts
    tm_p = _pick_tile(np_tok, 1024)
    tm_in = _pick_tile(np_tok, 2048)
    tm_s = _pick_tile(ns_tok, 1024)
    xp = x_prompt.reshape(np_tok, d)
    xs = x_sample.reshape(ns_tok, d)
    mem_flat = mem_prompt.reshape(bp * n_mem, d)
    zeros_gv = jnp.zeros((TN,), F32)

    kv_p, win_p, mem_p, kv_s, win_s, v_s = [], [], [], [], [], []
    for l in range(DEPTH):
        mem_kv, _ = _norm_matmul(mem_flat, ln_mem[l], w_mem16[l], zeros_gv, ("none", "none"), 2,
                                 _pick_tile(bp * n_mem, 1024), "mem_kv_proj")
        zp32, zp16 = _norm_matmul(xp, ln1[l], w_in_p[l], g_v[l], TILE_KINDS, N_F32_TILES, tm_in, "in_proj")
        zp32_3, zp16_3 = zp32.reshape(bp, t, Z32_COLS), zp16.reshape(bp, t, Z16_COLS)
        kvc_p = _compress(pt_prompt, zp32.reshape(np_tok // PAGE, PAGE, Z32_COLS), wc_all[l], pe_rep_all[l],
                          wphi2_all[l], col0=COL_PAGED, nr=_pick_tile(bp, 4), layer=None)
        o_nsa = _nsa_prompt(zp32_3, zp16_3, kvc_p, bias_ct, toep_t, ovl_p, ef_p_t, eg)
        o_mem = _mem_attn(zp16_3, mem_kv.reshape(bp, n_mem, 2 * MEM_WIDTH), _pick_tile(t, 512))
        x1 = _merge(xp, zp32, zp16, o_nsa.reshape(np_tok, 512), o_mem.reshape(np_tok, MEM_WIDTH), wsp_p[l],
                    bsp_p[l], w_a16[l], w_b16[l], w_c16[l], w_out16[l], _pick_tile(np_tok, 512))
        xp = _mlp(x1, ln2[l], w_up16[l], w_down16[l], tm_p, 1024)
        kv_p.append(zp32_3[:, :, COL_PAGED:COL_PAGED + 512].reshape(bp, t, 4, NSA_KV_HEADS, HEAD_DIM))
        win_p.append(zp32_3[:, t - WINDOW:, COL_WN:COL_WN + 256].reshape(bp, WINDOW, 2, NSA_KV_HEADS, HEAD_DIM))
        mem_p.append(mem_kv.reshape(bp, n_mem, 2, MEM_HEADS, MEM_HEAD_DIM))

        zs32, zs16 = _norm_matmul(xs, ln1[l], w_in_p[l], g_v[l], TILE_KINDS, N_F32_TILES, tm_s, "in_proj")
        zs32_3, zs16_3 = zs32.reshape(bs, ts, Z32_COLS), zs16.reshape(bs, ts, Z16_COLS)
        kvc_s = _compress(page_table, cache_t, wc_all[l], pe_rep_all[l], wphi2_all[l], col0=0,
                          nr=_pick_tile(bs, 4), layer=l)
        o_nsa = _nsa_sample(page_table, cache_t, cwin_t, l, zs32_3, zs16_3, kvc_s, bias_s, bias_n, bias_w,
                            bias_cs, ovl_s, ef_s, eg, _pick_tile(bs, 4))
        o_mem = _mem_attn_cache(zs16_3, cache_mem_kv, l, _pick_tile(bs, 8))
        x1 = _merge(xs, zs32, zs16, o_nsa.reshape(ns_tok, 512), o_mem.reshape(ns_tok, MEM_WIDTH), wsp_s[l],
                    bsp_s[l], w_a16[l], w_b16[l], w_c16[l], w_out16[l], _pick_tile(ns_tok, 256))
        xs = _mlp(x1, ln2[l], w_up16[l], w_down16[l], tm_s, 1024)
        kv_s.append(zs32_3[:, :, COL_PAGED:COL_PAGED + 512].reshape(bs, ts, 4, NSA_KV_HEADS, HEAD_DIM))
        win_new = zs32_3[:, :, COL_WN:COL_WN + 256].reshape(bs, ts, 2, NSA_KV_HEADS, HEAD_DIM)
        win_s.append(jnp.concatenate([cache_win_kv[l][:, ts:], win_new], axis=1))
        v_s.append(zs32_3[:, :, CB32_V * TN:(CB32_V + 1) * TN])

    y_prompt = _final_norm(xp, ln_f, tm_p).reshape(bp, t, d)
    y_sample = _final_norm(xs, ln_f, tm_s).reshape(bs, ts, d)
    return (y_prompt, y_sample, jnp.stack(kv_p), jnp.stack(win_p), jnp.stack(mem_p),
            jnp.stack(kv_s), jnp.stack(win_s), jnp.stack(v_s))
```

```python
import functools
import math

import numpy as np
import jax
import jax.numpy as jnp
from jax import lax
from jax.experimental import pallas as pl
from jax.experimental.pallas import tpu as pltpu

F32 = jnp.float32
BF16 = jnp.bfloat16

EPS = 1e-6
LOG2E = math.log2(math.e)
NEG_INF = -1e30
LOWEST = -3e38

D_MODEL = 1024
DEPTH = 4
PAGE = 128
CHUNK = 128
A_GROUPS = 4
A_WIDTH = 512
NSA_HEADS = 8
NSA_KV_HEADS = 2
NSA_REP = 4
HEAD_DIM = 64
L_CMP = 32
CMP_STRIDE = 16
L_SEL = 64
N_SEL = 8
WINDOW = 256
FORCE_BONUS = 1e3
MEM_HEADS = 4
MEM_HEAD_DIM = 128
MEM_WIDTH = 512
NUM_BUCKETS = 32
MAX_DISTANCE = 128
D_FF = 4096
OFF_Q = 1024
OFF_KV = 1536
OFF_NG = 2304
OFF_MQ = 2328
OFF_BG = 2840

TN = 512
TILE_KINDS = ("gelu_norm", "none", "half_sig",
              "sig", "sig", "sig", "sig", "sig", "sig", "none", "none", "gelu", "none")
N_F32_TILES = 3
Z32_COLS = TN * N_F32_TILES
Z16_COLS = TN * (len(TILE_KINDS) - N_F32_TILES)
CB32_V = 0
CB32_PAGED = 1
CB32_WN = 2
CB16_BG = 0
CB16_Q = 3
CB16_U = 8
CB16_MQ = 9
COL_PAGED = CB32_PAGED * TN
COL_WN = CB32_WN * TN

TQ = 128
KEY_UNROLL = 4
ROW_CHUNK = 256
VMEM_LIMIT = 56 * 1024 * 1024


def _cparams(sem):
    return pltpu.CompilerParams(dimension_semantics=sem, vmem_limit_bytes=VMEM_LIMIT)


def _rms(x, g):
    return x * lax.rsqrt(jnp.mean(x * x, axis=-1, keepdims=True) + EPS) * g


def _dot(a, b):
    return jnp.dot(a, b, preferred_element_type=F32)


def _dot_hi(a, b):
    return jnp.dot(a, b, preferred_element_type=F32, precision=lax.Precision.HIGHEST)


def _dot_nt(a, b):
    return lax.dot_general(a, b, (((1,), (1,)), ((), ())), preferred_element_type=F32)


def _split3(x):
    hi = x.astype(BF16)
    r1 = x - hi.astype(F32)
    mid = r1.astype(BF16)
    lo = (r1 - mid.astype(F32)).astype(BF16)
    return hi, mid, lo


def _bucket(dist):
    n = jnp.maximum(dist, 0)
    max_exact = NUM_BUCKETS // 2
    nf = jnp.maximum(n, 1).astype(F32)
    large = max_exact + (jnp.log(nf / max_exact) / math.log(MAX_DISTANCE / max_exact)
                         * (NUM_BUCKETS - max_exact)).astype(jnp.int32)
    return jnp.where(n < max_exact, n, jnp.minimum(large, NUM_BUCKETS - 1))


def _bias_table(rel_ref, o_ref, dist, valid, lead=(), mult=1.0):
    bucket = _bucket(dist)
    for h in range(NSA_HEADS):
        out = jnp.full(dist.shape, NEG_INF, F32)
        for b in range(NUM_BUCKETS):
            out = jnp.where(bucket == b, rel_ref[b, h], out)
        o_ref[lead + (h,)] = jnp.where(valid, out * mult, NEG_INF)


def _tables_kernel(rel_ref, toep_ref, tpl_ref, bs_ref, bn_ref, bw_ref, bcs_ref, *, past, n_cmp_s):
    def iota(shape, axis):
        return lax.broadcasted_iota(jnp.int32, shape, axis)

    sh = (TQ, TQ)
    d = iota(sh, 1) - iota(sh, 0)
    yes = d > -10 * TQ
    _bias_table(rel_ref, toep_ref, d, d >= 0, (0,), LOG2E)
    _bias_table(rel_ref, toep_ref, d + TQ, yes, (1,), LOG2E)
    _bias_table(rel_ref, toep_ref, d + 2 * TQ, yes, (2,), LOG2E)
    _bias_table(rel_ref, toep_ref, d + 2 * TQ, d + 2 * TQ < WINDOW, (3,), LOG2E)
    _bias_table(rel_ref, toep_ref, d, d > 10 * TQ, (4,), LOG2E)
    nj2 = tpl_ref.shape[1]
    sh = (nj2, TQ)
    dc = iota(sh, 1) - CMP_STRIDE * (iota(sh, 0) - nj2 // 2) - (L_CMP - 1)
    _bias_table(rel_ref, tpl_ref, dc, dc >= 0, (), LOG2E)
    ts = bs_ref.shape[1]
    sh = (ts, past)
    _bias_table(rel_ref, bs_ref, past + iota(sh, 0) - iota(sh, 1), iota(sh, 0) >= 0, (), LOG2E)
    sh = (ts, 128)
    dn = iota(sh, 0) - iota(sh, 1)
    _bias_table(rel_ref, bn_ref, dn, (dn >= 0) & (iota(sh, 1) < ts), (), LOG2E)
    wb = bw_ref.shape[2]
    sh = (ts, wb)
    dw = wb + iota(sh, 0) - iota(sh, 1)
    _bias_table(rel_ref, bw_ref, dw, dw < WINDOW, (), LOG2E)
    njs = bcs_ref.shape[2]
    sh = (ts, njs)
    dcs = past + iota(sh, 0) - CMP_STRIDE * iota(sh, 1) - (L_CMP - 1)
    _bias_table(rel_ref, bcs_ref, dcs, (dcs >= 0) & (iota(sh, 1) < n_cmp_s), (), LOG2E)


def _tables(rel_bias, t, past, ts, wb):
    nj = t // CMP_STRIDE
    njs = past // CMP_STRIDE
    n_cmp_s = (past + ts - L_CMP) // CMP_STRIDE + 1
    h = NSA_HEADS
    shapes = [(5, h, TQ, TQ), (h, 2 * nj, TQ), (h, ts, past), (h, ts, 128), (h, ts, wb), (h, ts, njs)]
    toep_t, tpl, bias_s, bias_n, bias_w, bias_cs = pl.pallas_call(
        functools.partial(_tables_kernel, past=past, n_cmp_s=n_cmp_s),
        in_specs=[pl.BlockSpec(memory_space=pltpu.SMEM)],
        out_shape=[jax.ShapeDtypeStruct(s, F32) for s in shapes],
        name="bias_tables",
    )(rel_bias)
    bias_ct = jnp.concatenate([tpl[:, nj - 8 * qt:2 * nj - 8 * qt, :] for qt in range(t // TQ)], axis=2)
    rows = h * ts
    return (toep_t, bias_ct, bias_s.reshape(rows, past), bias_n.reshape(rows, 128),
            bias_w.reshape(rows, wb), bias_cs.reshape(rows, njs))


def _norm_matmul_kernel(x_ref, g_ref, w_ref, gv_ref, *rest, kinds, n_f32):
    out_refs, hn_ref = rest[:-1], rest[-1]
    n = pl.program_id(1)
    tm = x_ref.shape[0]
    rc = min(ROW_CHUNK, tm)

    @pl.when(n == 0)
    def _():
        hn_ref[...] = _rms(x_ref[...], g_ref[...]).astype(BF16)

    for kind, wide in sorted(set((k, i < n_f32) for i, k in enumerate(kinds))):
        tiles = [i for i, k in enumerate(kinds) if k == kind and (i < n_f32) == wide]
        cond = functools.reduce(jnp.logical_or, [n == i for i in tiles])
        o_ref = out_refs[0] if wide else out_refs[1]

        @pl.when(cond)
        def _(kind=kind, o_ref=o_ref):
            for r in range(tm // rc):
                acc = _dot(hn_ref[r * rc:(r + 1) * rc, :], w_ref[...])
                if kind == "gelu":
                    acc = jax.nn.gelu(acc)
                elif kind == "gelu_norm":
                    acc = _rms(jax.nn.gelu(acc), gv_ref[...])
                elif kind == "sig":
                    acc = jax.nn.sigmoid(acc)
                elif kind == "half_sig":
                    half = acc.shape[1] // 2
                    acc = jnp.concatenate([acc[:, :half], jax.nn.sigmoid(acc[:, half:])], axis=1)
                o_ref[r * rc:(r + 1) * rc, :] = acc.astype(o_ref.dtype)


def _norm_matmul(x, gain, w, gv, kinds, n_f32, tm, name):
    m, d = x.shape
    n_tiles = len(kinds)
    tn = w.shape[1] // n_tiles
    out_specs = [pl.BlockSpec((tm, tn), lambda i, j: (i, jnp.minimum(j, n_f32 - 1)))]
    out_shape = [jax.ShapeDtypeStruct((m, n_f32 * tn), F32)]
    if n_tiles > n_f32:
        out_specs.append(pl.BlockSpec((tm, tn), lambda i, j: (i, jnp.maximum(j - n_f32, 0))))
        out_shape.append(jax.ShapeDtypeStruct((m, (n_tiles - n_f32) * tn), BF16))
    outs = pl.pallas_call(
        functools.partial(_norm_matmul_kernel, kinds=kinds, n_f32=n_f32),
        grid=(m // tm, n_tiles),
        in_specs=[
            pl.BlockSpec((tm, d), lambda i, j: (i, 0)),
            pl.BlockSpec((1, d), lambda i, j: (0, 0)),
            pl.BlockSpec((d, tn), lambda i, j: (0, j)),
            pl.BlockSpec((1, tn), lambda i, j: (0, 0)),
        ],
        out_specs=out_specs,
        out_shape=out_shape,
        scratch_shapes=[pltpu.VMEM((tm, d), BF16)],
        compiler_params=_cparams(("parallel", "arbitrary")),
        name=name,
    )(x, gain.reshape(1, d), w, gv.reshape(1, tn))
    return outs[0], (outs[1] if len(outs) > 1 else None)


def _cmp_copies(pt_ref, src_ref, buf_ref, sem_ref, step, slot, *, nr, n_pages, col0, layer):
    copies = []
    for r in range(nr):
        for p in range(n_pages):
            pid = pt_ref[(step * nr + r) * n_pages + p]
            if layer is None:
                for c in range(2):
                    copies.append(pltpu.make_async_copy(
                        src_ref.at[pid, :, pl.ds(col0 + c * 128, 128)],
                        buf_ref.at[slot, r, c, pl.ds(p * PAGE, PAGE), :],
                        sem_ref.at[slot]))
            else:
                copies.append(pltpu.make_async_copy(
                    src_ref.at[layer, pid, pl.ds(0, 2)],
                    buf_ref.at[slot, r, :, :, :, pl.ds(p * PAGE, PAGE)],
                    sem_ref.at[slot]))
    return copies


def _compress_kernel(pt_ref, src_ref, wc_ref, per_ref, wphi_ref, o_ref, buf_ref, sem_ref, *slab,
                     nr, n_pages, col0, layer):
    step = pl.program_id(0)
    n_steps = pl.num_programs(0)
    slot = lax.rem(step, 2)
    kw = dict(nr=nr, n_pages=n_pages, col0=col0, layer=layer)

    @pl.when(step == 0)
    def _():
        for cp in _cmp_copies(pt_ref, src_ref, buf_ref, sem_ref, step, slot, **kw):
            cp.start()

    @pl.when(step + 1 < n_steps)
    def _():
        for cp in _cmp_copies(pt_ref, src_ref, buf_ref, sem_ref, step + 1, 1 - slot, **kw):
            cp.start()

    for cp in _cmp_copies(pt_ref, src_ref, buf_ref, sem_ref, step, slot, **kw):
        cp.wait()

    rows = n_pages * PAGE
    m = rows // CMP_STRIDE
    if layer is not None:
        slab_ref, = slab
        for r in range(nr):
            for c in range(2):
                for p in range(n_pages):
                    blk = buf_ref[slot, r, c, :, :, pl.ds(p * PAGE, PAGE)].reshape(128, PAGE)
                    slab_ref[r, c, pl.ds(p * PAGE, PAGE), :] = blk.T

    outs = []
    for c in range(2):
        per_req = []
        for r in range(nr):
            if layer is None:
                slabs = [buf_ref[slot, r, c, pl.ds(l, m, stride=CMP_STRIDE), :] for l in range(CMP_STRIDE)]
            else:
                slabs = [slab_ref[r, c, pl.ds(l, m, stride=CMP_STRIDE), :] for l in range(CMP_STRIDE)]
            per_req.append(jnp.concatenate(slabs, axis=1))
        lhs = jnp.concatenate(per_req, axis=0).astype(BF16)
        p = _dot(lhs, wc_ref[c])
        bias = _dot_hi(per_ref[c], wphi_ref[c])[0:1]
        hi = pltpu.roll(p[:, 128:], nr * m - 1, 0)
        outs.append(p[:, :128] + hi + bias)
    res = jnp.concatenate(outs, axis=1)
    o_ref[...] = res.reshape(nr, m, 256)


def _compress(page_table, src, wc, pe_rep, wphi2, *, col0, nr, layer):
    nb, n_pages = page_table.shape
    rows = n_pages * PAGE
    m = rows // CMP_STRIDE
    if layer is None:
        scratch = [pltpu.VMEM((2, nr, 2, rows, 128), F32), pltpu.SemaphoreType.DMA((2,))]
    else:
        scratch = [pltpu.VMEM((2, nr, 2, NSA_KV_HEADS, HEAD_DIM, rows), F32), pltpu.SemaphoreType.DMA((2,)),
                   pltpu.VMEM((nr, 2, rows, 128), F32)]
    grid_spec = pltpu.PrefetchScalarGridSpec(
        num_scalar_prefetch=1,
        grid=(nb // nr,),
        in_specs=[
            pl.BlockSpec(memory_space=pl.ANY),
            pl.BlockSpec((2, CMP_STRIDE * 128, 256), lambda i, pt: (0, 0, 0)),
            pl.BlockSpec((2, 8, L_CMP * HEAD_DIM), lambda i, pt: (0, 0, 0)),
            pl.BlockSpec((2, L_CMP * HEAD_DIM, 128), lambda i, pt: (0, 0, 0)),
        ],
        out_specs=pl.BlockSpec((nr, m, 256), lambda i, pt: (i, 0, 0)),
        scratch_shapes=scratch,
    )
    return pl.pallas_call(
        functools.partial(_compress_kernel, nr=nr, n_pages=n_pages, col0=col0, layer=layer),
        grid_spec=grid_spec,
        out_shape=jax.ShapeDtypeStruct((nb, m, 256), F32),
        compiler_params=_cparams(("arbitrary",)),
        name="compress",
    )(page_table.reshape(-1), src, wc, pe_rep, wphi2)


def _select_blocks_t(imp_t, pos, n_slc):
    ns, c = imp_t.shape
    s_i = lax.broadcasted_iota(jnp.int32, (ns, c), 0)
    cur = lax.shift_right_logical(pos, 6)
    valid = (s_i * L_SEL) <= pos
    forced = (s_i == 0) | (s_i == cur) | (s_i == cur - 1)
    vals = jnp.where(valid, imp_t + jnp.where(forced, FORCE_BONUS, 0.0), -1.0)
    vals = jnp.where(s_i < n_slc, vals, LOWEST)
    rank = jnp.zeros((ns, c), F32)
    for sp in range(n_slc):
        row = vals[sp:sp + 1, :]
        beats = (row > vals) | ((row == vals) & (s_i > sp))
        rank = rank + jnp.where(beats, 1.0, 0.0)
    return jnp.where((rank < min(N_SEL, n_slc)) & (s_i < n_slc), 1.0, 0.0)


def _importance_t(ovl_t, p_sum):
    return functools.reduce(lambda a, b: a + b, [_dot(ovl_t, part) for part in _split3(p_sum)])


def _expand_gates(ng, eg_ref):
    parts = _split3(ng)
    return [functools.reduce(lambda a, b: a + b, [_dot(part, eg_ref[t]) for part in parts]) for t in range(3)]


def _pad_rows(x, rows):
    if x.shape[0] == rows:
        return x
    return jnp.concatenate([x, jnp.zeros((rows - x.shape[0],) + x.shape[1:], x.dtype)], axis=0)


def _nsa_prompt_kernel(q_ref, pg_ref, wn_ref, kvc_ref, bct_ref, toep_ref, ovl_ref, ef_ref, eg_ref, *rest, n_slc):
    o_ref, kvo_ref, wno_ref, ks_ref, vst_ref, kw_ref, vwt_ref, mft_ref = rest[-8:]
    qt = pl.program_id(1)
    t_len = pg_ref.shape[1]
    nj = kvc_ref.shape[1]
    keep0 = t_len - wno_ref.shape[5]

    @pl.when(qt == 0)
    def _():
        ks_ref[...] = pg_ref[0, :, 256:384].astype(BF16)
        kw_ref[...] = wn_ref[0, :, 0:128].astype(BF16)
        for blk in range(t_len // TQ):
            rs = slice(blk * TQ, (blk + 1) * TQ)
            for slot in range(4):
                tile = pg_ref[0, rs, slot * 128:(slot + 1) * 128].T
                kvo_ref[0, 0, slot, :, :, rs] = tile.reshape(NSA_KV_HEADS, HEAD_DIM, TQ)
                if slot == 3:
                    vst_ref[:, rs] = tile.astype(BF16)
            for c in range(2):
                tile = wn_ref[0, rs, c * 128:(c + 1) * 128].T
                if blk * TQ >= keep0:
                    ws = slice(blk * TQ - keep0, (blk + 1) * TQ - keep0)
                    wno_ref[0, 0, c, :, :, ws] = tile.reshape(NSA_KV_HEADS, HEAD_DIM, TQ)
                if c == 1:
                    vwt_ref[:, rs] = tile.astype(BF16)

    qts = [q_ref[0, :, h * 128:(h + 1) * 128].astype(F32).T.astype(BF16) for h in range(NSA_HEADS)]
    groups = [h // NSA_REP for h in range(NSA_HEADS)]

    def score_dots(k_ref, kts):
        starts = [pl.multiple_of(kt * TQ, TQ) for kt in kts]
        ks = [k_ref[pl.ds(st, TQ), :] for st in starts]
        return starts, [[_dot(k, qts[h]) for k in ks] for h in range(NSA_HEADS)]

    all_heads = tuple(range(NSA_HEADS))

    def softmax_update(raw, starts, tbls, masked, carry, heads=all_heads):
        stats, probs = [], []
        for h in heads:
            g = groups[h]
            m_i, l_i, _ = carry[h]
            ss = []
            for s, st, tbl in zip(raw[h], starts, tbls):
                s = s + toep_ref[tbl, h]
                if masked:
                    s = s + mft_ref[pl.ds(st, TQ), g * TQ:(g + 1) * TQ]
                ss.append(s)
            mx = functools.reduce(jnp.maximum, [jnp.max(s, axis=0, keepdims=True) for s in ss])
            m_new = jnp.maximum(m_i, mx)
            alpha = jnp.exp2(m_i - m_new)
            ps = [jnp.exp2(s - m_new) for s in ss]
            l_new = alpha * l_i + functools.reduce(
                lambda a, b: a + b, [jnp.sum(p, axis=0, keepdims=True) for p in ps])
            stats.append((m_new, l_new, alpha))
            probs.append(jnp.concatenate(ps, axis=0).astype(BF16))
        return stats, probs

    def value_dots(vt_ref, starts, stats, probs, carry, heads=all_heads):
        vts = [vt_ref[:, pl.ds(st, TQ)] for st in starts]
        new = []
        for i, h in enumerate(heads):
            g = groups[h]
            m_new, l_new, alpha = stats[i]
            vt = jnp.concatenate([v[g * HEAD_DIM:(g + 1) * HEAD_DIM, :] for v in vts], axis=1)
            new.append((m_new, l_new, alpha * carry[h][2] + _dot(vt, probs[i])))
        return tuple(new)

    init = tuple((jnp.full((1, TQ), NEG_INF, F32), jnp.zeros((1, TQ), F32),
                  jnp.zeros((HEAD_DIM, TQ), F32)) for _ in range(NSA_HEADS))

    def finish(state):
        return [acc * (1.0 / l_f) for (_, l_f, acc) in state]

    w_tiles = WINDOW // TQ + 1
    w_kts = [qt - (w_tiles - 1) + u for u in range(w_tiles)]
    w_tbls = [jnp.where(kt < 0, 4, tb) for kt, tb in zip(w_kts, (3, 1, 0))]
    w_starts, raw_w = score_dots(kw_ref, [jnp.maximum(kt, 0) for kt in w_kts])
    kc = kvc_ref[0, :, 0:128].astype(BF16)
    vct = kvc_ref[0, :, 128:256].T.astype(BF16)
    raw_c = [_dot(kc, qts[h]) for h in range(NSA_HEADS)]

    stats_w, probs_w = softmax_update(raw_w, w_starts, w_tbls, False, init)
    p_c, p_sum = [], [None] * NSA_KV_HEADS
    for h, g in enumerate(groups):
        s = raw_c[h] + bct_ref[h]
        mx = jnp.max(s, axis=0, keepdims=True)
        e = jnp.exp2(s - mx)
        p = jnp.where(mx > 0.5 * NEG_INF, e * (1.0 / jnp.sum(e, axis=0, keepdims=True)), 0.0)
        p_c.append(p.astype(BF16))
        p_sum[g] = p if p_sum[g] is None else p_sum[g] + p

    o_w = finish(value_dots(vwt_ref, w_starts, stats_w, probs_w, init))
    o_c = [_dot(vct[g * HEAD_DIM:(g + 1) * HEAD_DIM, :], p_c[h]) for h, g in enumerate(groups)]
    imp_t = _importance_t(ovl_ref[...], jnp.concatenate(p_sum, axis=1))
    ns = -(-n_slc // 8) * 8
    pos = qt * TQ + lax.rem(lax.broadcasted_iota(jnp.int32, (1, NSA_KV_HEADS * TQ), 1), TQ)
    sel_t = _select_blocks_t(imp_t[:ns], pos, n_slc)
    hidden = ((_pad_rows(sel_t, 128) - 1.0) * (-NEG_INF)).astype(BF16)
    mft_ref[...] = _dot(ef_ref[...], hidden)

    def sel_body(it, carry):
        kts = [it * KEY_UNROLL + u for u in range(KEY_UNROLL)]
        tbls = [jnp.where(qt - kt < 0, 4, jnp.minimum(qt - kt, 2)) for kt in kts]
        starts, raw = score_dots(ks_ref, kts)
        stats, probs = softmax_update(raw, starts, tbls, True, carry)
        return value_dots(vst_ref, starts, stats, probs, carry)

    o_s = finish(lax.fori_loop(0, (qt + KEY_UNROLL) // KEY_UNROLL, sel_body, init))

    start = pl.multiple_of(qt * TQ, TQ)
    gates = _expand_gates(wn_ref[0, pl.ds(start, TQ), 256:384], eg_ref)
    tiles = []
    for mt in range(NSA_HEADS // 2):
        acc = jnp.zeros((TQ, 128), F32)
        for o, gate in zip((o_c, o_s, o_w), gates):
            pair = jnp.concatenate([o[2 * mt], o[2 * mt + 1]], axis=0)
            acc = acc + gate[:, mt * 128:(mt + 1) * 128] * pair.T
        tiles.append(acc)
    o_ref[0] = jnp.concatenate(tiles, axis=1)


def _nsa_prompt(z32, z16, kvc, bias_ct, toep_t, ovl_t, ef_t, eg, layer, n_layers, kv_out, win_out):
    b, t, _ = z32.shape
    nj = kvc.shape[1]
    n_slc = -(-t // L_SEL)
    keep = min(WINDOW, t)
    kv_shape = (n_layers, b, 4, NSA_KV_HEADS, HEAD_DIM, t)
    win_shape = (n_layers, b, 2, NSA_KV_HEADS, HEAD_DIM, keep)
    args = [z16, z32, z32, kvc, bias_ct, toep_t, ovl_t, ef_t, eg]
    extra_specs, aliases = [], {}
    if kv_out is not None:
        extra_specs = [pl.BlockSpec(memory_space=pl.ANY)] * 2
        aliases = {len(args): 1, len(args) + 1: 2}
        args += [kv_out, win_out]
    return pl.pallas_call(
        functools.partial(_nsa_prompt_kernel, n_slc=n_slc),
        grid=(b, t // TQ),
        in_specs=[
            pl.BlockSpec((1, TQ, 1024), lambda i, j: (i, j, CB16_Q)),
            pl.BlockSpec((1, t, TN), lambda i, j: (i, 0, CB32_PAGED)),
            pl.BlockSpec((1, t, TN), lambda i, j: (i, 0, CB32_WN)),
            pl.BlockSpec((1, nj, 256), lambda i, j: (i, 0, 0)),
            pl.BlockSpec((NSA_HEADS, nj, TQ), lambda i, j: (0, 0, j)),
            pl.BlockSpec((5, NSA_HEADS, TQ, TQ), lambda i, j: (0, 0, 0, 0)),
            pl.BlockSpec((128, nj), lambda i, j: (0, 0)),
            pl.BlockSpec((t, 128), lambda i, j: (0, 0)),
            pl.BlockSpec((3, 128, 512), lambda i, j: (0, 0, 0)),
        ] + extra_specs,
        out_specs=[pl.BlockSpec((1, TQ, 512), lambda i, j: (i, j, 0)),
                   pl.BlockSpec((1, 1) + kv_shape[2:], lambda i, j: (layer, i, 0, 0, 0, 0)),
                   pl.BlockSpec((1, 1) + win_shape[2:], lambda i, j: (layer, i, 0, 0, 0, 0))],
        out_shape=[jax.ShapeDtypeStruct((b, t, 512), F32), jax.ShapeDtypeStruct(kv_shape, F32),
                   jax.ShapeDtypeStruct(win_shape, F32)],
        input_output_aliases=aliases,
        scratch_shapes=[pltpu.VMEM((t, 128), BF16), pltpu.VMEM((128, t), BF16),
                        pltpu.VMEM((t, 128), BF16), pltpu.VMEM((128, t), BF16),
                        pltpu.VMEM((t, NSA_KV_HEADS * TQ), F32)],
        compiler_params=_cparams(("arbitrary", "arbitrary")),
        name="nsa_prompt",
    )(*args)


def _slc_copies(pt_ref, src_ref, buf_ref, sem_ref, step, slot, *, nr, n_pages, layer):
    copies = []
    for r in range(nr):
        for p in range(n_pages):
            pid = pt_ref[(step * nr + r) * n_pages + p]
            copies.append(pltpu.make_async_copy(
                src_ref.at[layer, pid, pl.ds(2, 2)],
                buf_ref.at[slot, r, :, :, :, pl.ds(p * PAGE, PAGE)],
                sem_ref.at[slot]))
    return copies


def _nsa_sample_kernel(pt_ref, src_ref, q_ref, pg_ref, wn_ref, cw_ref, kvc_ref, bs_ref, bn_ref, bw_ref,
                       bcs_ref, ovl_ref, ef_ref, eg_ref, *rest, nr, n_pages, n_slc, pos0, layer):
    o_ref, wo_ref, buf_ref, sem_ref = rest[-4:]
    _nsa_sample_body(pt_ref, src_ref, q_ref, pg_ref, wn_ref, cw_ref, kvc_ref, bs_ref, bn_ref, bw_ref,
                     bcs_ref, ovl_ref, ef_ref, eg_ref, o_ref, wo_ref, buf_ref, sem_ref,
                     nr=nr, n_pages=n_pages, n_slc=n_slc, pos0=pos0, layer=layer)


def _nsa_sample_body(pt_ref, src_ref, q_ref, pg_ref, wn_ref, cw_ref, kvc_ref, bs_ref, bn_ref, bw_ref,
                     bcs_ref, ovl_ref, ef_ref, eg_ref, o_ref, wo_ref, buf_ref, sem_ref,
                     *, nr, n_pages, n_slc, pos0, layer):
    step = pl.program_id(0)
    n_steps = pl.num_programs(0)
    slot = lax.rem(step, 2)
    tq = q_ref.shape[1]
    rows = NSA_HEADS * tq
    nj = kvc_ref.shape[1]
    past = n_pages * PAGE
    wb = cw_ref.shape[5]
    reqs = range(nr)
    kw = dict(nr=nr, n_pages=n_pages, layer=layer)

    @pl.when(step == 0)
    def _():
        for cp in _slc_copies(pt_ref, src_ref, buf_ref, sem_ref, step, slot, **kw):
            cp.start()

    @pl.when(step + 1 < n_steps)
    def _():
        for cp in _slc_copies(pt_ref, src_ref, buf_ref, sem_ref, step + 1, 1 - slot, **kw):
            cp.start()

    for cp in _slc_copies(pt_ref, src_ref, buf_ref, sem_ref, step, slot, **kw):
        cp.wait()

    def new_rows(x):
        return _pad_rows(x, 128).astype(BF16)

    def row_max(*xs):
        return functools.reduce(jnp.maximum, [jnp.max(x, axis=-1, keepdims=True) for x in xs])

    def row_sum(*xs):
        return functools.reduce(lambda a, b: a + b, [jnp.sum(x, axis=-1, keepdims=True) for x in xs])

    qs, pgs, wns, raw = [], [], [], []
    for r in reqs:
        qf = q_ref[r].astype(F32)
        q = jnp.concatenate([qf[:, h * 128:(h + 1) * 128] for h in range(NSA_HEADS)], axis=0).astype(BF16)
        pg, wn = pg_ref[r], wn_ref[r]
        qs.append(q)
        pgs.append(pg)
        wns.append(wn)
        raw.append(dict(
            c=_dot_nt(q, kvc_ref[r, :, 0:128].astype(BF16)),
            w_old=_dot(q, cw_ref[0, r, 0].reshape(128, wb).astype(BF16)),
            w_new=_dot_nt(q, new_rows(wn[:, 0:128])),
            s_old=_dot(q, buf_ref[slot, r, 0].reshape(128, past).astype(BF16)),
            s_new=_dot_nt(q, new_rows(pg[:, 256:384]))))

    p_cs, win = [], []
    for r in reqs:
        s_c = raw[r]["c"] + bcs_ref[...]
        mx = row_max(s_c)
        e_c = jnp.exp2(s_c - mx)
        p_cs.append(jnp.where(mx > 0.5 * NEG_INF, e_c * (1.0 / row_sum(e_c)), 0.0))
        s_o, s_n = raw[r]["w_old"] + bw_ref[...], raw[r]["w_new"] + bn_ref[...]
        m2 = row_max(s_o, s_n)
        e_o, e_n = jnp.exp2(s_o - m2), jnp.exp2(s_n - m2)
        win.append((e_o.astype(BF16), e_n.astype(BF16), 1.0 / row_sum(e_o, e_n)))

    o_cs, o_ws, imps, gates = [], [], [], []
    for r in reqs:
        o_cs.append(_dot(p_cs[r].astype(BF16), kvc_ref[r, :, 128:256].astype(BF16)))
        e_o, e_n, inv = win[r]
        o_ws.append((_dot_nt(e_o, cw_ref[0, r, 1].reshape(128, wb).astype(BF16))
                     + _dot(e_n, new_rows(wns[r][:, 128:256]))) * inv)
        p_g = jnp.sum(p_cs[r].reshape(NSA_KV_HEADS, NSA_REP, tq, nj), axis=1).reshape(NSA_KV_HEADS * tq, nj)
        imps.append(functools.reduce(lambda a, b: a + b,
                                     [_dot_nt(ovl_ref[...], part) for part in _split3(_pad_rows(p_g, 128))]))
        gates.append(_expand_gates(wns[r][:, 256:384], eg_ref))

    ns = -(-n_slc // 8) * 8
    lane = lax.broadcasted_iota(jnp.int32, (1, 128), 1)
    mfs = []
    for r in reqs:
        sel_t = _select_blocks_t(imps[r][:ns], pos0 + lax.rem(lane, tq), n_slc)
        sel = _pad_rows(sel_t, 128).T[:NSA_KV_HEADS * tq]
        mfs.append(_dot(sel.astype(BF16), ef_ref[...]))

    sel_probs = []
    for r in reqs:
        mf = jnp.broadcast_to(mfs[r].reshape(NSA_KV_HEADS, 1, tq, past),
                              (NSA_KV_HEADS, NSA_REP, tq, past)).reshape(rows, past)
        s_o = raw[r]["s_old"] + jnp.where(mf > 0.5, bs_ref[...], NEG_INF)
        s_n = raw[r]["s_new"] + bn_ref[...]
        m2 = row_max(s_o, s_n)
        e_o, e_n = jnp.exp2(s_o - m2), jnp.exp2(s_n - m2)
        sel_probs.append((e_o.astype(BF16), e_n.astype(BF16), 1.0 / row_sum(e_o, e_n)))
    o_ss = []
    for r in reqs:
        e_o, e_n, inv = sel_probs[r]
        o_ss.append((_dot_nt(e_o, buf_ref[slot, r, 1].reshape(128, past).astype(BF16))
                     + _dot(e_n, new_rows(pgs[r][:, 384:512]))) * inv)

    lane_w = lax.broadcasted_iota(jnp.int32, (128, wb), 1)
    for r in reqs:
        for c in range(2):
            kept = pltpu.roll(cw_ref[0, r, c].reshape(128, wb), wb - tq, 1)
            fresh = pltpu.roll(_pad_rows(wns[r][:, c * 128:(c + 1) * 128], 128).T, 128 - tq, 1)
            if wb > 128:
                fresh = jnp.concatenate([jnp.zeros((128, wb - 128), F32), fresh], axis=1)
            wo_ref[0, r, c] = jnp.where(lane_w >= wb - tq, fresh, kept).reshape(NSA_KV_HEADS, HEAD_DIM, wb)

    lane128 = lax.broadcasted_iota(jnp.int32, (tq, 128), 1)
    for r in reqs:
        tiles = []
        for mt in range(NSA_HEADS // 2):
            g = (2 * mt) // NSA_REP
            acc = jnp.zeros((tq, 128), F32)
            for o, gate in zip((o_cs[r], o_ss[r], o_ws[r]), gates[r]):
                a = o[(2 * mt) * tq:(2 * mt + 1) * tq]
                b = o[(2 * mt + 1) * tq:(2 * mt + 2) * tq]
                if g == 1:
                    a = pltpu.roll(a, 64, 1)
                else:
                    b = pltpu.roll(b, 64, 1)
                acc = acc + gate[:, mt * 128:(mt + 1) * 128] * jnp.where(lane128 < 64, a, b)
            tiles.append(acc)
        o_ref[r] = jnp.concatenate(tiles, axis=1)


def _nsa_sample(page_table, cache_t, cwin_t, layer, z32, z16, kvc, bias_s, bias_n, bias_w, bias_cs, ovl_t, ef, eg,
                nr, win_out):
    nb, n_pages = page_table.shape
    past = n_pages * PAGE
    tq = z32.shape[1]
    rows = NSA_HEADS * tq
    nj = kvc.shape[1]
    n_slc = -(-(past + tq) // L_SEL)
    wb = cwin_t.shape[5]
    const2 = lambda i, pt: (0, 0)
    win_block = pl.BlockSpec((1, nr, 2, NSA_KV_HEADS, HEAD_DIM, wb), lambda i, pt: (layer, i, 0, 0, 0, 0))
    args = [page_table.reshape(-1), cache_t, z16, z32, z32, cwin_t, kvc, bias_s, bias_n, bias_w, bias_cs,
            ovl_t, ef, eg]
    extra_specs, aliases = [], {}
    if win_out is not None:
        extra_specs = [pl.BlockSpec(memory_space=pl.ANY)]
        aliases = {len(args): 1}
        args.append(win_out)
    grid_spec = pltpu.PrefetchScalarGridSpec(
        num_scalar_prefetch=1,
        grid=(nb // nr,),
        in_specs=[
            pl.BlockSpec(memory_space=pl.ANY),
            pl.BlockSpec((nr, tq, 1024), lambda i, pt: (i, 0, CB16_Q)),
            pl.BlockSpec((nr, tq, TN), lambda i, pt: (i, 0, CB32_PAGED)),
            pl.BlockSpec((nr, tq, TN), lambda i, pt: (i, 0, CB32_WN)),
            win_block,
            pl.BlockSpec((nr, nj, 256), lambda i, pt: (i, 0, 0)),
            pl.BlockSpec((rows, past), const2),
            pl.BlockSpec((rows, 128), const2),
            pl.BlockSpec((rows, wb), const2),
            pl.BlockSpec((rows, nj), const2),
            pl.BlockSpec((128, nj), const2),
            pl.BlockSpec((128, past), const2),
            pl.BlockSpec((3, 128, 512), lambda i, pt: (0, 0, 0)),
        ] + extra_specs,
        out_specs=[pl.BlockSpec((nr, tq, 512), lambda i, pt: (i, 0, 0)), win_block],
        scratch_shapes=[pltpu.VMEM((2, nr, 2, NSA_KV_HEADS, HEAD_DIM, past), F32),
                        pltpu.SemaphoreType.DMA((2,))],
    )
    return pl.pallas_call(
        functools.partial(_nsa_sample_kernel, nr=nr, n_pages=n_pages, n_slc=n_slc, pos0=past, layer=layer),
        grid_spec=grid_spec,
        out_shape=[jax.ShapeDtypeStruct((nb, tq, 512), F32), jax.ShapeDtypeStruct(cwin_t.shape, F32)],
        input_output_aliases=aliases,
        compiler_params=_cparams(("arbitrary",)),
        name="nsa_sample",
    )(*args)


def _mem_attn_kernel(q_ref, kv_ref, o_ref):
    q = q_ref[0]
    heads = range(MEM_HEADS)
    scores = [_dot_nt(q[:, h * 128:(h + 1) * 128], kv_ref[0, :, h * 128:(h + 1) * 128].astype(BF16))
              for h in heads]
    probs = []
    for s in scores:
        s = s * (MEM_HEAD_DIM ** -0.5)
        e = jnp.exp(s - jnp.max(s, axis=-1, keepdims=True))
        probs.append((e * (1.0 / jnp.sum(e, axis=-1, keepdims=True))).astype(BF16))
    o_ref[0] = jnp.concatenate(
        [_dot(probs[h], kv_ref[0, :, MEM_WIDTH + h * 128:MEM_WIDTH + (h + 1) * 128].astype(BF16))
         for h in heads], axis=1)


def _mem_attn(z16, mem_kv, tq):
    b, t, _ = z16.shape
    n_mem = mem_kv.shape[1]
    return pl.pallas_call(
        _mem_attn_kernel,
        grid=(b, t // tq),
        in_specs=[
            pl.BlockSpec((1, tq, TN), lambda i, j: (i, j, CB16_MQ)),
            pl.BlockSpec((1, n_mem, 2 * MEM_WIDTH), lambda i, j: (i, 0, 0)),
        ],
        out_specs=pl.BlockSpec((1, tq, MEM_WIDTH), lambda i, j: (i, j, 0)),
        out_shape=jax.ShapeDtypeStruct((b, t, MEM_WIDTH), F32),
        compiler_params=_cparams(("parallel", "parallel")),
        name="mem_attn",
    )(z16, mem_kv)


def _mem_attn_cache_kernel(q_ref, kv_ref, o_ref):
    nr, tq, _ = q_ref.shape
    n_mem = kv_ref.shape[2] // (2 * MEM_HEADS)
    pairs = [(r, h) for r in range(nr) for h in range(MEM_HEADS)]
    qs = [q_ref[r].astype(F32) for r in range(nr)]
    scores = []
    for r, h in pairs:
        kh = kv_ref[0, r, pl.ds(h, n_mem, stride=2 * MEM_HEADS), :].astype(BF16)
        scores.append(_dot_nt(qs[r][:, h * 128:(h + 1) * 128].astype(BF16), kh))
    s = jnp.concatenate(scores, axis=0) * (MEM_HEAD_DIM ** -0.5)
    mx = jnp.max(s, axis=-1, keepdims=True)
    e = jnp.exp(s - mx)
    p = e * (1.0 / jnp.sum(e, axis=-1, keepdims=True))
    outs = []
    for i, (r, h) in enumerate(pairs):
        vh = kv_ref[0, r, pl.ds(MEM_HEADS + h, n_mem, stride=2 * MEM_HEADS), :].astype(BF16)
        outs.append(_dot(p[i * tq:(i + 1) * tq].astype(BF16), vh))
    for r in range(nr):
        o_ref[r] = jnp.concatenate(outs[r * MEM_HEADS:(r + 1) * MEM_HEADS], axis=1)


def _mem_attn_cache(z16, cache_mem, layer, nr):
    b, tq, _ = z16.shape
    nl, _, n_mem = cache_mem.shape[:3]
    rows = n_mem * 2 * MEM_HEADS
    return pl.pallas_call(
        _mem_attn_cache_kernel,
        grid=(b // nr,),
        in_specs=[
            pl.BlockSpec((nr, tq, TN), lambda i: (i, 0, CB16_MQ)),
            pl.BlockSpec((1, nr, rows, MEM_HEAD_DIM), lambda i: (layer, i, 0, 0)),
        ],
        out_specs=pl.BlockSpec((nr, tq, MEM_WIDTH), lambda i: (i, 0, 0)),
        out_shape=jax.ShapeDtypeStruct((b, tq, MEM_WIDTH), F32),
        compiler_params=_cparams(("parallel",)),
        name="mem_attn_cache",
    )(z16, cache_mem.reshape(nl, b, rows, MEM_HEAD_DIM))


def _merge_kernel(x_ref, u_ref, v_ref, g0_ref, g1_ref, g2_ref, on_ref, om_ref, wsp_ref, bsp_ref,
                  wa_ref, wb_ref, wc_ref, wo_ref, o_ref):
    tm = x_ref.shape[0]
    u = u_ref[...].astype(F32)
    v = v_ref[...].astype(BF16)
    chunks = []
    for c in range(tm // CHUNK):
        parts = []
        for g in range(A_GROUPS):
            vg = v[c * CHUNK:(c + 1) * CHUNK, g * 128:(g + 1) * 128]
            parts.append(_dot(wsp_ref[g], vg) + bsp_ref[g])
        chunks.append(jnp.concatenate(parts, axis=1))
    ya = (u * jnp.concatenate(chunks, axis=0)).astype(BF16)
    merged = (g0_ref[...].astype(F32) * _dot(ya, wa_ref[...])
              + g1_ref[...].astype(F32) * _dot(on_ref[...].astype(BF16), wb_ref[...])
              + g2_ref[...].astype(F32) * _dot(om_ref[...].astype(BF16), wc_ref[...]))
    o_ref[...] = x_ref[...] + _dot(merged.astype(BF16), wo_ref[...])


def _merge(x, z32, z16, o_nsa, o_mem, wsp, bsp, w_a, w_b, w_c, w_out, tm):
    m, d = x.shape
    row = lambda i: (i, 0)
    c2 = lambda i: (0, 0)
    c3 = lambda i: (0, 0, 0)
    return pl.pallas_call(
        _merge_kernel,
        grid=(m // tm,),
        in_specs=[
            pl.BlockSpec((tm, d), row),
            pl.BlockSpec((tm, TN), lambda i: (i, CB16_U)),
            pl.BlockSpec((tm, TN), lambda i: (i, CB32_V)),
            pl.BlockSpec((tm, d), lambda i: (i, CB16_BG)),
            pl.BlockSpec((tm, d), lambda i: (i, CB16_BG + 1)),
            pl.BlockSpec((tm, d), lambda i: (i, CB16_BG + 2)),
            pl.BlockSpec((tm, 512), row),
            pl.BlockSpec((tm, 512), row),
            pl.BlockSpec((A_GROUPS, CHUNK, CHUNK), c3),
            pl.BlockSpec((A_GROUPS, CHUNK, 128), c3),
            pl.BlockSpec((A_WIDTH, d), c2),
            pl.BlockSpec((512, d), c2),
            pl.BlockSpec((MEM_WIDTH, d), c2),
            pl.BlockSpec((d, d), c2),
        ],
        out_specs=pl.BlockSpec((tm, d), row),
        out_shape=jax.ShapeDtypeStruct((m, d), F32),
        compiler_params=_cparams(("parallel",)),
        name="merge",
    )(x, z16, z32, z16, z16, z16, o_nsa, o_mem, wsp, bsp, w_a, w_b, w_c, w_out)


def _mlp_kernel(x_ref, g_ref, wu_ref, wd_ref, o_ref, hn_ref, acc_ref):
    f = pl.program_id(1)

    @pl.when(f == 0)
    def _():
        hn_ref[...] = _rms(x_ref[...], g_ref[...]).astype(BF16)
        acc_ref[...] = jnp.zeros_like(acc_ref)

    h = jnp.square(jnp.maximum(_dot(hn_ref[...], wu_ref[...]), 0.0))
    acc_ref[...] += _dot(h.astype(BF16), wd_ref[...])

    @pl.when(f == pl.num_programs(1) - 1)
    def _():
        o_ref[...] = x_ref[...] + acc_ref[...]


def _mlp(x, gain, w_up, w_down, tm, tf):
    m, d = x.shape
    ff = w_up.shape[1]
    return pl.pallas_call(
        _mlp_kernel,
        grid=(m // tm, ff // tf),
        in_specs=[
            pl.BlockSpec((tm, d), lambda i, j: (i, 0)),
            pl.BlockSpec((1, d), lambda i, j: (0, 0)),
            pl.BlockSpec((d, tf), lambda i, j: (0, j)),
            pl.BlockSpec((tf, d), lambda i, j: (j, 0)),
        ],
        out_specs=pl.BlockSpec((tm, d), lambda i, j: (i, 0)),
        out_shape=jax.ShapeDtypeStruct((m, d), F32),
        scratch_shapes=[pltpu.VMEM((tm, d), BF16), pltpu.VMEM((tm, d), F32)],
        compiler_params=_cparams(("parallel", "arbitrary")),
        name="mlp",
    )(x, gain.reshape(1, d), w_up, w_down)


def _final_norm_kernel(x_ref, g_ref, o_ref):
    o_ref[...] = _rms(x_ref[...], g_ref[...])


def _final_norm(x, gain, tm):
    m, d = x.shape
    return pl.pallas_call(
        _final_norm_kernel,
        grid=(m // tm,),
        in_specs=[pl.BlockSpec((tm, d), lambda i: (i, 0)), pl.BlockSpec((1, d), lambda i: (0, 0))],
        out_specs=pl.BlockSpec((tm, d), lambda i: (i, 0)),
        out_shape=jax.ShapeDtypeStruct((m, d), F32),
        compiler_params=_cparams(("parallel",)),
        name="final_norm",
    )(x, gain.reshape(1, d))


def _relayout_w_in(w_in):
    nl, d, _ = w_in.shape
    wq = w_in[:, :, OFF_Q:OFF_KV].reshape(nl, d, NSA_HEADS, 1, HEAD_DIM) * ((HEAD_DIM ** -0.5) * LOG2E)
    grp = (np.arange(NSA_HEADS) // NSA_REP)[:, None] == np.arange(NSA_KV_HEADS)[None, :]
    wq = jnp.where(jnp.asarray(grp)[None, None, :, :, None], wq, 0.0).reshape(nl, d, NSA_HEADS * 128)
    pad = jnp.zeros((nl, d, TN - 256 - (OFF_MQ - OFF_NG)), w_in.dtype)
    half_a = A_WIDTH
    parts = [w_in[:, :, half_a:OFF_Q], w_in[:, :, OFF_KV:OFF_KV + 512], w_in[:, :, OFF_KV + 512:OFF_MQ], pad,
             w_in[:, :, OFF_BG:], wq, w_in[:, :, :half_a], w_in[:, :, OFF_MQ:OFF_BG]]
    out = jnp.concatenate(parts, axis=2).astype(BF16)
    assert out.shape[2] == Z32_COLS + Z16_COLS
    return out


def _compress_weights(w_phi, pe_cmp):
    nl = w_phi.shape[0]
    w5 = w_phi.reshape(nl, 2, CMP_STRIDE, 2, HEAD_DIM, HEAD_DIM)
    eye = jnp.eye(NSA_KV_HEADS, dtype=w_phi.dtype)
    wc = jnp.einsum("nhlcde,gk->nclgdhke", w5, eye).reshape(nl, 2, CMP_STRIDE * 128, 256).astype(BF16)
    wphi2 = jnp.transpose(w_phi, (0, 2, 1, 3, 4)).reshape(nl, 2, L_CMP * HEAD_DIM, HEAD_DIM)
    wphi2 = jnp.concatenate([wphi2, wphi2], axis=-1)
    pe2 = jnp.transpose(pe_cmp, (0, 2, 1, 3)).reshape(nl, 2, 1, L_CMP * HEAD_DIM)
    pe_rep = jnp.broadcast_to(pe2, (nl, 2, 8, L_CMP * HEAD_DIM))
    return wc, pe_rep, wphi2


def _overlap_t(nj, n_cmp, n_slc):
    ss = np.arange(128)[:, None]
    jj = np.arange(nj)[None, :]
    ov = ((jj * CMP_STRIDE <= (ss + 1) * L_SEL - 1) & (jj * CMP_STRIDE + L_CMP - 1 >= ss * L_SEL)
          & (jj < n_cmp) & (ss < n_slc))
    return jnp.asarray(ov.astype(np.float32)).astype(BF16)


def _block_expand(n_keys):
    e = np.arange(128)[:, None] == (np.arange(n_keys)[None, :] // L_SEL)
    return jnp.asarray(e.astype(np.float32)).astype(BF16)


def _gate_expand():
    e = np.zeros((3, 128, 512), np.float32)
    for t in range(3):
        for h in range(NSA_HEADS):
            e[t, 3 * h + t, h * HEAD_DIM:(h + 1) * HEAD_DIM] = 1.0
    return jnp.asarray(e).astype(BF16)


def _pick_tile(m, pref):
    t = min(m, pref)
    while m % t:
        t //= 2
    return t


def kernel(x_prompt, x_sample, cache_kv, cache_win_kv, cache_mem_kv, page_table, mem_prompt,
           ln1, w_in, g_v, w_s, b_s, w_a, pe_cmp, w_phi, w_b, ln_mem, w_mem_kv, w_c, w_out,
           ln2, w_up, w_down, rel_bias, ln_f):
    bp, t, d = x_prompt.shape
    bs, ts, _ = x_sample.shape
    n_pages = page_table.shape[1]
    past = n_pages * PAGE
    n_mem = mem_prompt.shape[1]
    wb = cache_win_kv.shape[2]
    assert t % (KEY_UNROLL * TQ) == 0 and t % CHUNK == 0 and past % L_SEL == 0 and ts <= L_SEL and CHUNK % ts == 0
    assert (past + ts - L_CMP) // CMP_STRIDE + 1 <= past // CMP_STRIDE
    assert wb == WINDOW and t >= WINDOW

    w_in_p = _relayout_w_in(w_in)
    wc_all, pe_rep_all, wphi2_all = _compress_weights(w_phi, pe_cmp)
    w_a16, w_b16, w_c16, w_out16 = (w.astype(BF16) for w in (w_a, w_b, w_c, w_out))
    w_up16, w_down16, w_mem16 = w_up.astype(BF16), w_down.astype(BF16), w_mem_kv.astype(BF16)
    causal = np.tril(np.ones((CHUNK, CHUNK), bool))
    wsp_p = jnp.where(jnp.asarray(causal), w_s, 0.0)
    blockdiag = np.kron(np.eye(CHUNK // ts), np.ones((ts, ts))) > 0
    wsp_s = jnp.where(jnp.asarray(blockdiag & causal),
                      jnp.tile(w_s[:, :, :ts, :ts], (1, 1, CHUNK // ts, CHUNK // ts)), 0.0)
    bsp_p = jnp.broadcast_to(b_s[:, :, :, None], b_s.shape + (128,))
    bsp_s = jnp.broadcast_to(jnp.tile(b_s[:, :, :ts], (1, 1, CHUNK // ts))[:, :, :, None], b_s.shape + (128,))
    wsp_p, wsp_s = wsp_p.astype(BF16), wsp_s.astype(BF16)

    cache_t = jnp.transpose(cache_kv, (0, 1, 3, 4, 5, 2))
    cwin_t = jnp.transpose(cache_win_kv, (0, 1, 3, 4, 5, 2))

    toep_t, bias_ct, bias_s, bias_n, bias_w, bias_cs = _tables(rel_bias, t, past, ts, wb)
    nj_p, nj_s = t // CMP_STRIDE, past // CMP_STRIDE
    ovl_p = _overlap_t(nj_p, (t - L_CMP) // CMP_STRIDE + 1, -(-t // L_SEL))
    ovl_s = _overlap_t(nj_s, (past + ts - L_CMP) // CMP_STRIDE + 1, -(-(past + ts) // L_SEL))
    ef_p_t = _block_expand(t).T
    ef_s = _block_expand(past)
    eg = _gate_expand()
    pt_prompt = jnp.arange(bp * (t // PAGE), dtype=jnp.int32).reshape(bp, t // PAGE)

    np_tok, ns_tok = bp * t, bs * ts
    tm_p = _pick_tile(np_tok, 1024)
    tm_in = _pick_tile(np_tok, 2048)
    tm_s = _pick_tile(ns_tok, 1024)
    xp = x_prompt.reshape(np_tok, d)
    xs = x_sample.reshape(ns_tok, d)
    mem_flat = mem_prompt.reshape(bp * n_mem, d)
    zeros_gv = jnp.zeros((TN,), F32)

    mem_p, kv_s, v_s = [], [], []
    win_out = jnp.zeros((DEPTH, bs, 2, NSA_KV_HEADS, HEAD_DIM, wb), F32)
    kv_out_p = jnp.zeros((DEPTH, bp, 4, NSA_KV_HEADS, HEAD_DIM, t), F32)
    win_out_p = jnp.zeros((DEPTH, bp, 2, NSA_KV_HEADS, HEAD_DIM, min(WINDOW, t)), F32)
    for l in range(DEPTH):
        mem_kv, _ = _norm_matmul(mem_flat, ln_mem[l], w_mem16[l], zeros_gv, ("none", "none"), 2,
                                 _pick_tile(bp * n_mem, 1024), "mem_kv_proj")
        zp32, zp16 = _norm_matmul(xp, ln1[l], w_in_p[l], g_v[l], TILE_KINDS, N_F32_TILES, tm_in, "in_proj")
        zp32_3, zp16_3 = zp32.reshape(bp, t, Z32_COLS), zp16.reshape(bp, t, Z16_COLS)
        kvc_p = _compress(pt_prompt, zp32.reshape(np_tok // PAGE, PAGE, Z32_COLS), wc_all[l], pe_rep_all[l],
                          wphi2_all[l], col0=COL_PAGED, nr=_pick_tile(bp, 4), layer=None)
        o_nsa, kv_out_p, win_out_p = _nsa_prompt(zp32_3, zp16_3, kvc_p, bias_ct, toep_t, ovl_p, ef_p_t, eg,
                                                 l, DEPTH, kv_out_p, win_out_p)
        o_mem = _mem_attn(zp16_3, mem_kv.reshape(bp, n_mem, 2 * MEM_WIDTH), _pick_tile(t, 512))
        x1 = _merge(xp, zp32, zp16, o_nsa.reshape(np_tok, 512), o_mem.reshape(np_tok, MEM_WIDTH), wsp_p[l],
                    bsp_p[l], w_a16[l], w_b16[l], w_c16[l], w_out16[l], _pick_tile(np_tok, 512))
        xp = _mlp(x1, ln2[l], w_up16[l], w_down16[l], tm_p, 1024)
        mem_p.append(mem_kv.reshape(bp, n_mem, 2, MEM_HEADS, MEM_HEAD_DIM))

        zs32, zs16 = _norm_matmul(xs, ln1[l], w_in_p[l], g_v[l], TILE_KINDS, N_F32_TILES, tm_s, "in_proj")
        zs32_3, zs16_3 = zs32.reshape(bs, ts, Z32_COLS), zs16.reshape(bs, ts, Z16_COLS)
        kvc_s = _compress(page_table, cache_t, wc_all[l], pe_rep_all[l], wphi2_all[l], col0=0,
                          nr=_pick_tile(bs, 4), layer=l)
        o_nsa, win_out = _nsa_sample(page_table, cache_t, cwin_t, l, zs32_3, zs16_3, kvc_s, bias_s, bias_n,
                                     bias_w, bias_cs, ovl_s, ef_s, eg, _pick_tile(bs, 4), win_out)
        o_mem = _mem_attn_cache(zs16_3, cache_mem_kv, l, _pick_tile(bs, 8))
        x1 = _merge(xs, zs32, zs16, o_nsa.reshape(ns_tok, 512), o_mem.reshape(ns_tok, MEM_WIDTH), wsp_s[l],
                    bsp_s[l], w_a16[l], w_b16[l], w_c16[l], w_out16[l], _pick_tile(ns_tok, 256))
        xs = _mlp(x1, ln2[l], w_up16[l], w_down16[l], tm_s, 1024)
        kv_s.append(zs32_3[:, :, COL_PAGED:COL_PAGED + 512].reshape(bs, ts, 4, NSA_KV_HEADS, HEAD_DIM))
        v_s.append(zs32_3[:, :, CB32_V * TN:(CB32_V + 1) * TN])

    y_prompt = _final_norm(xp, ln_f, tm_p).reshape(bp, t, d)
    y_sample = _final_norm(xs, ln_f, tm_s).reshape(bs, ts, d)
    to_rows = lambda a: jnp.transpose(a, (0, 1, 5, 2, 3, 4))
    return (y_prompt, y_sample, to_rows(kv_out_p), to_rows(win_out_p), jnp.stack(mem_p),
            jnp.stack(kv_s), to_rows(win_out), jnp.stack(v_s))
```

```python
import functools
import math

import numpy as np
import jax
import jax.numpy as jnp
from jax import lax
from jax.experimental import pallas as pl
from jax.experimental.pallas import tpu as pltpu

F32 = jnp.float32
BF16 = jnp.bfloat16

EPS = 1e-6
LOG2E = math.log2(math.e)
NEG_INF = -1e30
LOWEST = -3e38

D_MODEL = 1024
DEPTH = 4
PAGE = 128
CHUNK = 128
A_GROUPS = 4
A_WIDTH = 512
NSA_HEADS = 8
NSA_KV_HEADS = 2
NSA_REP = 4
HEAD_DIM = 64
L_CMP = 32
CMP_STRIDE = 16
L_SEL = 64
N_SEL = 8
WINDOW = 256
FORCE_BONUS = 1e3
MEM_HEADS = 4
MEM_HEAD_DIM = 128
MEM_WIDTH = 512
NUM_BUCKETS = 32
MAX_DISTANCE = 128
D_FF = 4096
OFF_Q = 1024
OFF_KV = 1536
OFF_NG = 2304
OFF_MQ = 2328
OFF_BG = 2840

TN = 512
TILE_KINDS = ("gelu_norm", "none", "half_sig",
              "sig", "sig", "sig", "sig", "sig", "sig", "none", "none", "gelu", "none")
N_F32_TILES = 3
Z32_COLS = TN * N_F32_TILES
Z16_COLS = TN * (len(TILE_KINDS) - N_F32_TILES)
CB32_V = 0
CB32_PAGED = 1
CB32_WN = 2
CB16_BG = 0
CB16_Q = 3
CB16_U = 8
CB16_MQ = 9
COL_PAGED = CB32_PAGED * TN
COL_WN = CB32_WN * TN

TQ = 128
KEY_UNROLL = 4
ROW_CHUNK = 256
VMEM_LIMIT = 56 * 1024 * 1024


def _cparams(sem):
    return pltpu.CompilerParams(dimension_semantics=sem, vmem_limit_bytes=VMEM_LIMIT)


def _rms(x, g):
    return x * lax.rsqrt(jnp.mean(x * x, axis=-1, keepdims=True) + EPS) * g


def _dot(a, b):
    return jnp.dot(a, b, preferred_element_type=F32)


def _dot_hi(a, b):
    return jnp.dot(a, b, preferred_element_type=F32, precision=lax.Precision.HIGHEST)


def _dot_nt(a, b):
    return lax.dot_general(a, b, (((1,), (1,)), ((), ())), preferred_element_type=F32)


def _split3(x):
    hi = x.astype(BF16)
    r1 = x - hi.astype(F32)
    mid = r1.astype(BF16)
    lo = (r1 - mid.astype(F32)).astype(BF16)
    return hi, mid, lo


def _bucket(dist):
    n = jnp.maximum(dist, 0)
    max_exact = NUM_BUCKETS // 2
    nf = jnp.maximum(n, 1).astype(F32)
    large = max_exact + (jnp.log(nf / max_exact) / math.log(MAX_DISTANCE / max_exact)
                         * (NUM_BUCKETS - max_exact)).astype(jnp.int32)
    return jnp.where(n < max_exact, n, jnp.minimum(large, NUM_BUCKETS - 1))


def _bias_table(rel_ref, o_ref, dist, valid, lead=(), mult=1.0):
    bucket = _bucket(dist)
    for h in range(NSA_HEADS):
        out = jnp.full(dist.shape, NEG_INF, F32)
        for b in range(NUM_BUCKETS):
            out = jnp.where(bucket == b, rel_ref[b, h], out)
        o_ref[lead + (h,)] = jnp.where(valid, out * mult, NEG_INF)


def _tables_kernel(rel_ref, toep_ref, tpl_ref, bs_ref, bn_ref, bw_ref, bcs_ref, *, past, n_cmp_s):
    def iota(shape, axis):
        return lax.broadcasted_iota(jnp.int32, shape, axis)

    sh = (TQ, TQ)
    d = iota(sh, 1) - iota(sh, 0)
    yes = d > -10 * TQ
    _bias_table(rel_ref, toep_ref, d, d >= 0, (0,), LOG2E)
    _bias_table(rel_ref, toep_ref, d + TQ, yes, (1,), LOG2E)
    _bias_table(rel_ref, toep_ref, d + 2 * TQ, yes, (2,), LOG2E)
    _bias_table(rel_ref, toep_ref, d + 2 * TQ, d + 2 * TQ < WINDOW, (3,), LOG2E)
    _bias_table(rel_ref, toep_ref, d, d > 10 * TQ, (4,), LOG2E)
    nj2 = tpl_ref.shape[1]
    sh = (nj2, TQ)
    dc = iota(sh, 1) - CMP_STRIDE * (iota(sh, 0) - nj2 // 2) - (L_CMP - 1)
    _bias_table(rel_ref, tpl_ref, dc, dc >= 0, (), LOG2E)
    ts = bs_ref.shape[1]
    sh = (ts, past)
    _bias_table(rel_ref, bs_ref, past + iota(sh, 0) - iota(sh, 1), iota(sh, 0) >= 0, (), LOG2E)
    sh = (ts, 128)
    dn = iota(sh, 0) - iota(sh, 1)
    _bias_table(rel_ref, bn_ref, dn, (dn >= 0) & (iota(sh, 1) < ts), (), LOG2E)
    wb = bw_ref.shape[2]
    sh = (ts, wb)
    dw = wb + iota(sh, 0) - iota(sh, 1)
    _bias_table(rel_ref, bw_ref, dw, dw < WINDOW, (), LOG2E)
    njs = bcs_ref.shape[2]
    sh = (ts, njs)
    dcs = past + iota(sh, 0) - CMP_STRIDE * iota(sh, 1) - (L_CMP - 1)
    _bias_table(rel_ref, bcs_ref, dcs, (dcs >= 0) & (iota(sh, 1) < n_cmp_s), (), LOG2E)


def _tables(rel_bias, t, past, ts, wb):
    nj = t // CMP_STRIDE
    njs = past // CMP_STRIDE
    n_cmp_s = (past + ts - L_CMP) // CMP_STRIDE + 1
    h = NSA_HEADS
    shapes = [(5, h, TQ, TQ), (h, 2 * nj, TQ), (h, ts, past), (h, ts, 128), (h, ts, wb), (h, ts, njs)]
    toep_t, tpl, bias_s, bias_n, bias_w, bias_cs = pl.pallas_call(
        functools.partial(_tables_kernel, past=past, n_cmp_s=n_cmp_s),
        in_specs=[pl.BlockSpec(memory_space=pltpu.SMEM)],
        out_shape=[jax.ShapeDtypeStruct(s, F32) for s in shapes],
        name="bias_tables",
    )(rel_bias)
    bias_ct = jnp.concatenate([tpl[:, nj - 8 * qt:2 * nj - 8 * qt, :] for qt in range(t // TQ)], axis=2)
    rows = h * ts
    return (toep_t, bias_ct, bias_s.reshape(rows, past), bias_n.reshape(rows, 128),
            bias_w.reshape(rows, wb), bias_cs.reshape(rows, njs))


def _norm_matmul_kernel(x_ref, g_ref, w_ref, gv_ref, *rest, kinds, n_f32):
    out_refs, hn_ref = rest[:-1], rest[-1]
    n = pl.program_id(1)
    tm = x_ref.shape[0]
    rc = min(ROW_CHUNK, tm)

    @pl.when(n == 0)
    def _():
        hn_ref[...] = _rms(x_ref[...], g_ref[...]).astype(BF16)

    for kind, wide in sorted(set((k, i < n_f32) for i, k in enumerate(kinds))):
        tiles = [i for i, k in enumerate(kinds) if k == kind and (i < n_f32) == wide]
        cond = functools.reduce(jnp.logical_or, [n == i for i in tiles])
        o_ref = out_refs[0] if wide else out_refs[1]

        @pl.when(cond)
        def _(kind=kind, o_ref=o_ref):
            for r in range(tm // rc):
                acc = _dot(hn_ref[r * rc:(r + 1) * rc, :], w_ref[...])
                if kind == "gelu":
                    acc = jax.nn.gelu(acc)
                elif kind == "gelu_norm":
                    acc = _rms(jax.nn.gelu(acc), gv_ref[...])
                elif kind == "sig":
                    acc = jax.nn.sigmoid(acc)
                elif kind == "half_sig":
                    half = acc.shape[1] // 2
                    acc = jnp.concatenate([acc[:, :half], jax.nn.sigmoid(acc[:, half:])], axis=1)
                o_ref[r * rc:(r + 1) * rc, :] = acc.astype(o_ref.dtype)


def _norm_matmul(x, gain, w, gv, kinds, n_f32, tm, name):
    m, d = x.shape
    n_tiles = len(kinds)
    tn = w.shape[1] // n_tiles
    out_specs = [pl.BlockSpec((tm, tn), lambda i, j: (i, jnp.minimum(j, n_f32 - 1)))]
    out_shape = [jax.ShapeDtypeStruct((m, n_f32 * tn), F32)]
    if n_tiles > n_f32:
        out_specs.append(pl.BlockSpec((tm, tn), lambda i, j: (i, jnp.maximum(j - n_f32, 0))))
        out_shape.append(jax.ShapeDtypeStruct((m, (n_tiles - n_f32) * tn), BF16))
    outs = pl.pallas_call(
        functools.partial(_norm_matmul_kernel, kinds=kinds, n_f32=n_f32),
        grid=(m // tm, n_tiles),
        in_specs=[
            pl.BlockSpec((tm, d), lambda i, j: (i, 0)),
            pl.BlockSpec((1, d), lambda i, j: (0, 0)),
            pl.BlockSpec((d, tn), lambda i, j: (0, j)),
            pl.BlockSpec((1, tn), lambda i, j: (0, 0)),
        ],
        out_specs=out_specs,
        out_shape=out_shape,
        scratch_shapes=[pltpu.VMEM((tm, d), BF16)],
        compiler_params=_cparams(("parallel", "arbitrary")),
        name=name,
    )(x, gain.reshape(1, d), w, gv.reshape(1, tn))
    return outs[0], (outs[1] if len(outs) > 1 else None)


def _cmp_copies(pt_ref, src_ref, buf_ref, sem_ref, step, slot, *, nr, n_pages, col0, layer):
    copies = []
    for r in range(nr):
        for p in range(n_pages):
            pid = pt_ref[(step * nr + r) * n_pages + p]
            if layer is None:
                for c in range(2):
                    copies.append(pltpu.make_async_copy(
                        src_ref.at[pid, :, pl.ds(col0 + c * 128, 128)],
                        buf_ref.at[slot, r, c, pl.ds(p * PAGE, PAGE), :],
                        sem_ref.at[slot]))
            else:
                copies.append(pltpu.make_async_copy(
                    src_ref.at[layer, pid, pl.ds(0, 2)],
                    buf_ref.at[slot, r, :, :, :, pl.ds(p * PAGE, PAGE)],
                    sem_ref.at[slot]))
    return copies


def _compress_kernel(pt_ref, src_ref, wc_ref, per_ref, wphi_ref, o_ref, buf_ref, sem_ref, *slab,
                     nr, n_pages, col0, layer):
    step = pl.program_id(0)
    n_steps = pl.num_programs(0)
    slot = lax.rem(step, 2)
    kw = dict(nr=nr, n_pages=n_pages, col0=col0, layer=layer)

    @pl.when(step == 0)
    def _():
        for cp in _cmp_copies(pt_ref, src_ref, buf_ref, sem_ref, step, slot, **kw):
            cp.start()

    @pl.when(step + 1 < n_steps)
    def _():
        for cp in _cmp_copies(pt_ref, src_ref, buf_ref, sem_ref, step + 1, 1 - slot, **kw):
            cp.start()

    for cp in _cmp_copies(pt_ref, src_ref, buf_ref, sem_ref, step, slot, **kw):
        cp.wait()

    rows = n_pages * PAGE
    m = rows // CMP_STRIDE
    if layer is not None:
        slab_ref, = slab
        for r in range(nr):
            for c in range(2):
                for p in range(n_pages):
                    blk = buf_ref[slot, r, c, :, :, pl.ds(p * PAGE, PAGE)].reshape(128, PAGE)
                    slab_ref[r, c, pl.ds(p * PAGE, PAGE), :] = blk.T

    outs = []
    for c in range(2):
        per_req = []
        for r in range(nr):
            if layer is None:
                slabs = [buf_ref[slot, r, c, pl.ds(l, m, stride=CMP_STRIDE), :] for l in range(CMP_STRIDE)]
            else:
                slabs = [slab_ref[r, c, pl.ds(l, m, stride=CMP_STRIDE), :] for l in range(CMP_STRIDE)]
            per_req.append(jnp.concatenate(slabs, axis=1))
        lhs = jnp.concatenate(per_req, axis=0).astype(BF16)
        p = _dot(lhs, wc_ref[c])
        bias = _dot_hi(per_ref[c], wphi_ref[c])[0:1]
        hi = pltpu.roll(p[:, 128:], nr * m - 1, 0)
        outs.append(p[:, :128] + hi + bias)
    res = jnp.concatenate(outs, axis=1)
    o_ref[...] = res.reshape(nr, m, 256)


def _compress(page_table, src, wc, pe_rep, wphi2, *, col0, nr, layer):
    nb, n_pages = page_table.shape
    rows = n_pages * PAGE
    m = rows // CMP_STRIDE
    if layer is None:
        scratch = [pltpu.VMEM((2, nr, 2, rows, 128), F32), pltpu.SemaphoreType.DMA((2,))]
    else:
        scratch = [pltpu.VMEM((2, nr, 2, NSA_KV_HEADS, HEAD_DIM, rows), F32), pltpu.SemaphoreType.DMA((2,)),
                   pltpu.VMEM((nr, 2, rows, 128), F32)]
    grid_spec = pltpu.PrefetchScalarGridSpec(
        num_scalar_prefetch=1,
        grid=(nb // nr,),
        in_specs=[
            pl.BlockSpec(memory_space=pl.ANY),
            pl.BlockSpec((2, CMP_STRIDE * 128, 256), lambda i, pt: (0, 0, 0)),
            pl.BlockSpec((2, 8, L_CMP * HEAD_DIM), lambda i, pt: (0, 0, 0)),
            pl.BlockSpec((2, L_CMP * HEAD_DIM, 128), lambda i, pt: (0, 0, 0)),
        ],
        out_specs=pl.BlockSpec((nr, m, 256), lambda i, pt: (i, 0, 0)),
        scratch_shapes=scratch,
    )
    return pl.pallas_call(
        functools.partial(_compress_kernel, nr=nr, n_pages=n_pages, col0=col0, layer=layer),
        grid_spec=grid_spec,
        out_shape=jax.ShapeDtypeStruct((nb, m, 256), F32),
        compiler_params=_cparams(("arbitrary",)),
        name="compress",
    )(page_table.reshape(-1), src, wc, pe_rep, wphi2)


def _select_blocks_t(imp_t, pos, n_slc):
    ns, c = imp_t.shape
    s_i = lax.broadcasted_iota(jnp.int32, (ns, c), 0)
    cur = lax.shift_right_logical(pos, 6)
    valid = (s_i * L_SEL) <= pos
    forced = (s_i == 0) | (s_i == cur) | (s_i == cur - 1)
    vals = jnp.where(valid, imp_t + jnp.where(forced, FORCE_BONUS, 0.0), -1.0)
    vals = jnp.where(s_i < n_slc, vals, LOWEST)
    rank = jnp.zeros((ns, c), F32)
    for sp in range(n_slc):
        row = vals[sp:sp + 1, :]
        beats = (row > vals) | ((row == vals) & (s_i > sp))
        rank = rank + jnp.where(beats, 1.0, 0.0)
    return jnp.where((rank < min(N_SEL, n_slc)) & (s_i < n_slc), 1.0, 0.0)


def _importance_t(ovl_t, p_sum):
    return functools.reduce(lambda a, b: a + b, [_dot(ovl_t, part) for part in _split3(p_sum)])


def _expand_gates(ng, eg_ref):
    parts = _split3(ng)
    return [functools.reduce(lambda a, b: a + b, [_dot(part, eg_ref[t]) for part in parts]) for t in range(3)]


def _pad_rows(x, rows):
    if x.shape[0] == rows:
        return x
    return jnp.concatenate([x, jnp.zeros((rows - x.shape[0],) + x.shape[1:], x.dtype)], axis=0)


def _nsa_prompt_kernel(q_ref, pg_ref, wn_ref, kvc_ref, bct_ref, toep_ref, ovl_ref, ef_ref, eg_ref, *rest, n_slc):
    o_ref, kvo_ref, wno_ref, ks_ref, vst_ref, kw_ref, vwt_ref, mft_ref = rest[-8:]
    qt = pl.program_id(1)
    t_len = pg_ref.shape[1]
    nj = kvc_ref.shape[1]
    keep0 = t_len - wno_ref.shape[5]

    @pl.when(qt == 0)
    def _():
        ks_ref[...] = pg_ref[0, :, 256:384].astype(BF16)
        kw_ref[...] = wn_ref[0, :, 0:128].astype(BF16)
        for blk in range(t_len // TQ):
            rs = slice(blk * TQ, (blk + 1) * TQ)
            for slot in range(4):
                tile = pg_ref[0, rs, slot * 128:(slot + 1) * 128].T
                kvo_ref[0, 0, slot, :, :, rs] = tile.reshape(NSA_KV_HEADS, HEAD_DIM, TQ)
                if slot == 3:
                    vst_ref[:, rs] = tile.astype(BF16)
            for c in range(2):
                tile = wn_ref[0, rs, c * 128:(c + 1) * 128].T
                if blk * TQ >= keep0:
                    ws = slice(blk * TQ - keep0, (blk + 1) * TQ - keep0)
                    wno_ref[0, 0, c, :, :, ws] = tile.reshape(NSA_KV_HEADS, HEAD_DIM, TQ)
                if c == 1:
                    vwt_ref[:, rs] = tile.astype(BF16)

    qts = [q_ref[0, :, h * 128:(h + 1) * 128].astype(F32).T.astype(BF16) for h in range(NSA_HEADS)]
    groups = [h // NSA_REP for h in range(NSA_HEADS)]

    def score_dots(k_ref, kts):
        starts = [pl.multiple_of(kt * TQ, TQ) for kt in kts]
        ks = [k_ref[pl.ds(st, TQ), :] for st in starts]
        return starts, [[_dot(k, qts[h]) for k in ks] for h in range(NSA_HEADS)]

    all_heads = tuple(range(NSA_HEADS))

    def softmax_update(raw, starts, tbls, masked, carry, heads=all_heads):
        stats, probs = [], []
        for h in heads:
            g = groups[h]
            m_i, l_i, _ = carry[h]
            ss = []
            for s, st, tbl in zip(raw[h], starts, tbls):
                s = s + toep_ref[tbl, h]
                if masked:
                    s = s + mft_ref[pl.ds(st, TQ), g * TQ:(g + 1) * TQ]
                ss.append(s)
            mx = functools.reduce(jnp.maximum, [jnp.max(s, axis=0, keepdims=True) for s in ss])
            m_new = jnp.maximum(m_i, mx)
            alpha = jnp.exp2(m_i - m_new)
            ps = [jnp.exp2(s - m_new) for s in ss]
            l_new = alpha * l_i + functools.reduce(
                lambda a, b: a + b, [jnp.sum(p, axis=0, keepdims=True) for p in ps])
            stats.append((m_new, l_new, alpha))
            probs.append(jnp.concatenate(ps, axis=0).astype(BF16))
        return stats, probs

    def value_dots(vt_ref, starts, stats, probs, carry, heads=all_heads):
        vts = [vt_ref[:, pl.ds(st, TQ)] for st in starts]
        new = []
        for i, h in enumerate(heads):
            g = groups[h]
            m_new, l_new, alpha = stats[i]
            vt = jnp.concatenate([v[g * HEAD_DIM:(g + 1) * HEAD_DIM, :] for v in vts], axis=1)
            new.append((m_new, l_new, alpha * carry[h][2] + _dot(vt, probs[i])))
        return tuple(new)

    init = tuple((jnp.full((1, TQ), NEG_INF, F32), jnp.zeros((1, TQ), F32),
                  jnp.zeros((HEAD_DIM, TQ), F32)) for _ in range(NSA_HEADS))

    def finish(state):
        return [acc * (1.0 / l_f) for (_, l_f, acc) in state]

    w_tiles = WINDOW // TQ + 1
    w_kts = [qt - (w_tiles - 1) + u for u in range(w_tiles)]
    w_tbls = [jnp.where(kt < 0, 4, tb) for kt, tb in zip(w_kts, (3, 1, 0))]
    w_starts, raw_w = score_dots(kw_ref, [jnp.maximum(kt, 0) for kt in w_kts])
    kc = kvc_ref[0, :, 0:128].astype(BF16)
    vct = kvc_ref[0, :, 128:256].T.astype(BF16)
    raw_c = [_dot(kc, qts[h]) for h in range(NSA_HEADS)]
    first_kts = list(range(KEY_UNROLL))
    s_starts0, raw_s0 = score_dots(ks_ref, first_kts)

    stats_w, probs_w = softmax_update(raw_w, w_starts, w_tbls, False, init)
    p_c, p_sum = [], [None] * NSA_KV_HEADS
    for h, g in enumerate(groups):
        s = raw_c[h] + bct_ref[h]
        mx = jnp.max(s, axis=0, keepdims=True)
        e = jnp.exp2(s - mx)
        p = jnp.where(mx > 0.5 * NEG_INF, e * (1.0 / jnp.sum(e, axis=0, keepdims=True)), 0.0)
        p_c.append(p.astype(BF16))
        p_sum[g] = p if p_sum[g] is None else p_sum[g] + p

    o_w = finish(value_dots(vwt_ref, w_starts, stats_w, probs_w, init))
    o_c = [_dot(vct[g * HEAD_DIM:(g + 1) * HEAD_DIM, :], p_c[h]) for h, g in enumerate(groups)]
    imp_t = _importance_t(ovl_ref[...], jnp.concatenate(p_sum, axis=1))
    ns = -(-n_slc // 8) * 8
    pos = qt * TQ + lax.rem(lax.broadcasted_iota(jnp.int32, (1, NSA_KV_HEADS * TQ), 1), TQ)
    sel_t = _select_blocks_t(imp_t[:ns], pos, n_slc)
    hidden = ((_pad_rows(sel_t, 128) - 1.0) * (-NEG_INF)).astype(BF16)
    mft_ref[...] = _dot(ef_ref[...], hidden)

    def sel_body(it, carry):
        kts = [it * KEY_UNROLL + u for u in range(KEY_UNROLL)]
        tbls = [jnp.where(qt - kt < 0, 4, jnp.minimum(qt - kt, 2)) for kt in kts]
        starts, raw = score_dots(ks_ref, kts)
        stats, probs = softmax_update(raw, starts, tbls, True, carry)
        return value_dots(vst_ref, starts, stats, probs, carry)

    tbls0 = [jnp.where(qt - kt < 0, 4, jnp.minimum(qt - kt, 2)) for kt in first_kts]
    stats0, probs0 = softmax_update(raw_s0, s_starts0, tbls0, True, init)
    state0 = value_dots(vst_ref, s_starts0, stats0, probs0, init)
    o_s = finish(lax.fori_loop(1, (qt + KEY_UNROLL) // KEY_UNROLL, sel_body, state0))

    start = pl.multiple_of(qt * TQ, TQ)
    gates = _expand_gates(wn_ref[0, pl.ds(start, TQ), 256:384], eg_ref)
    tiles = []
    for mt in range(NSA_HEADS // 2):
        acc = jnp.zeros((TQ, 128), F32)
        for o, gate in zip((o_c, o_s, o_w), gates):
            pair = jnp.concatenate([o[2 * mt], o[2 * mt + 1]], axis=0)
            acc = acc + gate[:, mt * 128:(mt + 1) * 128] * pair.T
        tiles.append(acc)
    o_ref[0] = jnp.concatenate(tiles, axis=1)


def _nsa_prompt(z32, z16, kvc, bias_ct, toep_t, ovl_t, ef_t, eg, layer, n_layers, kv_out, win_out):
    b, t, _ = z32.shape
    nj = kvc.shape[1]
    n_slc = -(-t // L_SEL)
    keep = min(WINDOW, t)
    kv_shape = (n_layers, b, 4, NSA_KV_HEADS, HEAD_DIM, t)
    win_shape = (n_layers, b, 2, NSA_KV_HEADS, HEAD_DIM, keep)
    args = [z16, z32, z32, kvc, bias_ct, toep_t, ovl_t, ef_t, eg]
    extra_specs, aliases = [], {}
    if kv_out is not None:
        extra_specs = [pl.BlockSpec(memory_space=pl.ANY)] * 2
        aliases = {len(args): 1, len(args) + 1: 2}
        args += [kv_out, win_out]
    return pl.pallas_call(
        functools.partial(_nsa_prompt_kernel, n_slc=n_slc),
        grid=(b, t // TQ),
        in_specs=[
            pl.BlockSpec((1, TQ, 1024), lambda i, j: (i, j, CB16_Q)),
            pl.BlockSpec((1, t, TN), lambda i, j: (i, 0, CB32_PAGED)),
            pl.BlockSpec((1, t, TN), lambda i, j: (i, 0, CB32_WN)),
            pl.BlockSpec((1, nj, 256), lambda i, j: (i, 0, 0)),
            pl.BlockSpec((NSA_HEADS, nj, TQ), lambda i, j: (0, 0, j)),
            pl.BlockSpec((5, NSA_HEADS, TQ, TQ), lambda i, j: (0, 0, 0, 0)),
            pl.BlockSpec((128, nj), lambda i, j: (0, 0)),
            pl.BlockSpec((t, 128), lambda i, j: (0, 0)),
            pl.BlockSpec((3, 128, 512), lambda i, j: (0, 0, 0)),
        ] + extra_specs,
        out_specs=[pl.BlockSpec((1, TQ, 512), lambda i, j: (i, j, 0)),
                   pl.BlockSpec((1, 1) + kv_shape[2:], lambda i, j: (layer, i, 0, 0, 0, 0)),
                   pl.BlockSpec((1, 1) + win_shape[2:], lambda i, j: (layer, i, 0, 0, 0, 0))],
        out_shape=[jax.ShapeDtypeStruct((b, t, 512), F32), jax.ShapeDtypeStruct(kv_shape, F32),
                   jax.ShapeDtypeStruct(win_shape, F32)],
        input_output_aliases=aliases,
        scratch_shapes=[pltpu.VMEM((t, 128), BF16), pltpu.VMEM((128, t), BF16),
                        pltpu.VMEM((t, 128), BF16), pltpu.VMEM((128, t), BF16),
                        pltpu.VMEM((t, NSA_KV_HEADS * TQ), F32)],
        compiler_params=_cparams(("arbitrary", "arbitrary")),
        name="nsa_prompt",
    )(*args)


def _slc_copies(pt_ref, src_ref, buf_ref, sem_ref, step, slot, *, nr, n_pages, layer):
    copies = []
    for r in range(nr):
        for p in range(n_pages):
            pid = pt_ref[(step * nr + r) * n_pages + p]
            copies.append(pltpu.make_async_copy(
                src_ref.at[layer, pid, pl.ds(2, 2)],
                buf_ref.at[slot, r, :, :, :, pl.ds(p * PAGE, PAGE)],
                sem_ref.at[slot]))
    return copies


def _nsa_sample_kernel(pt_ref, src_ref, q_ref, pg_ref, wn_ref, cw_ref, kvc_ref, bs_ref, bn_ref, bw_ref,
                       bcs_ref, ovl_ref, ef_ref, eg_ref, *rest, nr, n_pages, n_slc, pos0, layer):
    o_ref, wo_ref, buf_ref, sem_ref = rest[-4:]
    _nsa_sample_body(pt_ref, src_ref, q_ref, pg_ref, wn_ref, cw_ref, kvc_ref, bs_ref, bn_ref, bw_ref,
                     bcs_ref, ovl_ref, ef_ref, eg_ref, o_ref, wo_ref, buf_ref, sem_ref,
                     nr=nr, n_pages=n_pages, n_slc=n_slc, pos0=pos0, layer=layer)


def _nsa_sample_body(pt_ref, src_ref, q_ref, pg_ref, wn_ref, cw_ref, kvc_ref, bs_ref, bn_ref, bw_ref,
                     bcs_ref, ovl_ref, ef_ref, eg_ref, o_ref, wo_ref, buf_ref, sem_ref,
                     *, nr, n_pages, n_slc, pos0, layer):
    step = pl.program_id(0)
    n_steps = pl.num_programs(0)
    slot = lax.rem(step, 2)
    tq = q_ref.shape[1]
    rows = NSA_HEADS * tq
    nj = kvc_ref.shape[1]
    past = n_pages * PAGE
    wb = cw_ref.shape[5]
    reqs = range(nr)
    kw = dict(nr=nr, n_pages=n_pages, layer=layer)

    @pl.when(step == 0)
    def _():
        for cp in _slc_copies(pt_ref, src_ref, buf_ref, sem_ref, step, slot, **kw):
            cp.start()

    @pl.when(step + 1 < n_steps)
    def _():
        for cp in _slc_copies(pt_ref, src_ref, buf_ref, sem_ref, step + 1, 1 - slot, **kw):
            cp.start()

    for cp in _slc_copies(pt_ref, src_ref, buf_ref, sem_ref, step, slot, **kw):
        cp.wait()

    def new_rows(x):
        return _pad_rows(x, 128).astype(BF16)

    def row_max(*xs):
        return functools.reduce(jnp.maximum, [jnp.max(x, axis=-1, keepdims=True) for x in xs])

    def row_sum(*xs):
        return functools.reduce(lambda a, b: a + b, [jnp.sum(x, axis=-1, keepdims=True) for x in xs])

    qs, pgs, wns, raw = [], [], [], []
    for r in reqs:
        qf = q_ref[r].astype(F32)
        q = jnp.concatenate([qf[:, h * 128:(h + 1) * 128] for h in range(NSA_HEADS)], axis=0).astype(BF16)
        pg, wn = pg_ref[r], wn_ref[r]
        qs.append(q)
        pgs.append(pg)
        wns.append(wn)
        raw.append(dict(
            c=_dot_nt(q, kvc_ref[r, :, 0:128].astype(BF16)),
            w_old=_dot(q, cw_ref[0, r, 0].reshape(128, wb).astype(BF16)),
            w_new=_dot_nt(q, new_rows(wn[:, 0:128])),
            s_old=_dot(q, buf_ref[slot, r, 0].reshape(128, past).astype(BF16)),
            s_new=_dot_nt(q, new_rows(pg[:, 256:384]))))

    p_cs, win = [], []
    for r in reqs:
        s_c = raw[r]["c"] + bcs_ref[...]
        mx = row_max(s_c)
        e_c = jnp.exp2(s_c - mx)
        p_cs.append(jnp.where(mx > 0.5 * NEG_INF, e_c * (1.0 / row_sum(e_c)), 0.0))
        s_o, s_n = raw[r]["w_old"] + bw_ref[...], raw[r]["w_new"] + bn_ref[...]
        m2 = row_max(s_o, s_n)
        e_o, e_n = jnp.exp2(s_o - m2), jnp.exp2(s_n - m2)
        win.append((e_o.astype(BF16), e_n.astype(BF16), 1.0 / row_sum(e_o, e_n)))

    o_cs, o_ws, imps, gates = [], [], [], []
    for r in reqs:
        o_cs.append(_dot(p_cs[r].astype(BF16), kvc_ref[r, :, 128:256].astype(BF16)))
        e_o, e_n, inv = win[r]
        o_ws.append((_dot_nt(e_o, cw_ref[0, r, 1].reshape(128, wb).astype(BF16))
                     + _dot(e_n, new_rows(wns[r][:, 128:256]))) * inv)
        p_g = jnp.sum(p_cs[r].reshape(NSA_KV_HEADS, NSA_REP, tq, nj), axis=1).reshape(NSA_KV_HEADS * tq, nj)
        imps.append(functools.reduce(lambda a, b: a + b,
                                     [_dot_nt(ovl_ref[...], part) for part in _split3(_pad_rows(p_g, 128))]))
        gates.append(_expand_gates(wns[r][:, 256:384], eg_ref))

    ns = -(-n_slc // 8) * 8
    lane = lax.broadcasted_iota(jnp.int32, (1, 128), 1)
    mfs = []
    for r in reqs:
        sel_t = _select_blocks_t(imps[r][:ns], pos0 + lax.rem(lane, tq), n_slc)
        sel = _pad_rows(sel_t, 128).T[:NSA_KV_HEADS * tq]
        mfs.append(_dot(sel.astype(BF16), ef_ref[...]))

    sel_probs = []
    for r in reqs:
        mf = jnp.broadcast_to(mfs[r].reshape(NSA_KV_HEADS, 1, tq, past),
                              (NSA_KV_HEADS, NSA_REP, tq, past)).reshape(rows, past)
        s_o = raw[r]["s_old"] + jnp.where(mf > 0.5, bs_ref[...], NEG_INF)
        s_n = raw[r]["s_new"] + bn_ref[...]
        m2 = row_max(s_o, s_n)
        e_o, e_n = jnp.exp2(s_o - m2), jnp.exp2(s_n - m2)
        sel_probs.append((e_o.astype(BF16), e_n.astype(BF16), 1.0 / row_sum(e_o, e_n)))
    o_ss = []
    for r in reqs:
        e_o, e_n, inv = sel_probs[r]
        o_ss.append((_dot_nt(e_o, buf_ref[slot, r, 1].reshape(128, past).astype(BF16))
                     + _dot(e_n, new_rows(pgs[r][:, 384:512]))) * inv)

    lane_w = lax.broadcasted_iota(jnp.int32, (128, wb), 1)
    for r in reqs:
        for c in range(2):
            kept = pltpu.roll(cw_ref[0, r, c].reshape(128, wb), wb - tq, 1)
            fresh = pltpu.roll(_pad_rows(wns[r][:, c * 128:(c + 1) * 128], 128).T, 128 - tq, 1)
            if wb > 128:
                fresh = jnp.concatenate([jnp.zeros((128, wb - 128), F32), fresh], axis=1)
            wo_ref[0, r, c] = jnp.where(lane_w >= wb - tq, fresh, kept).reshape(NSA_KV_HEADS, HEAD_DIM, wb)

    lane128 = lax.broadcasted_iota(jnp.int32, (tq, 128), 1)
    for r in reqs:
        tiles = []
        for mt in range(NSA_HEADS // 2):
            g = (2 * mt) // NSA_REP
            acc = jnp.zeros((tq, 128), F32)
            for o, gate in zip((o_cs[r], o_ss[r], o_ws[r]), gates[r]):
                a = o[(2 * mt) * tq:(2 * mt + 1) * tq]
                b = o[(2 * mt + 1) * tq:(2 * mt + 2) * tq]
                if g == 1:
                    a = pltpu.roll(a, 64, 1)
                else:
                    b = pltpu.roll(b, 64, 1)
                acc = acc + gate[:, mt * 128:(mt + 1) * 128] * jnp.where(lane128 < 64, a, b)
            tiles.append(acc)
        o_ref[r] = jnp.concatenate(tiles, axis=1)


def _nsa_sample(page_table, cache_t, cwin_t, layer, z32, z16, kvc, bias_s, bias_n, bias_w, bias_cs, ovl_t, ef, eg,
                nr, win_out):
    nb, n_pages = page_table.shape
    past = n_pages * PAGE
    tq = z32.shape[1]
    rows = NSA_HEADS * tq
    nj = kvc.shape[1]
    n_slc = -(-(past + tq) // L_SEL)
    wb = cwin_t.shape[5]
    const2 = lambda i, pt: (0, 0)
    win_block = pl.BlockSpec((1, nr, 2, NSA_KV_HEADS, HEAD_DIM, wb), lambda i, pt: (layer, i, 0, 0, 0, 0))
    args = [page_table.reshape(-1), cache_t, z16, z32, z32, cwin_t, kvc, bias_s, bias_n, bias_w, bias_cs,
            ovl_t, ef, eg]
    extra_specs, aliases = [], {}
    if win_out is not None:
        extra_specs = [pl.BlockSpec(memory_space=pl.ANY)]
        aliases = {len(args): 1}
        args.append(win_out)
    grid_spec = pltpu.PrefetchScalarGridSpec(
        num_scalar_prefetch=1,
        grid=(nb // nr,),
        in_specs=[
            pl.BlockSpec(memory_space=pl.ANY),
            pl.BlockSpec((nr, tq, 1024), lambda i, pt: (i, 0, CB16_Q)),
            pl.BlockSpec((nr, tq, TN), lambda i, pt: (i, 0, CB32_PAGED)),
            pl.BlockSpec((nr, tq, TN), lambda i, pt: (i, 0, CB32_WN)),
            win_block,
            pl.BlockSpec((nr, nj, 256), lambda i, pt: (i, 0, 0)),
            pl.BlockSpec((rows, past), const2),
            pl.BlockSpec((rows, 128), const2),
            pl.BlockSpec((rows, wb), const2),
            pl.BlockSpec((rows, nj), const2),
            pl.BlockSpec((128, nj), const2),
            pl.BlockSpec((128, past), const2),
            pl.BlockSpec((3, 128, 512), lambda i, pt: (0, 0, 0)),
        ] + extra_specs,
        out_specs=[pl.BlockSpec((nr, tq, 512), lambda i, pt: (i, 0, 0)), win_block],
        scratch_shapes=[pltpu.VMEM((2, nr, 2, NSA_KV_HEADS, HEAD_DIM, past), F32),
                        pltpu.SemaphoreType.DMA((2,))],
    )
    return pl.pallas_call(
        functools.partial(_nsa_sample_kernel, nr=nr, n_pages=n_pages, n_slc=n_slc, pos0=past, layer=layer),
        grid_spec=grid_spec,
        out_shape=[jax.ShapeDtypeStruct((nb, tq, 512), F32), jax.ShapeDtypeStruct(cwin_t.shape, F32)],
        input_output_aliases=aliases,
        compiler_params=_cparams(("arbitrary",)),
        name="nsa_sample",
    )(*args)


def _mem_attn_kernel(q_ref, kv_ref, o_ref):
    q = q_ref[0]
    heads = range(MEM_HEADS)
    scores = [_dot_nt(q[:, h * 128:(h + 1) * 128], kv_ref[0, :, h * 128:(h + 1) * 128].astype(BF16))
              for h in heads]
    probs = []
    for s in scores:
        s = s * (MEM_HEAD_DIM ** -0.5)
        e = jnp.exp(s - jnp.max(s, axis=-1, keepdims=True))
        probs.append((e * (1.0 / jnp.sum(e, axis=-1, keepdims=True))).astype(BF16))
    o_ref[0] = jnp.concatenate(
        [_dot(probs[h], kv_ref[0, :, MEM_WIDTH + h * 128:MEM_WIDTH + (h + 1) * 128].astype(BF16))
         for h in heads], axis=1)


def _mem_attn(z16, mem_kv, tq):
    b, t, _ = z16.shape
    n_mem = mem_kv.shape[1]
    return pl.pallas_call(
        _mem_attn_kernel,
        grid=(b, t // tq),
        in_specs=[
            pl.BlockSpec((1, tq, TN), lambda i, j: (i, j, CB16_MQ)),
            pl.BlockSpec((1, n_mem, 2 * MEM_WIDTH), lambda i, j: (i, 0, 0)),
        ],
        out_specs=pl.BlockSpec((1, tq, MEM_WIDTH), lambda i, j: (i, j, 0)),
        out_shape=jax.ShapeDtypeStruct((b, t, MEM_WIDTH), F32),
        compiler_params=_cparams(("parallel", "parallel")),
        name="mem_attn",
    )(z16, mem_kv)


def _mem_attn_cache_kernel(q_ref, kv_ref, o_ref):
    nr, tq, _ = q_ref.shape
    n_mem = kv_ref.shape[2] // (2 * MEM_HEADS)
    pairs = [(r, h) for r in range(nr) for h in range(MEM_HEADS)]
    qs = [q_ref[r].astype(F32) for r in range(nr)]
    scores = []
    for r, h in pairs:
        kh = kv_ref[0, r, pl.ds(h, n_mem, stride=2 * MEM_HEADS), :].astype(BF16)
        scores.append(_dot_nt(qs[r][:, h * 128:(h + 1) * 128].astype(BF16), kh))
    s = jnp.concatenate(scores, axis=0) * (MEM_HEAD_DIM ** -0.5)
    mx = jnp.max(s, axis=-1, keepdims=True)
    e = jnp.exp(s - mx)
    p = e * (1.0 / jnp.sum(e, axis=-1, keepdims=True))
    outs = []
    for i, (r, h) in enumerate(pairs):
        vh = kv_ref[0, r, pl.ds(MEM_HEADS + h, n_mem, stride=2 * MEM_HEADS), :].astype(BF16)
        outs.append(_dot(p[i * tq:(i + 1) * tq].astype(BF16), vh))
    for r in range(nr):
        o_ref[r] = jnp.concatenate(outs[r * MEM_HEADS:(r + 1) * MEM_HEADS], axis=1)


def _mem_attn_cache(z16, cache_mem, layer, nr):
    b, tq, _ = z16.shape
    nl, _, n_mem = cache_mem.shape[:3]
    rows = n_mem * 2 * MEM_HEADS
    return pl.pallas_call(
        _mem_attn_cache_kernel,
        grid=(b // nr,),
        in_specs=[
            pl.BlockSpec((nr, tq, TN), lambda i: (i, 0, CB16_MQ)),
            pl.BlockSpec((1, nr, rows, MEM_HEAD_DIM), lambda i: (layer, i, 0, 0)),
        ],
        out_specs=pl.BlockSpec((nr, tq, MEM_WIDTH), lambda i: (i, 0, 0)),
        out_shape=jax.ShapeDtypeStruct((b, tq, MEM_WIDTH), F32),
        compiler_params=_cparams(("parallel",)),
        name="mem_attn_cache",
    )(z16, cache_mem.reshape(nl, b, rows, MEM_HEAD_DIM))


def _merge_kernel(x_ref, u_ref, v_ref, g0_ref, g1_ref, g2_ref, on_ref, om_ref, wsp_ref, bsp_ref,
                  wa_ref, wb_ref, wc_ref, wo_ref, o_ref):
    tm = x_ref.shape[0]
    u = u_ref[...].astype(F32)
    v = v_ref[...].astype(BF16)
    chunks = []
    for c in range(tm // CHUNK):
        parts = []
        for g in range(A_GROUPS):
            vg = v[c * CHUNK:(c + 1) * CHUNK, g * 128:(g + 1) * 128]
            parts.append(_dot(wsp_ref[g], vg) + bsp_ref[g])
        chunks.append(jnp.concatenate(parts, axis=1))
    ya = (u * jnp.concatenate(chunks, axis=0)).astype(BF16)
    merged = (g0_ref[...].astype(F32) * _dot(ya, wa_ref[...])
              + g1_ref[...].astype(F32) * _dot(on_ref[...].astype(BF16), wb_ref[...])
              + g2_ref[...].astype(F32) * _dot(om_ref[...].astype(BF16), wc_ref[...]))
    o_ref[...] = x_ref[...] + _dot(merged.astype(BF16), wo_ref[...])


def _merge(x, z32, z16, o_nsa, o_mem, wsp, bsp, w_a, w_b, w_c, w_out, tm):
    m, d = x.shape
    row = lambda i: (i, 0)
    c2 = lambda i: (0, 0)
    c3 = lambda i: (0, 0, 0)
    return pl.pallas_call(
        _merge_kernel,
        grid=(m // tm,),
        in_specs=[
            pl.BlockSpec((tm, d), row),
            pl.BlockSpec((tm, TN), lambda i: (i, CB16_U)),
            pl.BlockSpec((tm, TN), lambda i: (i, CB32_V)),
            pl.BlockSpec((tm, d), lambda i: (i, CB16_BG)),
            pl.BlockSpec((tm, d), lambda i: (i, CB16_BG + 1)),
            pl.BlockSpec((tm, d), lambda i: (i, CB16_BG + 2)),
            pl.BlockSpec((tm, 512), row),
            pl.BlockSpec((tm, 512), row),
            pl.BlockSpec((A_GROUPS, CHUNK, CHUNK), c3),
            pl.BlockSpec((A_GROUPS, CHUNK, 128), c3),
            pl.BlockSpec((A_WIDTH, d), c2),
            pl.BlockSpec((512, d), c2),
            pl.BlockSpec((MEM_WIDTH, d), c2),
            pl.BlockSpec((d, d), c2),
        ],
        out_specs=pl.BlockSpec((tm, d), row),
        out_shape=jax.ShapeDtypeStruct((m, d), F32),
        compiler_params=_cparams(("parallel",)),
        name="merge",
    )(x, z16, z32, z16, z16, z16, o_nsa, o_mem, wsp, bsp, w_a, w_b, w_c, w_out)


def _mlp_kernel(x_ref, g_ref, wu_ref, wd_ref, gf_ref, o_ref, hn_ref, acc_ref, *, final):
    f = pl.program_id(1)

    @pl.when(f == 0)
    def _():
        hn_ref[...] = _rms(x_ref[...], g_ref[...]).astype(BF16)
        acc_ref[...] = jnp.zeros_like(acc_ref)

    h = jnp.square(jnp.maximum(_dot(hn_ref[...], wu_ref[...]), 0.0))
    acc_ref[...] += _dot(h.astype(BF16), wd_ref[...])

    @pl.when(f == pl.num_programs(1) - 1)
    def _():
        y = x_ref[...] + acc_ref[...]
        o_ref[...] = _rms(y, gf_ref[...]) if final else y


def _mlp(x, gain, w_up, w_down, out_gain, final, tm, tf):
    m, d = x.shape
    ff = w_up.shape[1]
    return pl.pallas_call(
        functools.partial(_mlp_kernel, final=final),
        grid=(m // tm, ff // tf),
        in_specs=[
            pl.BlockSpec((tm, d), lambda i, j: (i, 0)),
            pl.BlockSpec((1, d), lambda i, j: (0, 0)),
            pl.BlockSpec((d, tf), lambda i, j: (0, j)),
            pl.BlockSpec((tf, d), lambda i, j: (j, 0)),
            pl.BlockSpec((1, d), lambda i, j: (0, 0)),
        ],
        out_specs=pl.BlockSpec((tm, d), lambda i, j: (i, 0)),
        out_shape=jax.ShapeDtypeStruct((m, d), F32),
        scratch_shapes=[pltpu.VMEM((tm, d), BF16), pltpu.VMEM((tm, d), F32)],
        compiler_params=_cparams(("parallel", "arbitrary")),
        name="mlp",
    )(x, gain.reshape(1, d), w_up, w_down, out_gain.reshape(1, d))


def _relayout_w_in(w_in):
    nl, d, _ = w_in.shape
    wq = w_in[:, :, OFF_Q:OFF_KV].reshape(nl, d, NSA_HEADS, 1, HEAD_DIM) * ((HEAD_DIM ** -0.5) * LOG2E)
    grp = (np.arange(NSA_HEADS) // NSA_REP)[:, None] == np.arange(NSA_KV_HEADS)[None, :]
    wq = jnp.where(jnp.asarray(grp)[None, None, :, :, None], wq, 0.0).reshape(nl, d, NSA_HEADS * 128)
    pad = jnp.zeros((nl, d, TN - 256 - (OFF_MQ - OFF_NG)), w_in.dtype)
    half_a = A_WIDTH
    parts = [w_in[:, :, half_a:OFF_Q], w_in[:, :, OFF_KV:OFF_KV + 512], w_in[:, :, OFF_KV + 512:OFF_MQ], pad,
             w_in[:, :, OFF_BG:], wq, w_in[:, :, :half_a], w_in[:, :, OFF_MQ:OFF_BG]]
    out = jnp.concatenate(parts, axis=2).astype(BF16)
    assert out.shape[2] == Z32_COLS + Z16_COLS
    return out


def _compress_weights(w_phi, pe_cmp):
    nl = w_phi.shape[0]
    w5 = w_phi.reshape(nl, 2, CMP_STRIDE, 2, HEAD_DIM, HEAD_DIM)
    eye = jnp.eye(NSA_KV_HEADS, dtype=w_phi.dtype)
    wc = jnp.einsum("nhlcde,gk->nclgdhke", w5, eye).reshape(nl, 2, CMP_STRIDE * 128, 256).astype(BF16)
    wphi2 = jnp.transpose(w_phi, (0, 2, 1, 3, 4)).reshape(nl, 2, L_CMP * HEAD_DIM, HEAD_DIM)
    wphi2 = jnp.concatenate([wphi2, wphi2], axis=-1)
    pe2 = jnp.transpose(pe_cmp, (0, 2, 1, 3)).reshape(nl, 2, 1, L_CMP * HEAD_DIM)
    pe_rep = jnp.broadcast_to(pe2, (nl, 2, 8, L_CMP * HEAD_DIM))
    return wc, pe_rep, wphi2


def _overlap_t(nj, n_cmp, n_slc):
    ss = np.arange(128)[:, None]
    jj = np.arange(nj)[None, :]
    ov = ((jj * CMP_STRIDE <= (ss + 1) * L_SEL - 1) & (jj * CMP_STRIDE + L_CMP - 1 >= ss * L_SEL)
          & (jj < n_cmp) & (ss < n_slc))
    return jnp.asarray(ov.astype(np.float32)).astype(BF16)


def _block_expand(n_keys):
    e = np.arange(128)[:, None] == (np.arange(n_keys)[None, :] // L_SEL)
    return jnp.asarray(e.astype(np.float32)).astype(BF16)


def _gate_expand():
    e = np.zeros((3, 128, 512), np.float32)
    for t in range(3):
        for h in range(NSA_HEADS):
            e[t, 3 * h + t, h * HEAD_DIM:(h + 1) * HEAD_DIM] = 1.0
    return jnp.asarray(e).astype(BF16)


def _pick_tile(m, pref):
    t = min(m, pref)
    while m % t:
        t //= 2
    return t


def kernel(x_prompt, x_sample, cache_kv, cache_win_kv, cache_mem_kv, page_table, mem_prompt,
           ln1, w_in, g_v, w_s, b_s, w_a, pe_cmp, w_phi, w_b, ln_mem, w_mem_kv, w_c, w_out,
           ln2, w_up, w_down, rel_bias, ln_f):
    bp, t, d = x_prompt.shape
    bs, ts, _ = x_sample.shape
    n_pages = page_table.shape[1]
    past = n_pages * PAGE
    n_mem = mem_prompt.shape[1]
    wb = cache_win_kv.shape[2]
    assert t % (KEY_UNROLL * TQ) == 0 and t % CHUNK == 0 and past % L_SEL == 0 and ts <= L_SEL and CHUNK % ts == 0
    assert (past + ts - L_CMP) // CMP_STRIDE + 1 <= past // CMP_STRIDE
    assert wb == WINDOW and t >= WINDOW

    w_in_p = _relayout_w_in(w_in)
    wc_all, pe_rep_all, wphi2_all = _compress_weights(w_phi, pe_cmp)
    w_a16, w_b16, w_c16, w_out16 = (w.astype(BF16) for w in (w_a, w_b, w_c, w_out))
    w_up16, w_down16, w_mem16 = w_up.astype(BF16), w_down.astype(BF16), w_mem_kv.astype(BF16)
    causal = np.tril(np.ones((CHUNK, CHUNK), bool))
    wsp_p = jnp.where(jnp.asarray(causal), w_s, 0.0)
    blockdiag = np.kron(np.eye(CHUNK // ts), np.ones((ts, ts))) > 0
    wsp_s = jnp.where(jnp.asarray(blockdiag & causal),
                      jnp.tile(w_s[:, :, :ts, :ts], (1, 1, CHUNK // ts, CHUNK // ts)), 0.0)
    bsp_p = jnp.broadcast_to(b_s[:, :, :, None], b_s.shape + (128,))
    bsp_s = jnp.broadcast_to(jnp.tile(b_s[:, :, :ts], (1, 1, CHUNK // ts))[:, :, :, None], b_s.shape + (128,))
    wsp_p, wsp_s = wsp_p.astype(BF16), wsp_s.astype(BF16)

    cache_t = jnp.transpose(cache_kv, (0, 1, 3, 4, 5, 2))
    cwin_t = jnp.transpose(cache_win_kv, (0, 1, 3, 4, 5, 2))

    toep_t, bias_ct, bias_s, bias_n, bias_w, bias_cs = _tables(rel_bias, t, past, ts, wb)
    nj_p, nj_s = t // CMP_STRIDE, past // CMP_STRIDE
    ovl_p = _overlap_t(nj_p, (t - L_CMP) // CMP_STRIDE + 1, -(-t // L_SEL))
    ovl_s = _overlap_t(nj_s, (past + ts - L_CMP) // CMP_STRIDE + 1, -(-(past + ts) // L_SEL))
    ef_p_t = _block_expand(t).T
    ef_s = _block_expand(past)
    eg = _gate_expand()
    pt_prompt = jnp.arange(bp * (t // PAGE), dtype=jnp.int32).reshape(bp, t // PAGE)

    np_tok, ns_tok = bp * t, bs * ts
    tm_p = _pick_tile(np_tok, 1024)
    tm_in = _pick_tile(np_tok, 2048)
    tm_s = _pick_tile(ns_tok, 1024)
    xp = x_prompt.reshape(np_tok, d)
    xs = x_sample.reshape(ns_tok, d)
    mem_flat = mem_prompt.reshape(bp * n_mem, d)
    zeros_gv = jnp.zeros((TN,), F32)

    mem_p, kv_s, v_s = [], [], []
    win_out = jnp.zeros((DEPTH, bs, 2, NSA_KV_HEADS, HEAD_DIM, wb), F32)
    kv_out_p = jnp.zeros((DEPTH, bp, 4, NSA_KV_HEADS, HEAD_DIM, t), F32)
    win_out_p = jnp.zeros((DEPTH, bp, 2, NSA_KV_HEADS, HEAD_DIM, min(WINDOW, t)), F32)
    for l in range(DEPTH):
        mem_kv, _ = _norm_matmul(mem_flat, ln_mem[l], w_mem16[l], zeros_gv, ("none", "none"), 2,
                                 _pick_tile(bp * n_mem, 1024), "mem_kv_proj")
        zp32, zp16 = _norm_matmul(xp, ln1[l], w_in_p[l], g_v[l], TILE_KINDS, N_F32_TILES, tm_in, "in_proj")
        zp32_3, zp16_3 = zp32.reshape(bp, t, Z32_COLS), zp16.reshape(bp, t, Z16_COLS)
        kvc_p = _compress(pt_prompt, zp32.reshape(np_tok // PAGE, PAGE, Z32_COLS), wc_all[l], pe_rep_all[l],
                          wphi2_all[l], col0=COL_PAGED, nr=_pick_tile(bp, 4), layer=None)
        o_nsa, kv_out_p, win_out_p = _nsa_prompt(zp32_3, zp16_3, kvc_p, bias_ct, toep_t, ovl_p, ef_p_t, eg,
                                                 l, DEPTH, kv_out_p, win_out_p)
        o_mem = _mem_attn(zp16_3, mem_kv.reshape(bp, n_mem, 2 * MEM_WIDTH), _pick_tile(t, 512))
        x1 = _merge(xp, zp32, zp16, o_nsa.reshape(np_tok, 512), o_mem.reshape(np_tok, MEM_WIDTH), wsp_p[l],
                    bsp_p[l], w_a16[l], w_b16[l], w_c16[l], w_out16[l], _pick_tile(np_tok, 512))
        xp = _mlp(x1, ln2[l], w_up16[l], w_down16[l], ln_f, l == DEPTH - 1, tm_p, 1024)
        mem_p.append(mem_kv.reshape(bp, n_mem, 2, MEM_HEADS, MEM_HEAD_DIM))

        zs32, zs16 = _norm_matmul(xs, ln1[l], w_in_p[l], g_v[l], TILE_KINDS, N_F32_TILES, tm_s, "in_proj")
        zs32_3, zs16_3 = zs32.reshape(bs, ts, Z32_COLS), zs16.reshape(bs, ts, Z16_COLS)
        kvc_s = _compress(page_table, cache_t, wc_all[l], pe_rep_all[l], wphi2_all[l], col0=0,
                          nr=_pick_tile(bs, 4), layer=l)
        o_nsa, win_out = _nsa_sample(page_table, cache_t, cwin_t, l, zs32_3, zs16_3, kvc_s, bias_s, bias_n,
                                     bias_w, bias_cs, ovl_s, ef_s, eg, _pick_tile(bs, 4), win_out)
        o_mem = _mem_attn_cache(zs16_3, cache_mem_kv, l, _pick_tile(bs, 8))
        x1 = _merge(xs, zs32, zs16, o_nsa.reshape(ns_tok, 512), o_mem.reshape(ns_tok, MEM_WIDTH), wsp_s[l],
                    bsp_s[l], w_a16[l], w_b16[l], w_c16[l], w_out16[l], _pick_tile(ns_tok, 256))
        xs = _mlp(x1, ln2[l], w_up16[l], w_down16[l], ln_f, l == DEPTH - 1, tm_s, 1024)
        kv_s.append(zs32_3[:, :, COL_PAGED:COL_PAGED + 512].reshape(bs, ts, 4, NSA_KV_HEADS, HEAD_DIM))
        v_s.append(zs32_3[:, :, CB32_V * TN:(CB32_V + 1) * TN])

    y_prompt = xp.reshape(bp, t, d)
    y_sample = xs.reshape(bs, ts, d)
    to_rows = lambda a: jnp.transpose(a, (0, 1, 5, 2, 3, 4))
    return (y_prompt, y_sample, to_rows(kv_out_p), to_rows(win_out_p), jnp.stack(mem_p),
            jnp.stack(kv_s), to_rows(win_out), jnp.stack(v_s))
```

```python
import functools
import math

import numpy as np
import jax
import jax.numpy as jnp
from jax import lax
from jax.experimental import pallas as pl
from jax.experimental.pallas import tpu as pltpu

F32 = jnp.float32
BF16 = jnp.bfloat16

EPS = 1e-6
LOG2E = math.log2(math.e)
NEG_INF = -1e30
LOWEST = -3e38

D_MODEL = 1024
DEPTH = 4
PAGE = 128
CHUNK = 128
A_GROUPS = 4
A_WIDTH = 512
NSA_HEADS = 8
NSA_KV_HEADS = 2
NSA_REP = 4
HEAD_DIM = 64
L_CMP = 32
CMP_STRIDE = 16
L_SEL = 64
N_SEL = 8
WINDOW = 256
FORCE_BONUS = 1e3
MEM_HEADS = 4
MEM_HEAD_DIM = 128
MEM_WIDTH = 512
NUM_BUCKETS = 32
MAX_DISTANCE = 128
D_FF = 4096
OFF_Q = 1024
OFF_KV = 1536
OFF_NG = 2304
OFF_MQ = 2328
OFF_BG = 2840

TN = 512
TILE_KINDS = ("gelu_norm", "none", "half_sig",
              "sig", "sig", "sig", "sig", "sig", "sig", "none", "none", "gelu", "none")
N_F32_TILES = 3
Z32_COLS = TN * N_F32_TILES
Z16_COLS = TN * (len(TILE_KINDS) - N_F32_TILES)
CB32_V = 0
CB32_PAGED = 1
CB32_WN = 2
CB16_BG = 0
CB16_Q = 3
CB16_U = 8
CB16_MQ = 9
COL_PAGED = CB32_PAGED * TN
COL_WN = CB32_WN * TN

TQ = 128
KEY_UNROLL = 4
ROW_CHUNK = 256
VMEM_LIMIT = 56 * 1024 * 1024


def _cparams(sem):
    return pltpu.CompilerParams(dimension_semantics=sem, vmem_limit_bytes=VMEM_LIMIT)


def _rms(x, g):
    return x * lax.rsqrt(jnp.mean(x * x, axis=-1, keepdims=True) + EPS) * g


def _dot(a, b):
    return jnp.dot(a, b, preferred_element_type=F32)


def _dot_hi(a, b):
    return jnp.dot(a, b, preferred_element_type=F32, precision=lax.Precision.HIGHEST)


def _dot_nt(a, b):
    return lax.dot_general(a, b, (((1,), (1,)), ((), ())), preferred_element_type=F32)


def _split3(x):
    hi = x.astype(BF16)
    r1 = x - hi.astype(F32)
    mid = r1.astype(BF16)
    lo = (r1 - mid.astype(F32)).astype(BF16)
    return hi, mid, lo


def _bucket(dist):
    n = jnp.maximum(dist, 0)
    max_exact = NUM_BUCKETS // 2
    nf = jnp.maximum(n, 1).astype(F32)
    large = max_exact + (jnp.log(nf / max_exact) / math.log(MAX_DISTANCE / max_exact)
                         * (NUM_BUCKETS - max_exact)).astype(jnp.int32)
    return jnp.where(n < max_exact, n, jnp.minimum(large, NUM_BUCKETS - 1))


def _bias_table(rel_ref, o_ref, dist, valid, lead=(), mult=1.0):
    bucket = _bucket(dist)
    for h in range(NSA_HEADS):
        out = jnp.full(dist.shape, NEG_INF, F32)
        for b in range(NUM_BUCKETS):
            out = jnp.where(bucket == b, rel_ref[b, h], out)
        o_ref[lead + (h,)] = jnp.where(valid, out * mult, NEG_INF)


def _tables_kernel(rel_ref, toep_ref, tpl_ref, bs_ref, bn_ref, bw_ref, bcs_ref, *, past, n_cmp_s):
    def iota(shape, axis):
        return lax.broadcasted_iota(jnp.int32, shape, axis)

    sh = (TQ, TQ)
    d = iota(sh, 1) - iota(sh, 0)
    yes = d > -10 * TQ
    _bias_table(rel_ref, toep_ref, d, d >= 0, (0,), LOG2E)
    _bias_table(rel_ref, toep_ref, d + TQ, yes, (1,), LOG2E)
    _bias_table(rel_ref, toep_ref, d + 2 * TQ, yes, (2,), LOG2E)
    _bias_table(rel_ref, toep_ref, d + 2 * TQ, d + 2 * TQ < WINDOW, (3,), LOG2E)
    _bias_table(rel_ref, toep_ref, d, d > 10 * TQ, (4,), LOG2E)
    nj2 = tpl_ref.shape[1]
    sh = (nj2, TQ)
    dc = iota(sh, 1) - CMP_STRIDE * (iota(sh, 0) - nj2 // 2) - (L_CMP - 1)
    _bias_table(rel_ref, tpl_ref, dc, dc >= 0, (), LOG2E)
    ts = bs_ref.shape[1]
    sh = (ts, past)
    _bias_table(rel_ref, bs_ref, past + iota(sh, 0) - iota(sh, 1), iota(sh, 0) >= 0, (), LOG2E)
    sh = (ts, 128)
    dn = iota(sh, 0) - iota(sh, 1)
    _bias_table(rel_ref, bn_ref, dn, (dn >= 0) & (iota(sh, 1) < ts), (), LOG2E)
    wb = bw_ref.shape[2]
    sh = (ts, wb)
    dw = wb + iota(sh, 0) - iota(sh, 1)
    _bias_table(rel_ref, bw_ref, dw, dw < WINDOW, (), LOG2E)
    njs = bcs_ref.shape[2]
    sh = (ts, njs)
    dcs = past + iota(sh, 0) - CMP_STRIDE * iota(sh, 1) - (L_CMP - 1)
    _bias_table(rel_ref, bcs_ref, dcs, (dcs >= 0) & (iota(sh, 1) < n_cmp_s), (), LOG2E)


def _tables(rel_bias, t, past, ts, wb):
    nj = t // CMP_STRIDE
    njs = past // CMP_STRIDE
    n_cmp_s = (past + ts - L_CMP) // CMP_STRIDE + 1
    h = NSA_HEADS
    shapes = [(5, h, TQ, TQ), (h, 2 * nj, TQ), (h, ts, past), (h, ts, 128), (h, ts, wb), (h, ts, njs)]
    toep_t, tpl, bias_s, bias_n, bias_w, bias_cs = pl.pallas_call(
        functools.partial(_tables_kernel, past=past, n_cmp_s=n_cmp_s),
        in_specs=[pl.BlockSpec(memory_space=pltpu.SMEM)],
        out_shape=[jax.ShapeDtypeStruct(s, F32) for s in shapes],
        name="bias_tables",
    )(rel_bias)
    bias_ct = jnp.concatenate([tpl[:, nj - 8 * qt:2 * nj - 8 * qt, :] for qt in range(t // TQ)], axis=2)
    rows = h * ts
    return (toep_t, bias_ct, bias_s.reshape(rows, past), bias_n.reshape(rows, 128),
            bias_w.reshape(rows, wb), bias_cs.reshape(rows, njs))


def _norm_matmul_kernel(x_ref, g_ref, w_ref, gv_ref, *rest, kinds, n_f32, w_t):
    out_refs, hn_ref = rest[:-1], rest[-1]
    n = pl.program_id(1)
    tm = x_ref.shape[0]
    rc = min(ROW_CHUNK, tm)

    @pl.when(n == 0)
    def _():
        hn_ref[...] = _rms(x_ref[...], g_ref[...]).astype(BF16)

    for kind, wide in sorted(set((k, i < n_f32) for i, k in enumerate(kinds))):
        tiles = [i for i, k in enumerate(kinds) if k == kind and (i < n_f32) == wide]
        cond = functools.reduce(jnp.logical_or, [n == i for i in tiles])
        o_ref = out_refs[0] if wide else out_refs[1]

        @pl.when(cond)
        def _(kind=kind, o_ref=o_ref):
            for r in range(tm // rc):
                acc = (_dot_nt if w_t else _dot)(hn_ref[r * rc:(r + 1) * rc, :], w_ref[...])
                if kind == "gelu":
                    acc = jax.nn.gelu(acc)
                elif kind == "gelu_norm":
                    acc = _rms(jax.nn.gelu(acc), gv_ref[...])
                elif kind == "sig":
                    acc = jax.nn.sigmoid(acc)
                elif kind == "half_sig":
                    half = acc.shape[1] // 2
                    acc = jnp.concatenate([acc[:, :half], jax.nn.sigmoid(acc[:, half:])], axis=1)
                o_ref[r * rc:(r + 1) * rc, :] = acc.astype(o_ref.dtype)


def _norm_matmul(x, gain, w, gv, kinds, n_f32, tm, name, w_t=False):
    m, d = x.shape
    n_tiles = len(kinds)
    n_out = w.shape[0] if w_t else w.shape[1]
    tn = n_out // n_tiles
    w_spec = pl.BlockSpec((tn, d), lambda i, j: (j, 0)) if w_t else pl.BlockSpec((d, tn), lambda i, j: (0, j))
    out_specs = [pl.BlockSpec((tm, tn), lambda i, j: (i, jnp.minimum(j, n_f32 - 1)))]
    out_shape = [jax.ShapeDtypeStruct((m, n_f32 * tn), F32)]
    if n_tiles > n_f32:
        out_specs.append(pl.BlockSpec((tm, tn), lambda i, j: (i, jnp.maximum(j - n_f32, 0))))
        out_shape.append(jax.ShapeDtypeStruct((m, (n_tiles - n_f32) * tn), BF16))
    outs = pl.pallas_call(
        functools.partial(_norm_matmul_kernel, kinds=kinds, n_f32=n_f32, w_t=w_t),
        grid=(m // tm, n_tiles),
        in_specs=[
            pl.BlockSpec((tm, d), lambda i, j: (i, 0)),
            pl.BlockSpec((1, d), lambda i, j: (0, 0)),
            w_spec,
            pl.BlockSpec((1, tn), lambda i, j: (0, 0)),
        ],
        out_specs=out_specs,
        out_shape=out_shape,
        scratch_shapes=[pltpu.VMEM((tm, d), BF16)],
        compiler_params=_cparams(("parallel", "arbitrary")),
        name=name,
    )(x, gain.reshape(1, d), w, gv.reshape(1, tn))
    return outs[0], (outs[1] if len(outs) > 1 else None)


def _cmp_copies(pt_ref, src_ref, buf_ref, sem_ref, step, slot, *, nr, n_pages, col0, layer):
    copies = []
    for r in range(nr):
        for p in range(n_pages):
            pid = pt_ref[(step * nr + r) * n_pages + p]
            if layer is None:
                for c in range(2):
                    copies.append(pltpu.make_async_copy(
                        src_ref.at[pid, :, pl.ds(col0 + c * 128, 128)],
                        buf_ref.at[slot, r, c, pl.ds(p * PAGE, PAGE), :],
                        sem_ref.at[slot]))
            else:
                copies.append(pltpu.make_async_copy(
                    src_ref.at[layer, pid, pl.ds(0, 2)],
                    buf_ref.at[slot, r, :, :, :, pl.ds(p * PAGE, PAGE)],
                    sem_ref.at[slot]))
    return copies


def _compress_kernel(pt_ref, src_ref, wc_ref, per_ref, wphi_ref, o_ref, buf_ref, sem_ref, *slab,
                     nr, n_pages, col0, layer):
    step = pl.program_id(0)
    n_steps = pl.num_programs(0)
    slot = lax.rem(step, 2)
    kw = dict(nr=nr, n_pages=n_pages, col0=col0, layer=layer)

    @pl.when(step == 0)
    def _():
        for cp in _cmp_copies(pt_ref, src_ref, buf_ref, sem_ref, step, slot, **kw):
            cp.start()

    @pl.when(step + 1 < n_steps)
    def _():
        for cp in _cmp_copies(pt_ref, src_ref, buf_ref, sem_ref, step + 1, 1 - slot, **kw):
            cp.start()

    for cp in _cmp_copies(pt_ref, src_ref, buf_ref, sem_ref, step, slot, **kw):
        cp.wait()

    rows = n_pages * PAGE
    m = rows // CMP_STRIDE
    if layer is not None:
        slab_ref, = slab
        for r in range(nr):
            for c in range(2):
                for p in range(n_pages):
                    blk = buf_ref[slot, r, c, :, :, pl.ds(p * PAGE, PAGE)].reshape(128, PAGE)
                    slab_ref[r, c, pl.ds(p * PAGE, PAGE), :] = blk.T

    outs = []
    for c in range(2):
        per_req = []
        for r in range(nr):
            if layer is None:
                slabs = [buf_ref[slot, r, c, pl.ds(l, m, stride=CMP_STRIDE), :] for l in range(CMP_STRIDE)]
            else:
                slabs = [slab_ref[r, c, pl.ds(l, m, stride=CMP_STRIDE), :] for l in range(CMP_STRIDE)]
            per_req.append(jnp.concatenate(slabs, axis=1))
        lhs = jnp.concatenate(per_req, axis=0).astype(BF16)
        p = _dot(lhs, wc_ref[c])
        bias = _dot_hi(per_ref[c], wphi_ref[c])[0:1]
        hi = pltpu.roll(p[:, 128:], nr * m - 1, 0)
        outs.append(p[:, :128] + hi + bias)
    res = jnp.concatenate(outs, axis=1)
    o_ref[...] = res.reshape(nr, m, 256)


def _compress(page_table, src, wc, pe_rep, wphi2, *, col0, nr, layer):
    nb, n_pages = page_table.shape
    rows = n_pages * PAGE
    m = rows // CMP_STRIDE
    if layer is None:
        scratch = [pltpu.VMEM((2, nr, 2, rows, 128), F32), pltpu.SemaphoreType.DMA((2,))]
    else:
        scratch = [pltpu.VMEM((2, nr, 2, NSA_KV_HEADS, HEAD_DIM, rows), F32), pltpu.SemaphoreType.DMA((2,)),
                   pltpu.VMEM((nr, 2, rows, 128), F32)]
    grid_spec = pltpu.PrefetchScalarGridSpec(
        num_scalar_prefetch=1,
        grid=(nb // nr,),
        in_specs=[
            pl.BlockSpec(memory_space=pl.ANY),
            pl.BlockSpec((2, CMP_STRIDE * 128, 256), lambda i, pt: (0, 0, 0)),
            pl.BlockSpec((2, 8, L_CMP * HEAD_DIM), lambda i, pt: (0, 0, 0)),
            pl.BlockSpec((2, L_CMP * HEAD_DIM, 128), lambda i, pt: (0, 0, 0)),
        ],
        out_specs=pl.BlockSpec((nr, m, 256), lambda i, pt: (i, 0, 0)),
        scratch_shapes=scratch,
    )
    return pl.pallas_call(
        functools.partial(_compress_kernel, nr=nr, n_pages=n_pages, col0=col0, layer=layer),
        grid_spec=grid_spec,
        out_shape=jax.ShapeDtypeStruct((nb, m, 256), F32),
        compiler_params=_cparams(("arbitrary",)),
        name="compress",
    )(page_table.reshape(-1), src, wc, pe_rep, wphi2)


def _select_blocks_t(imp_t, pos, n_slc):
    ns, c = imp_t.shape
    s_i = lax.broadcasted_iota(jnp.int32, (ns, c), 0)
    cur = lax.shift_right_logical(pos, 6)
    valid = (s_i * L_SEL) <= pos
    forced = (s_i == 0) | (s_i == cur) | (s_i == cur - 1)
    vals = jnp.where(valid, imp_t + jnp.where(forced, FORCE_BONUS, 0.0), -1.0)
    vals = jnp.where(s_i < n_slc, vals, LOWEST)
    rank = jnp.zeros((ns, c), F32)
    for sp in range(n_slc):
        row = vals[sp:sp + 1, :]
        beats = (row > vals) | ((row == vals) & (s_i > sp))
        rank = rank + jnp.where(beats, 1.0, 0.0)
    return jnp.where((rank < min(N_SEL, n_slc)) & (s_i < n_slc), 1.0, 0.0)


def _importance_t(ovl_t, p_sum):
    return functools.reduce(lambda a, b: a + b, [_dot(ovl_t, part) for part in _split3(p_sum)])


def _expand_gates(ng, eg_ref):
    parts = _split3(ng)
    return [functools.reduce(lambda a, b: a + b, [_dot(part, eg_ref[t]) for part in parts]) for t in range(3)]


def _pad_rows(x, rows):
    if x.shape[0] == rows:
        return x
    return jnp.concatenate([x, jnp.zeros((rows - x.shape[0],) + x.shape[1:], x.dtype)], axis=0)


def _nsa_prompt_kernel(q_ref, pg_ref, wn_ref, kvc_ref, bct_ref, toep_ref, ovl_ref, ef_ref, eg_ref, *rest, n_slc):
    o_ref, kvo_ref, wno_ref, ks_ref, vst_ref, kw_ref, vwt_ref, mft_ref = rest[-8:]
    qt = pl.program_id(1)
    t_len = pg_ref.shape[1]
    nj = kvc_ref.shape[1]
    keep0 = t_len - wno_ref.shape[5]

    @pl.when(qt == 0)
    def _():
        ks_ref[...] = pg_ref[0, :, 256:384].astype(BF16)
        kw_ref[...] = wn_ref[0, :, 0:128].astype(BF16)
        for blk in range(t_len // TQ):
            rs = slice(blk * TQ, (blk + 1) * TQ)
            for slot in range(4):
                tile = pg_ref[0, rs, slot * 128:(slot + 1) * 128].T
                kvo_ref[0, 0, slot, :, :, rs] = tile.reshape(NSA_KV_HEADS, HEAD_DIM, TQ)
                if slot == 3:
                    vst_ref[:, rs] = tile.astype(BF16)
            for c in range(2):
                tile = wn_ref[0, rs, c * 128:(c + 1) * 128].T
                if blk * TQ >= keep0:
                    ws = slice(blk * TQ - keep0, (blk + 1) * TQ - keep0)
                    wno_ref[0, 0, c, :, :, ws] = tile.reshape(NSA_KV_HEADS, HEAD_DIM, TQ)
                if c == 1:
                    vwt_ref[:, rs] = tile.astype(BF16)

    qts = [q_ref[0, :, h * 128:(h + 1) * 128].astype(F32).T.astype(BF16) for h in range(NSA_HEADS)]
    groups = [h // NSA_REP for h in range(NSA_HEADS)]

    def score_dots(k_ref, kts):
        starts = [pl.multiple_of(kt * TQ, TQ) for kt in kts]
        ks = [k_ref[pl.ds(st, TQ), :] for st in starts]
        return starts, [[_dot(k, qts[h]) for k in ks] for h in range(NSA_HEADS)]

    all_heads = tuple(range(NSA_HEADS))

    def softmax_update(raw, starts, tbls, masked, carry, heads=all_heads):
        stats, probs = [], []
        for h in heads:
            g = groups[h]
            m_i, l_i, _ = carry[h]
            ss = []
            for s, st, tbl in zip(raw[h], starts, tbls):
                s = s + toep_ref[tbl, h]
                if masked:
                    s = s + mft_ref[pl.ds(st, TQ), g * TQ:(g + 1) * TQ]
                ss.append(s)
            mx = functools.reduce(jnp.maximum, [jnp.max(s, axis=0, keepdims=True) for s in ss])
            m_new = jnp.maximum(m_i, mx)
            alpha = jnp.exp2(m_i - m_new)
            ps = [jnp.exp2(s - m_new) for s in ss]
            l_new = alpha * l_i + functools.reduce(
                lambda a, b: a + b, [jnp.sum(p, axis=0, keepdims=True) for p in ps])
            stats.append((m_new, l_new, alpha))
            probs.append(jnp.concatenate(ps, axis=0).astype(BF16))
        return stats, probs

    def value_dots(vt_ref, starts, stats, probs, carry, heads=all_heads):
        vts = [vt_ref[:, pl.ds(st, TQ)] for st in starts]
        new = []
        for i, h in enumerate(heads):
            g = groups[h]
            m_new, l_new, alpha = stats[i]
            vt = jnp.concatenate([v[g * HEAD_DIM:(g + 1) * HEAD_DIM, :] for v in vts], axis=1)
            new.append((m_new, l_new, alpha * carry[h][2] + _dot(vt, probs[i])))
        return tuple(new)

    init = tuple((jnp.full((1, TQ), NEG_INF, F32), jnp.zeros((1, TQ), F32),
                  jnp.zeros((HEAD_DIM, TQ), F32)) for _ in range(NSA_HEADS))

    def finish(state):
        return [acc * (1.0 / l_f) for (_, l_f, acc) in state]

    w_tiles = WINDOW // TQ + 1
    w_kts = [qt - (w_tiles - 1) + u for u in range(w_tiles)]
    w_tbls = [jnp.where(kt < 0, 4, tb) for kt, tb in zip(w_kts, (3, 1, 0))]
    w_starts, raw_w = score_dots(kw_ref, [jnp.maximum(kt, 0) for kt in w_kts])
    kc = kvc_ref[0, :, 0:128].astype(BF16)
    vct = kvc_ref[0, :, 128:256].T.astype(BF16)
    raw_c = [_dot(kc, qts[h]) for h in range(NSA_HEADS)]
    first_kts = list(range(KEY_UNROLL))
    s_starts0, raw_s0 = score_dots(ks_ref, first_kts)

    stats_w, probs_w = softmax_update(raw_w, w_starts, w_tbls, False, init)
    p_c, p_sum = [], [None] * NSA_KV_HEADS
    for h, g in enumerate(groups):
        s = raw_c[h] + bct_ref[h]
        mx = jnp.max(s, axis=0, keepdims=True)
        e = jnp.exp2(s - mx)
        p = jnp.where(mx > 0.5 * NEG_INF, e * (1.0 / jnp.sum(e, axis=0, keepdims=True)), 0.0)
        p_c.append(p.astype(BF16))
        p_sum[g] = p if p_sum[g] is None else p_sum[g] + p

    o_w = finish(value_dots(vwt_ref, w_starts, stats_w, probs_w, init))
    o_c = [_dot(vct[g * HEAD_DIM:(g + 1) * HEAD_DIM, :], p_c[h]) for h, g in enumerate(groups)]
    imp_t = _importance_t(ovl_ref[...], jnp.concatenate(p_sum, axis=1))
    ns = -(-n_slc // 8) * 8
    pos = qt * TQ + lax.rem(lax.broadcasted_iota(jnp.int32, (1, NSA_KV_HEADS * TQ), 1), TQ)
    sel_t = _select_blocks_t(imp_t[:ns], pos, n_slc)
    hidden = ((_pad_rows(sel_t, 128) - 1.0) * (-NEG_INF)).astype(BF16)
    mft_ref[...] = _dot(ef_ref[...], hidden)

    def sel_body(it, carry):
        kts = [it * KEY_UNROLL + u for u in range(KEY_UNROLL)]
        tbls = [jnp.where(qt - kt < 0, 4, jnp.minimum(qt - kt, 2)) for kt in kts]
        starts, raw = score_dots(ks_ref, kts)
        stats, probs = softmax_update(raw, starts, tbls, True, carry)
        return value_dots(vst_ref, starts, stats, probs, carry)

    tbls0 = [jnp.where(qt - kt < 0, 4, jnp.minimum(qt - kt, 2)) for kt in first_kts]
    stats0, probs0 = softmax_update(raw_s0, s_starts0, tbls0, True, init)
    state0 = value_dots(vst_ref, s_starts0, stats0, probs0, init)
    o_s = finish(lax.fori_loop(1, (qt + KEY_UNROLL) // KEY_UNROLL, sel_body, state0))

    start = pl.multiple_of(qt * TQ, TQ)
    gates = _expand_gates(wn_ref[0, pl.ds(start, TQ), 256:384], eg_ref)
    tiles = []
    for mt in range(NSA_HEADS // 2):
        acc = jnp.zeros((TQ, 128), F32)
        for o, gate in zip((o_c, o_s, o_w), gates):
            pair = jnp.concatenate([o[2 * mt], o[2 * mt + 1]], axis=0)
            acc = acc + gate[:, mt * 128:(mt + 1) * 128] * pair.T
        tiles.append(acc)
    o_ref[0] = jnp.concatenate(tiles, axis=1)


def _nsa_prompt(z32, z16, kvc, bias_ct, toep_t, ovl_t, ef_t, eg, layer, n_layers, kv_out, win_out):
    b, t, _ = z32.shape
    nj = kvc.shape[1]
    n_slc = -(-t // L_SEL)
    keep = min(WINDOW, t)
    kv_shape = (n_layers, b, 4, NSA_KV_HEADS, HEAD_DIM, t)
    win_shape = (n_layers, b, 2, NSA_KV_HEADS, HEAD_DIM, keep)
    args = [z16, z32, z32, kvc, bias_ct, toep_t, ovl_t, ef_t, eg]
    extra_specs, aliases = [], {}
    if kv_out is not None:
        extra_specs = [pl.BlockSpec(memory_space=pl.ANY)] * 2
        aliases = {len(args): 1, len(args) + 1: 2}
        args += [kv_out, win_out]
    return pl.pallas_call(
        functools.partial(_nsa_prompt_kernel, n_slc=n_slc),
        grid=(b, t // TQ),
        in_specs=[
            pl.BlockSpec((1, TQ, 1024), lambda i, j: (i, j, CB16_Q)),
            pl.BlockSpec((1, t, TN), lambda i, j: (i, 0, CB32_PAGED)),
            pl.BlockSpec((1, t, TN), lambda i, j: (i, 0, CB32_WN)),
            pl.BlockSpec((1, nj, 256), lambda i, j: (i, 0, 0)),
            pl.BlockSpec((NSA_HEADS, nj, TQ), lambda i, j: (0, 0, j)),
            pl.BlockSpec((5, NSA_HEADS, TQ, TQ), lambda i, j: (0, 0, 0, 0)),
            pl.BlockSpec((128, nj), lambda i, j: (0, 0)),
            pl.BlockSpec((t, 128), lambda i, j: (0, 0)),
            pl.BlockSpec((3, 128, 512), lambda i, j: (0, 0, 0)),
        ] + extra_specs,
        out_specs=[pl.BlockSpec((1, TQ, 512), lambda i, j: (i, j, 0)),
                   pl.BlockSpec((1, 1) + kv_shape[2:], lambda i, j: (layer, i, 0, 0, 0, 0)),
                   pl.BlockSpec((1, 1) + win_shape[2:], lambda i, j: (layer, i, 0, 0, 0, 0))],
        out_shape=[jax.ShapeDtypeStruct((b, t, 512), F32), jax.ShapeDtypeStruct(kv_shape, F32),
                   jax.ShapeDtypeStruct(win_shape, F32)],
        input_output_aliases=aliases,
        scratch_shapes=[pltpu.VMEM((t, 128), BF16), pltpu.VMEM((128, t), BF16),
                        pltpu.VMEM((t, 128), BF16), pltpu.VMEM((128, t), BF16),
                        pltpu.VMEM((t, NSA_KV_HEADS * TQ), F32)],
        compiler_params=_cparams(("arbitrary", "arbitrary")),
        name="nsa_prompt",
    )(*args)


def _slc_copies(pt_ref, src_ref, buf_ref, sem_ref, step, slot, *, nr, n_pages, layer):
    copies = []
    for r in range(nr):
        for p in range(n_pages):
            pid = pt_ref[(step * nr + r) * n_pages + p]
            copies.append(pltpu.make_async_copy(
                src_ref.at[layer, pid, pl.ds(2, 2)],
                buf_ref.at[slot, r, :, :, :, pl.ds(p * PAGE, PAGE)],
                sem_ref.at[slot]))
    return copies


def _nsa_sample_kernel(pt_ref, src_ref, q_ref, pg_ref, wn_ref, cw_ref, kvc_ref, bs_ref, bn_ref, bw_ref,
                       bcs_ref, ovl_ref, ef_ref, eg_ref, *rest, nr, n_pages, n_slc, pos0, layer):
    o_ref, wo_ref, buf_ref, sem_ref = rest[-4:]
    _nsa_sample_body(pt_ref, src_ref, q_ref, pg_ref, wn_ref, cw_ref, kvc_ref, bs_ref, bn_ref, bw_ref,
                     bcs_ref, ovl_ref, ef_ref, eg_ref, o_ref, wo_ref, buf_ref, sem_ref,
                     nr=nr, n_pages=n_pages, n_slc=n_slc, pos0=pos0, layer=layer)


def _nsa_sample_body(pt_ref, src_ref, q_ref, pg_ref, wn_ref, cw_ref, kvc_ref, bs_ref, bn_ref, bw_ref,
                     bcs_ref, ovl_ref, ef_ref, eg_ref, o_ref, wo_ref, buf_ref, sem_ref,
                     *, nr, n_pages, n_slc, pos0, layer):
    step = pl.program_id(0)
    n_steps = pl.num_programs(0)
    slot = lax.rem(step, 2)
    tq = q_ref.shape[1]
    rows = NSA_HEADS * tq
    nj = kvc_ref.shape[1]
    past = n_pages * PAGE
    wb = cw_ref.shape[5]
    reqs = range(nr)
    kw = dict(nr=nr, n_pages=n_pages, layer=layer)

    @pl.when(step == 0)
    def _():
        for cp in _slc_copies(pt_ref, src_ref, buf_ref, sem_ref, step, slot, **kw):
            cp.start()

    @pl.when(step + 1 < n_steps)
    def _():
        for cp in _slc_copies(pt_ref, src_ref, buf_ref, sem_ref, step + 1, 1 - slot, **kw):
            cp.start()

    for cp in _slc_copies(pt_ref, src_ref, buf_ref, sem_ref, step, slot, **kw):
        cp.wait()

    def new_rows(x):
        return _pad_rows(x, 128).astype(BF16)

    def row_max(*xs):
        return functools.reduce(jnp.maximum, [jnp.max(x, axis=-1, keepdims=True) for x in xs])

    def row_sum(*xs):
        return functools.reduce(lambda a, b: a + b, [jnp.sum(x, axis=-1, keepdims=True) for x in xs])

    qs, pgs, wns, raw = [], [], [], []
    for r in reqs:
        qf = q_ref[r].astype(F32)
        q = jnp.concatenate([qf[:, h * 128:(h + 1) * 128] for h in range(NSA_HEADS)], axis=0).astype(BF16)
        pg, wn = pg_ref[r], wn_ref[r]
        qs.append(q)
        pgs.append(pg)
        wns.append(wn)
        raw.append(dict(
            c=_dot_nt(q, kvc_ref[r, :, 0:128].astype(BF16)),
            w_old=_dot(q, cw_ref[0, r, 0].reshape(128, wb).astype(BF16)),
            w_new=_dot_nt(q, new_rows(wn[:, 0:128])),
            s_old=_dot(q, buf_ref[slot, r, 0].reshape(128, past).astype(BF16)),
            s_new=_dot_nt(q, new_rows(pg[:, 256:384]))))

    p_cs, win = [], []
    for r in reqs:
        s_c = raw[r]["c"] + bcs_ref[...]
        mx = row_max(s_c)
        e_c = jnp.exp2(s_c - mx)
        p_cs.append(jnp.where(mx > 0.5 * NEG_INF, e_c * (1.0 / row_sum(e_c)), 0.0))
        s_o, s_n = raw[r]["w_old"] + bw_ref[...], raw[r]["w_new"] + bn_ref[...]
        m2 = row_max(s_o, s_n)
        e_o, e_n = jnp.exp2(s_o - m2), jnp.exp2(s_n - m2)
        win.append((e_o.astype(BF16), e_n.astype(BF16), 1.0 / row_sum(e_o, e_n)))

    o_cs, o_ws, imps, gates = [], [], [], []
    for r in reqs:
        o_cs.append(_dot(p_cs[r].astype(BF16), kvc_ref[r, :, 128:256].astype(BF16)))
        e_o, e_n, inv = win[r]
        o_ws.append((_dot_nt(e_o, cw_ref[0, r, 1].reshape(128, wb).astype(BF16))
                     + _dot(e_n, new_rows(wns[r][:, 128:256]))) * inv)
        p_g = jnp.sum(p_cs[r].reshape(NSA_KV_HEADS, NSA_REP, tq, nj), axis=1).reshape(NSA_KV_HEADS * tq, nj)
        imps.append(functools.reduce(lambda a, b: a + b,
                                     [_dot_nt(ovl_ref[...], part) for part in _split3(_pad_rows(p_g, 128))]))
        gates.append(_expand_gates(wns[r][:, 256:384], eg_ref))

    ns = -(-n_slc // 8) * 8
    lane = lax.broadcasted_iota(jnp.int32, (1, 128), 1)
    mfs = []
    for r in reqs:
        sel_t = _select_blocks_t(imps[r][:ns], pos0 + lax.rem(lane, tq), n_slc)
        sel = _pad_rows(sel_t, 128).T[:NSA_KV_HEADS * tq]
        mfs.append(_dot(sel.astype(BF16), ef_ref[...]))

    sel_probs = []
    for r in reqs:
        mf = jnp.broadcast_to(mfs[r].reshape(NSA_KV_HEADS, 1, tq, past),
                              (NSA_KV_HEADS, NSA_REP, tq, past)).reshape(rows, past)
        s_o = raw[r]["s_old"] + jnp.where(mf > 0.5, bs_ref[...], NEG_INF)
        s_n = raw[r]["s_new"] + bn_ref[...]
        m2 = row_max(s_o, s_n)
        e_o, e_n = jnp.exp2(s_o - m2), jnp.exp2(s_n - m2)
        sel_probs.append((e_o.astype(BF16), e_n.astype(BF16), 1.0 / row_sum(e_o, e_n)))
    o_ss = []
    for r in reqs:
        e_o, e_n, inv = sel_probs[r]
        o_ss.append((_dot_nt(e_o, buf_ref[slot, r, 1].reshape(128, past).astype(BF16))
                     + _dot(e_n, new_rows(pgs[r][:, 384:512]))) * inv)

    lane_w = lax.broadcasted_iota(jnp.int32, (128, wb), 1)
    for r in reqs:
        for c in range(2):
            kept = pltpu.roll(cw_ref[0, r, c].reshape(128, wb), wb - tq, 1)
            fresh = pltpu.roll(_pad_rows(wns[r][:, c * 128:(c + 1) * 128], 128).T, 128 - tq, 1)
            if wb > 128:
                fresh = jnp.concatenate([jnp.zeros((128, wb - 128), F32), fresh], axis=1)
            wo_ref[0, r, c] = jnp.where(lane_w >= wb - tq, fresh, kept).reshape(NSA_KV_HEADS, HEAD_DIM, wb)

    lane128 = lax.broadcasted_iota(jnp.int32, (tq, 128), 1)
    for r in reqs:
        tiles = []
        for mt in range(NSA_HEADS // 2):
            g = (2 * mt) // NSA_REP
            acc = jnp.zeros((tq, 128), F32)
            for o, gate in zip((o_cs[r], o_ss[r], o_ws[r]), gates[r]):
                a = o[(2 * mt) * tq:(2 * mt + 1) * tq]
                b = o[(2 * mt + 1) * tq:(2 * mt + 2) * tq]
                if g == 1:
                    a = pltpu.roll(a, 64, 1)
                else:
                    b = pltpu.roll(b, 64, 1)
                acc = acc + gate[:, mt * 128:(mt + 1) * 128] * jnp.where(lane128 < 64, a, b)
            tiles.append(acc)
        o_ref[r] = jnp.concatenate(tiles, axis=1)


def _nsa_sample(page_table, cache_t, cwin_t, layer, z32, z16, kvc, bias_s, bias_n, bias_w, bias_cs, ovl_t, ef, eg,
                nr, win_out):
    nb, n_pages = page_table.shape
    past = n_pages * PAGE
    tq = z32.shape[1]
    rows = NSA_HEADS * tq
    nj = kvc.shape[1]
    n_slc = -(-(past + tq) // L_SEL)
    wb = cwin_t.shape[5]
    const2 = lambda i, pt: (0, 0)
    win_block = pl.BlockSpec((1, nr, 2, NSA_KV_HEADS, HEAD_DIM, wb), lambda i, pt: (layer, i, 0, 0, 0, 0))
    args = [page_table.reshape(-1), cache_t, z16, z32, z32, cwin_t, kvc, bias_s, bias_n, bias_w, bias_cs,
            ovl_t, ef, eg]
    extra_specs, aliases = [], {}
    if win_out is not None:
        extra_specs = [pl.BlockSpec(memory_space=pl.ANY)]
        aliases = {len(args): 1}
        args.append(win_out)
    grid_spec = pltpu.PrefetchScalarGridSpec(
        num_scalar_prefetch=1,
        grid=(nb // nr,),
        in_specs=[
            pl.BlockSpec(memory_space=pl.ANY),
            pl.BlockSpec((nr, tq, 1024), lambda i, pt: (i, 0, CB16_Q)),
            pl.BlockSpec((nr, tq, TN), lambda i, pt: (i, 0, CB32_PAGED)),
            pl.BlockSpec((nr, tq, TN), lambda i, pt: (i, 0, CB32_WN)),
            win_block,
            pl.BlockSpec((nr, nj, 256), lambda i, pt: (i, 0, 0)),
            pl.BlockSpec((rows, past), const2),
            pl.BlockSpec((rows, 128), const2),
            pl.BlockSpec((rows, wb), const2),
            pl.BlockSpec((rows, nj), const2),
            pl.BlockSpec((128, nj), const2),
            pl.BlockSpec((128, past), const2),
            pl.BlockSpec((3, 128, 512), lambda i, pt: (0, 0, 0)),
        ] + extra_specs,
        out_specs=[pl.BlockSpec((nr, tq, 512), lambda i, pt: (i, 0, 0)), win_block],
        scratch_shapes=[pltpu.VMEM((2, nr, 2, NSA_KV_HEADS, HEAD_DIM, past), F32),
                        pltpu.SemaphoreType.DMA((2,))],
    )
    return pl.pallas_call(
        functools.partial(_nsa_sample_kernel, nr=nr, n_pages=n_pages, n_slc=n_slc, pos0=past, layer=layer),
        grid_spec=grid_spec,
        out_shape=[jax.ShapeDtypeStruct((nb, tq, 512), F32), jax.ShapeDtypeStruct(cwin_t.shape, F32)],
        input_output_aliases=aliases,
        compiler_params=_cparams(("arbitrary",)),
        name="nsa_sample",
    )(*args)


def _mem_attn_kernel(q_ref, kv_ref, o_ref):
    q = q_ref[0]
    heads = range(MEM_HEADS)
    scores = [_dot_nt(q[:, h * 128:(h + 1) * 128], kv_ref[0, :, h * 128:(h + 1) * 128].astype(BF16))
              for h in heads]
    probs = []
    for s in scores:
        s = s * (MEM_HEAD_DIM ** -0.5)
        e = jnp.exp(s - jnp.max(s, axis=-1, keepdims=True))
        probs.append((e * (1.0 / jnp.sum(e, axis=-1, keepdims=True))).astype(BF16))
    o_ref[0] = jnp.concatenate(
        [_dot(probs[h], kv_ref[0, :, MEM_WIDTH + h * 128:MEM_WIDTH + (h + 1) * 128].astype(BF16))
         for h in heads], axis=1)


def _mem_attn(z16, mem_kv, tq):
    b, t, _ = z16.shape
    n_mem = mem_kv.shape[1]
    return pl.pallas_call(
        _mem_attn_kernel,
        grid=(b, t // tq),
        in_specs=[
            pl.BlockSpec((1, tq, TN), lambda i, j: (i, j, CB16_MQ)),
            pl.BlockSpec((1, n_mem, 2 * MEM_WIDTH), lambda i, j: (i, 0, 0)),
        ],
        out_specs=pl.BlockSpec((1, tq, MEM_WIDTH), lambda i, j: (i, j, 0)),
        out_shape=jax.ShapeDtypeStruct((b, t, MEM_WIDTH), F32),
        compiler_params=_cparams(("parallel", "parallel")),
        name="mem_attn",
    )(z16, mem_kv)


def _mem_attn_cache_kernel(q_ref, kv_ref, o_ref):
    nr, tq, _ = q_ref.shape
    n_mem = kv_ref.shape[2] // (2 * MEM_HEADS)
    pairs = [(r, h) for r in range(nr) for h in range(MEM_HEADS)]
    qs = [q_ref[r].astype(F32) for r in range(nr)]
    scores = []
    for r, h in pairs:
        kh = kv_ref[0, r, pl.ds(h, n_mem, stride=2 * MEM_HEADS), :].astype(BF16)
        scores.append(_dot_nt(qs[r][:, h * 128:(h + 1) * 128].astype(BF16), kh))
    s = jnp.concatenate(scores, axis=0) * (MEM_HEAD_DIM ** -0.5)
    mx = jnp.max(s, axis=-1, keepdims=True)
    e = jnp.exp(s - mx)
    p = e * (1.0 / jnp.sum(e, axis=-1, keepdims=True))
    outs = []
    for i, (r, h) in enumerate(pairs):
        vh = kv_ref[0, r, pl.ds(MEM_HEADS + h, n_mem, stride=2 * MEM_HEADS), :].astype(BF16)
        outs.append(_dot(p[i * tq:(i + 1) * tq].astype(BF16), vh))
    for r in range(nr):
        o_ref[r] = jnp.concatenate(outs[r * MEM_HEADS:(r + 1) * MEM_HEADS], axis=1)


def _mem_attn_cache(z16, cache_mem, layer, nr):
    b, tq, _ = z16.shape
    nl, _, n_mem = cache_mem.shape[:3]
    rows = n_mem * 2 * MEM_HEADS
    return pl.pallas_call(
        _mem_attn_cache_kernel,
        grid=(b // nr,),
        in_specs=[
            pl.BlockSpec((nr, tq, TN), lambda i: (i, 0, CB16_MQ)),
            pl.BlockSpec((1, nr, rows, MEM_HEAD_DIM), lambda i: (layer, i, 0, 0)),
        ],
        out_specs=pl.BlockSpec((nr, tq, MEM_WIDTH), lambda i: (i, 0, 0)),
        out_shape=jax.ShapeDtypeStruct((b, tq, MEM_WIDTH), F32),
        compiler_params=_cparams(("parallel",)),
        name="mem_attn_cache",
    )(z16, cache_mem.reshape(nl, b, rows, MEM_HEAD_DIM))


def _merge_kernel(x_ref, u_ref, v_ref, g0_ref, g1_ref, g2_ref, on_ref, om_ref, wsp_ref, bsp_ref,
                  wa_ref, wb_ref, wc_ref, wo_ref, o_ref):
    tm = x_ref.shape[0]
    u = u_ref[...].astype(F32)
    v = v_ref[...].astype(BF16)
    chunks = []
    for c in range(tm // CHUNK):
        parts = []
        for g in range(A_GROUPS):
            vg = v[c * CHUNK:(c + 1) * CHUNK, g * 128:(g + 1) * 128]
            parts.append(_dot(wsp_ref[g], vg) + bsp_ref[g])
        chunks.append(jnp.concatenate(parts, axis=1))
    ya = (u * jnp.concatenate(chunks, axis=0)).astype(BF16)
    merged = (g0_ref[...].astype(F32) * _dot(ya, wa_ref[...])
              + g1_ref[...].astype(F32) * _dot(on_ref[...].astype(BF16), wb_ref[...])
              + g2_ref[...].astype(F32) * _dot(om_ref[...].astype(BF16), wc_ref[...]))
    o_ref[...] = x_ref[...] + _dot(merged.astype(BF16), wo_ref[...])


def _merge(x, z32, z16, o_nsa, o_mem, wsp, bsp, w_a, w_b, w_c, w_out, tm):
    m, d = x.shape
    row = lambda i: (i, 0)
    c2 = lambda i: (0, 0)
    c3 = lambda i: (0, 0, 0)
    return pl.pallas_call(
        _merge_kernel,
        grid=(m // tm,),
        in_specs=[
            pl.BlockSpec((tm, d), row),
            pl.BlockSpec((tm, TN), lambda i: (i, CB16_U)),
            pl.BlockSpec((tm, TN), lambda i: (i, CB32_V)),
            pl.BlockSpec((tm, d), lambda i: (i, CB16_BG)),
            pl.BlockSpec((tm, d), lambda i: (i, CB16_BG + 1)),
            pl.BlockSpec((tm, d), lambda i: (i, CB16_BG + 2)),
            pl.BlockSpec((tm, 512), row),
            pl.BlockSpec((tm, 512), row),
            pl.BlockSpec((A_GROUPS, CHUNK, CHUNK), c3),
            pl.BlockSpec((A_GROUPS, CHUNK, 128), c3),
            pl.BlockSpec((A_WIDTH, d), c2),
            pl.BlockSpec((512, d), c2),
            pl.BlockSpec((MEM_WIDTH, d), c2),
            pl.BlockSpec((d, d), c2),
        ],
        out_specs=pl.BlockSpec((tm, d), row),
        out_shape=jax.ShapeDtypeStruct((m, d), F32),
        compiler_params=_cparams(("parallel",)),
        name="merge",
    )(x, z16, z32, z16, z16, z16, o_nsa, o_mem, wsp, bsp, w_a, w_b, w_c, w_out)


def _mlp_kernel(x_ref, g_ref, wu_ref, wd_ref, gf_ref, o_ref, hn_ref, acc_ref, *, final):
    f = pl.program_id(1)

    @pl.when(f == 0)
    def _():
        hn_ref[...] = _rms(x_ref[...], g_ref[...]).astype(BF16)
        acc_ref[...] = jnp.zeros_like(acc_ref)

    h = jnp.square(jnp.maximum(_dot(hn_ref[...], wu_ref[...]), 0.0))
    acc_ref[...] += _dot(h.astype(BF16), wd_ref[...])

    @pl.when(f == pl.num_programs(1) - 1)
    def _():
        y = x_ref[...] + acc_ref[...]
        o_ref[...] = _rms(y, gf_ref[...]) if final else y


def _mlp(x, gain, w_up, w_down, out_gain, final, tm, tf):
    m, d = x.shape
    ff = w_up.shape[1]
    return pl.pallas_call(
        functools.partial(_mlp_kernel, final=final),
        grid=(m // tm, ff // tf),
        in_specs=[
            pl.BlockSpec((tm, d), lambda i, j: (i, 0)),
            pl.BlockSpec((1, d), lambda i, j: (0, 0)),
            pl.BlockSpec((d, tf), lambda i, j: (0, j)),
            pl.BlockSpec((tf, d), lambda i, j: (j, 0)),
            pl.BlockSpec((1, d), lambda i, j: (0, 0)),
        ],
        out_specs=pl.BlockSpec((tm, d), lambda i, j: (i, 0)),
        out_shape=jax.ShapeDtypeStruct((m, d), F32),
        scratch_shapes=[pltpu.VMEM((tm, d), BF16), pltpu.VMEM((tm, d), F32)],
        compiler_params=_cparams(("parallel", "arbitrary")),
        name="mlp",
    )(x, gain.reshape(1, d), w_up, w_down, out_gain.reshape(1, d))


def _relayout_w_in(w_in):
    nl, d, _ = w_in.shape
    wt = jnp.transpose(w_in, (0, 2, 1))
    wq = wt[:, OFF_Q:OFF_KV].reshape(nl, NSA_HEADS, 1, HEAD_DIM, d) * ((HEAD_DIM ** -0.5) * LOG2E)
    grp = (np.arange(NSA_HEADS) // NSA_REP)[:, None] == np.arange(NSA_KV_HEADS)[None, :]
    wq = jnp.where(jnp.asarray(grp)[None, :, :, None, None], wq, 0.0).reshape(nl, NSA_HEADS * 128, d)
    pad = jnp.zeros((nl, TN - 256 - (OFF_MQ - OFF_NG), d), w_in.dtype)
    half_a = A_WIDTH
    parts = [wt[:, half_a:OFF_Q], wt[:, OFF_KV:OFF_KV + 512], wt[:, OFF_KV + 512:OFF_MQ], pad,
             wt[:, OFF_BG:], wq, wt[:, :half_a], wt[:, OFF_MQ:OFF_BG]]
    out = jnp.concatenate(parts, axis=1).astype(BF16)
    assert out.shape[1] == Z32_COLS + Z16_COLS
    return out


def _compress_weights(w_phi, pe_cmp):
    nl = w_phi.shape[0]
    w5 = w_phi.reshape(nl, 2, CMP_STRIDE, 2, HEAD_DIM, HEAD_DIM)
    eye = jnp.eye(NSA_KV_HEADS, dtype=w_phi.dtype)
    wc = jnp.einsum("nhlcde,gk->nclgdhke", w5, eye).reshape(nl, 2, CMP_STRIDE * 128, 256).astype(BF16)
    wphi2 = jnp.transpose(w_phi, (0, 2, 1, 3, 4)).reshape(nl, 2, L_CMP * HEAD_DIM, HEAD_DIM)
    wphi2 = jnp.concatenate([wphi2, wphi2], axis=-1)
    pe2 = jnp.transpose(pe_cmp, (0, 2, 1, 3)).reshape(nl, 2, 1, L_CMP * HEAD_DIM)
    pe_rep = jnp.broadcast_to(pe2, (nl, 2, 8, L_CMP * HEAD_DIM))
    return wc, pe_rep, wphi2


def _overlap_t(nj, n_cmp, n_slc):
    ss = np.arange(128)[:, None]
    jj = np.arange(nj)[None, :]
    ov = ((jj * CMP_STRIDE <= (ss + 1) * L_SEL - 1) & (jj * CMP_STRIDE + L_CMP - 1 >= ss * L_SEL)
          & (jj < n_cmp) & (ss < n_slc))
    return jnp.asarray(ov.astype(np.float32)).astype(BF16)


def _block_expand(n_keys):
    e = np.arange(128)[:, None] == (np.arange(n_keys)[None, :] // L_SEL)
    return jnp.asarray(e.astype(np.float32)).astype(BF16)


def _gate_expand():
    e = np.zeros((3, 128, 512), np.float32)
    for t in range(3):
        for h in range(NSA_HEADS):
            e[t, 3 * h + t, h * HEAD_DIM:(h + 1) * HEAD_DIM] = 1.0
    return jnp.asarray(e).astype(BF16)


def _pick_tile(m, pref):
    t = min(m, pref)
    while m % t:
        t //= 2
    return t


def kernel(x_prompt, x_sample, cache_kv, cache_win_kv, cache_mem_kv, page_table, mem_prompt,
           ln1, w_in, g_v, w_s, b_s, w_a, pe_cmp, w_phi, w_b, ln_mem, w_mem_kv, w_c, w_out,
           ln2, w_up, w_down, rel_bias, ln_f):
    bp, t, d = x_prompt.shape
    bs, ts, _ = x_sample.shape
    n_pages = page_table.shape[1]
    past = n_pages * PAGE
    n_mem = mem_prompt.shape[1]
    wb = cache_win_kv.shape[2]
    assert t % (KEY_UNROLL * TQ) == 0 and t % CHUNK == 0 and past % L_SEL == 0 and ts <= L_SEL and CHUNK % ts == 0
    assert (past + ts - L_CMP) // CMP_STRIDE + 1 <= past // CMP_STRIDE
    assert wb == WINDOW and t >= WINDOW

    w_in_p = _relayout_w_in(w_in)
    wc_all, pe_rep_all, wphi2_all = _compress_weights(w_phi, pe_cmp)
    w_a16, w_b16, w_c16, w_out16 = (w.astype(BF16) for w in (w_a, w_b, w_c, w_out))
    w_up16, w_down16, w_mem16 = w_up.astype(BF16), w_down.astype(BF16), w_mem_kv.astype(BF16)
    causal = np.tril(np.ones((CHUNK, CHUNK), bool))
    wsp_p = jnp.where(jnp.asarray(causal), w_s, 0.0)
    blockdiag = np.kron(np.eye(CHUNK // ts), np.ones((ts, ts))) > 0
    wsp_s = jnp.where(jnp.asarray(blockdiag & causal),
                      jnp.tile(w_s[:, :, :ts, :ts], (1, 1, CHUNK // ts, CHUNK // ts)), 0.0)
    bsp_p = jnp.broadcast_to(b_s[:, :, :, None], b_s.shape + (128,))
    bsp_s = jnp.broadcast_to(jnp.tile(b_s[:, :, :ts], (1, 1, CHUNK // ts))[:, :, :, None], b_s.shape + (128,))
    wsp_p, wsp_s = wsp_p.astype(BF16), wsp_s.astype(BF16)

    cache_t = jnp.transpose(cache_kv, (0, 1, 3, 4, 5, 2))
    cwin_t = jnp.transpose(cache_win_kv, (0, 1, 3, 4, 5, 2))

    toep_t, bias_ct, bias_s, bias_n, bias_w, bias_cs = _tables(rel_bias, t, past, ts, wb)
    nj_p, nj_s = t // CMP_STRIDE, past // CMP_STRIDE
    ovl_p = _overlap_t(nj_p, (t - L_CMP) // CMP_STRIDE + 1, -(-t // L_SEL))
    ovl_s = _overlap_t(nj_s, (past + ts - L_CMP) // CMP_STRIDE + 1, -(-(past + ts) // L_SEL))
    ef_p_t = _block_expand(t).T
    ef_s = _block_expand(past)
    eg = _gate_expand()
    pt_prompt = jnp.arange(bp * (t // PAGE), dtype=jnp.int32).reshape(bp, t // PAGE)

    np_tok, ns_tok = bp * t, bs * ts
    tm_p = _pick_tile(np_tok, 1024)
    tm_in = _pick_tile(np_tok, 2048)
    tm_s = _pick_tile(ns_tok, 1024)
    xp = x_prompt.reshape(np_tok, d)
    xs = x_sample.reshape(ns_tok, d)
    mem_flat = mem_prompt.reshape(bp * n_mem, d)
    zeros_gv = jnp.zeros((TN,), F32)

    mem_p, kv_s, v_s = [], [], []
    win_out = jnp.zeros((DEPTH, bs, 2, NSA_KV_HEADS, HEAD_DIM, wb), F32)
    kv_out_p = jnp.zeros((DEPTH, bp, 4, NSA_KV_HEADS, HEAD_DIM, t), F32)
    win_out_p = jnp.zeros((DEPTH, bp, 2, NSA_KV_HEADS, HEAD_DIM, min(WINDOW, t)), F32)
    for l in range(DEPTH):
        mem_kv, _ = _norm_matmul(mem_flat, ln_mem[l], w_mem16[l], zeros_gv, ("none", "none"), 2,
                                 _pick_tile(bp * n_mem, 1024), "mem_kv_proj")
        zp32, zp16 = _norm_matmul(xp, ln1[l], w_in_p[l], g_v[l], TILE_KINDS, N_F32_TILES, tm_in, "in_proj",
                                  w_t=True)
        zp32_3, zp16_3 = zp32.reshape(bp, t, Z32_COLS), zp16.reshape(bp, t, Z16_COLS)
        kvc_p = _compress(pt_prompt, zp32.reshape(np_tok // PAGE, PAGE, Z32_COLS), wc_all[l], pe_rep_all[l],
                          wphi2_all[l], col0=COL_PAGED, nr=_pick_tile(bp, 4), layer=None)
        o_nsa, kv_out_p, win_out_p = _nsa_prompt(zp32_3, zp16_3, kvc_p, bias_ct, toep_t, ovl_p, ef_p_t, eg,
                                                 l, DEPTH, kv_out_p, win_out_p)
        o_mem = _mem_attn(zp16_3, mem_kv.reshape(bp, n_mem, 2 * MEM_WIDTH), _pick_tile(t, 512))
        x1 = _merge(xp, zp32, zp16, o_nsa.reshape(np_tok, 512), o_mem.reshape(np_tok, MEM_WIDTH), wsp_p[l],
                    bsp_p[l], w_a16[l], w_b16[l], w_c16[l], w_out16[l], _pick_tile(np_tok, 512))
        xp = _mlp(x1, ln2[l], w_up16[l], w_down16[l], ln_f, l == DEPTH - 1, tm_p, 1024)
        mem_p.append(mem_kv.reshape(bp, n_mem, 2, MEM_HEADS, MEM_HEAD_DIM))

        zs32, zs16 = _norm_matmul(xs, ln1[l], w_in_p[l], g_v[l], TILE_KINDS, N_F32_TILES, tm_s, "in_proj",
                                  w_t=True)
        zs32_3, zs16_3 = zs32.reshape(bs, ts, Z32_COLS), zs16.reshape(bs, ts, Z16_COLS)
        kvc_s = _compress(page_table, cache_t, wc_all[l], pe_rep_all[l], wphi2_all[l], col0=0,
                          nr=_pick_tile(bs, 4), layer=l)
        o_nsa, win_out = _nsa_sample(page_table, cache_t, cwin_t, l, zs32_3, zs16_3, kvc_s, bias_s, bias_n,
                                     bias_w, bias_cs, ovl_s, ef_s, eg, _pick_tile(bs, 4), win_out)
        o_mem = _mem_attn_cache(zs16_3, cache_mem_kv, l, _pick_tile(bs, 8))
        x1 = _merge(xs, zs32, zs16, o_nsa.reshape(ns_tok, 512), o_mem.reshape(ns_tok, MEM_WIDTH), wsp_s[l],
                    bsp_s[l], w_a16[l], w_b16[l], w_c16[l], w_out16[l], _pick_tile(ns_tok, 256))
        xs = _mlp(x1, ln2[l], w_up16[l], w_down16[l], ln_f, l == DEPTH - 1, tm_s, 1024)
        kv_s.append(zs32_3[:, :, COL_PAGED:COL_PAGED + 512].reshape(bs, ts, 4, NSA_KV_HEADS, HEAD_DIM))
        v_s.append(zs32_3[:, :, CB32_V * TN:(CB32_V + 1) * TN])

    y_prompt = xp.reshape(bp, t, d)
    y_sample = xs.reshape(bs, ts, d)
    to_rows = lambda a: jnp.transpose(a, (0, 1, 5, 2, 3, 4))
    return (y_prompt, y_sample, to_rows(kv_out_p), to_rows(win_out_p), jnp.stack(mem_p),
            jnp.stack(kv_s), to_rows(win_out), jnp.stack(v_s))
```
